```python
import jax, jax.numpy as jnp
from jax import lax
import numpy as np

D_MODEL = 1024
BATCH = 4
SEQ = 4096
DEPTH = 2

HEAD_DIM = 64
ROPE_THETA = 10000.0
LN_EPS = 1e-5
A_HEADS = 8
MOBA_BLOCK = 256
MOBA_TOPK = 3
MOBA_Q_CHUNK = 32
B_HEADS = 8
B_KV_HEADS = 2
SWA_WINDOW = 128
C_HEADS = 8
C_NOPE_DIM = 64
C_ROPE_DIM = 32
C_V_DIM = 64
C_KV_LATENT = 128
IDX_HEADS = 8
IDX_DIM = 32
DSA_TOPK = 256
DSA_Q_CHUNK = 128
D_HEADS = 8
SB_Q_BLOCK = 128

MIX_WIDTH = A_HEADS * HEAD_DIM + B_HEADS * HEAD_DIM
EVEN_SIZES = (A_HEADS * HEAD_DIM, A_HEADS * HEAD_DIM, A_HEADS * HEAD_DIM,
              B_HEADS * HEAD_DIM, B_KV_HEADS * HEAD_DIM, B_KV_HEADS * HEAD_DIM, MIX_WIDTH)
ODD_SIZES = (C_HEADS * C_NOPE_DIM, C_HEADS * C_ROPE_DIM, C_KV_LATENT, C_ROPE_DIM,
             IDX_HEADS * IDX_DIM, IDX_DIM, IDX_HEADS,
             D_HEADS * HEAD_DIM, D_HEADS * HEAD_DIM, D_HEADS * HEAD_DIM, MIX_WIDTH)
EVEN_IN = sum(EVEN_SIZES)
ODD_IN = sum(ODD_SIZES)
DEEPNORM_ALPHA = (2 * DEPTH) ** 0.25
DEEPNORM_BETA = (8 * DEPTH) ** -0.25

kernel_name = 'hybrid_moba_swa_dsa_stickbreak_deepnorm'


def _split(x, sizes):
    outs, off = [], 0
    for n in sizes:
        outs.append(x[..., off:off + n])
        off += n
    return outs


def _rope_tables(seq, dim):
    inv = 1.0 / (ROPE_THETA ** (jnp.arange(0, dim, 2, dtype=jnp.float32) / dim))
    ang = jnp.arange(seq, dtype=jnp.float32)[:, None] * inv[None, :]
    return jnp.cos(ang), jnp.sin(ang)


def _apply_rope(x, cos, sin):
    if x.ndim == 4:
        cos, sin = cos[:, None, :], sin[:, None, :]
    cos, sin = cos.astype(x.dtype), sin.astype(x.dtype)
    half = x.shape[-1] // 2
    x1, x2 = x[..., :half], x[..., half:]
    return jnp.concatenate([x1 * cos - x2 * sin, x2 * cos + x1 * sin], axis=-1)


def _layer_norm(x, g, b):
    xf = x.astype(jnp.float32)
    mu = jnp.mean(xf, axis=-1, keepdims=True)
    var = jnp.mean(jnp.square(xf - mu), axis=-1, keepdims=True)
    return ((xf - mu) * lax.rsqrt(var + LN_EPS) * g + b).astype(x.dtype)


def _rms_norm(x, g):
    xf = x.astype(jnp.float32)
    return (xf * lax.rsqrt(jnp.mean(jnp.square(xf), axis=-1, keepdims=True) + LN_EPS) * g).astype(x.dtype)


def _moba_attention(q, k, v):
    bsz, seq, nh, dh = q.shape
    nblk = -(-seq // MOBA_BLOCK)
    pad = nblk * MOBA_BLOCK - seq
    topk = min(MOBA_TOPK, nblk)

    def to_blocks(t):
        t = jnp.pad(t, ((0, 0), (0, pad), (0, 0), (0, 0)))
        return t.reshape(bsz, nblk, MOBA_BLOCK, nh, dh).transpose(0, 3, 1, 2, 4)

    kb, vb = to_blocks(k), to_blocks(v)
    k_mean = jnp.mean(kb.astype(jnp.float32), axis=3).astype(q.dtype)
    n_chunks = seq // MOBA_Q_CHUNK
    qc = q.transpose(0, 2, 1, 3).reshape(bsz, nh, n_chunks, MOBA_Q_CHUNK, dh).transpose(2, 0, 1, 3, 4)
    b_ix = jnp.arange(bsz)[:, None, None, None]
    h_ix = jnp.arange(nh)[None, :, None, None]
    blk_ids = jnp.arange(nblk)
    scale = dh ** -0.5

    def chunk(args):
        qi, ci = args
        t_pos = ci * MOBA_Q_CHUNK + jnp.arange(MOBA_Q_CHUNK)
        own = (ci * MOBA_Q_CHUNK) // MOBA_BLOCK
        gate = jnp.einsum('bhqd,bhnd->bhqn', qi, k_mean).astype(jnp.float32)
        gate = jnp.where(blk_ids < own, gate, -jnp.inf)
        _, sel = lax.top_k(gate, topk)
        sel_ok = sel < own
        k_sel = kb[b_ix, h_ix, sel]
        v_sel = vb[b_ix, h_ix, sel]
        s_sel = jnp.einsum('bhqd,bhqnkd->bhqnk', qi, k_sel).astype(jnp.float32) * scale
        s_sel = jnp.where(sel_ok[..., None], s_sel, -jnp.inf).reshape(bsz, nh, MOBA_Q_CHUNK, topk * MOBA_BLOCK)
        k_own = lax.dynamic_index_in_dim(kb, own, axis=2, keepdims=False)
        v_own = lax.dynamic_index_in_dim(vb, own, axis=2, keepdims=False)
        s_own = jnp.einsum('bhqd,bhkd->bhqk', qi, k_own).astype(jnp.float32) * scale
        k_pos = own * MOBA_BLOCK + jnp.arange(MOBA_BLOCK)
        s_own = jnp.where(k_pos[None, :] <= t_pos[:, None], s_own, -jnp.inf)
        p = jax.nn.softmax(jnp.concatenate([s_sel, s_own], axis=-1), axis=-1).astype(v.dtype)
        p_sel = p[..., :topk * MOBA_BLOCK].reshape(bsz, nh, MOBA_Q_CHUNK, topk, MOBA_BLOCK)
        p_own = p[..., topk * MOBA_BLOCK:]
        return (jnp.einsum('bhqnk,bhqnkd->bhqd', p_sel, v_sel)
                + jnp.einsum('bhqk,bhkd->bhqd', p_own, v_own))

    out = lax.map(chunk, (qc, jnp.arange(n_chunks)))
    return out.transpose(1, 0, 3, 2, 4).reshape(bsz, seq, nh, dh)


def _swa_sink_attention(q, k, v, sinks):
    bsz, seq, nq, dh = q.shape
    nkv = k.shape[2]
    grp = nq // nkv
    w = SWA_WINDOW
    nb = seq // w
    qb = q.reshape(bsz, nb, w, nkv, grp, dh)

    def band(t):
        tb = t.reshape(bsz, nb, w, nkv, dh)
        prev = jnp.pad(tb, ((0, 0), (1, 0), (0, 0), (0, 0), (0, 0)))[:, :-1]
        return jnp.concatenate([prev, tb], axis=2)

    kband, vband = band(k), band(v)
    s = jnp.einsum('bnqkgd,bnskd->bnkgqs', qb, kband).astype(jnp.float32) * dh ** -0.5
    qi = jnp.arange(w)[:, None]
    sj = jnp.arange(2 * w)[None, :]
    in_win = (sj > qi) & (sj <= qi + w)
    real = (jnp.arange(nb) > 0)[:, None, None] | (sj >= w)[None]
    mask = in_win[None] & real
    s = jnp.where(mask[None, :, None, None], s, -jnp.inf)
    sink = jnp.broadcast_to(sinks.astype(jnp.float32).reshape(nkv, grp)[None, None, :, :, None, None],
                            s.shape[:-1] + (1,))
    p = jax.nn.softmax(jnp.concatenate([s, sink], axis=-1), axis=-1)[..., :2 * w].astype(v.dtype)
    o = jnp.einsum('bnkgqs,bnskd->bnqkgd', p, vband)
    return o.reshape(bsz, seq, nq, dh)


def _dsa_attention(q_nope, q_rope, c_kv, k_rope, iq, ik, iw, w_uk, w_uv):
    bsz, seq, nh, _ = q_nope.shape
    n_sel = min(DSA_TOPK, seq // 4)
    nc = seq // DSA_Q_CHUNK
    q_cat = jnp.concatenate([jnp.einsum('bshd,hcd->bshc', q_nope, w_uk), q_rope], axis=-1)
    kv_cat = jnp.concatenate([c_kv, k_rope], axis=-1)
    scale = (C_NOPE_DIM + C_ROPE_DIM) ** -0.5
    idx_scale = (IDX_DIM * IDX_HEADS) ** -0.5
    s_pos = jnp.arange(seq)
    gather = jax.vmap(lambda kv, i: kv[i])

    def chunks(t):
        return t.reshape((bsz, nc, DSA_Q_CHUNK) + t.shape[2:]).swapaxes(0, 1)

    def chunk(args):
        qc, iqc, iwc, ci = args
        t_pos = ci * DSA_Q_CHUNK + jnp.arange(DSA_Q_CHUNK)
        rel = jax.nn.relu(jnp.einsum('bqhd,bsd->bqhs', iqc, ik))
        score = jnp.einsum('bqh,bqhs->bqs', iwc, rel).astype(jnp.float32) * idx_scale
        score = jnp.where(s_pos[None, None, :] <= t_pos[None, :, None], score, -jnp.inf)
        _, sel = lax.top_k(score, n_sel)
        sel_ok = sel <= t_pos[None, :, None]
        kv_sel = gather(kv_cat, sel)
        logits = jnp.einsum('bqhc,bqkc->bqhk', qc, kv_sel).astype(jnp.float32) * scale
        logits = jnp.where(sel_ok[:, :, None, :], logits, -jnp.inf)
        p = jax.nn.softmax(logits, axis=-1).astype(c_kv.dtype)
        return jnp.einsum('bqhk,bqkc->bqhc', p, kv_sel[..., :C_KV_LATENT])

    o_lat = lax.map(chunk, (chunks(q_cat), chunks(iq), chunks(iw), jnp.arange(nc)))
    o_lat = o_lat.swapaxes(0, 1).reshape(bsz, seq, nh, C_KV_LATENT)
    return jnp.einsum('bshc,hcd->bshd', o_lat, w_uv)


def _stick_breaking_attention(q, k, v):
    bsz, seq, nh, dh = q.shape
    nb = seq // SB_Q_BLOCK
    kh = k.transpose(0, 2, 1, 3)
    vh = v.transpose(0, 2, 1, 3)
    qc = q.transpose(0, 2, 1, 3).reshape(bsz, nh, nb, SB_Q_BLOCK, dh).transpose(2, 0, 1, 3, 4)
    s_pos = jnp.arange(seq)

    def block(args):
        qi, bi = args
        t_pos = bi * SB_Q_BLOCK + jnp.arange(SB_Q_BLOCK)
        past = s_pos[None, :] < t_pos[:, None]
        z = jnp.einsum('bhqd,bhsd->bhqs', qi, kh).astype(jnp.float32) * dh ** -0.5
        log_keep = jnp.where(past, jax.nn.log_sigmoid(-z), 0.0)
        log_after = lax.cumsum(log_keep, axis=3, reverse=True) - log_keep
        a = jnp.where(past, jnp.exp(jax.nn.log_sigmoid(z) + log_after), 0.0)
        return jnp.einsum('bhqs,bhsd->bhqd', a.astype(v.dtype), vh)

    out = lax.map(block, (qc, jnp.arange(nb)))
    return out.transpose(1, 0, 3, 2, 4).reshape(bsz, seq, nh, dh)


def _even_mixer(h, w_in, sinks, w_out, rope_h):
    bsz, seq, _ = h.shape
    cos_h, sin_h = rope_h
    aq, ak, av, bq, bk, bv, gate = _split(h @ w_in, EVEN_SIZES)
    aq = _apply_rope(aq.reshape(bsz, seq, A_HEADS, HEAD_DIM), cos_h, sin_h)
    ak = _apply_rope(ak.reshape(bsz, seq, A_HEADS, HEAD_DIM), cos_h, sin_h)
    av = av.reshape(bsz, seq, A_HEADS, HEAD_DIM)
    bq = _apply_rope(bq.reshape(bsz, seq, B_HEADS, HEAD_DIM), cos_h, sin_h)
    bk = _apply_rope(bk.reshape(bsz, seq, B_KV_HEADS, HEAD_DIM), cos_h, sin_h)
    bv = bv.reshape(bsz, seq, B_KV_HEADS, HEAD_DIM)
    oa = _moba_attention(aq, ak, av).reshape(bsz, seq, -1)
    ob = _swa_sink_attention(bq, bk, bv, sinks).reshape(bsz, seq, -1)
    o = jnp.concatenate([oa, ob], axis=-1) * jax.nn.silu(gate)
    return o @ w_out


def _odd_mixer(h, w_in, kv_norm_g, w_uk, w_uv, w_out, rope_r, rope_i):
    bsz, seq, _ = h.shape
    cos_r, sin_r = rope_r
    cos_i, sin_i = rope_i
    cqn, cqr, ckv, ckr, iq, ik, iw, dq, dk, dv, gate = _split(h @ w_in, ODD_SIZES)
    cqn = cqn.reshape(bsz, seq, C_HEADS, C_NOPE_DIM)
    cqr = _apply_rope(cqr.reshape(bsz, seq, C_HEADS, C_ROPE_DIM), cos_r, sin_r)
    ckv = _rms_norm(ckv, kv_norm_g)
    ckr = _apply_rope(ckr, cos_r, sin_r)
    iq = _apply_rope(iq.reshape(bsz, seq, IDX_HEADS, IDX_DIM), cos_i, sin_i)
    ik = _apply_rope(ik, cos_i, sin_i)
    oc = _dsa_attention(cqn, cqr, ckv, ckr, iq, ik, iw, w_uk, w_uv).reshape(bsz, seq, -1)
    od = _stick_breaking_attention(dq.reshape(bsz, seq, D_HEADS, HEAD_DIM),
                                   dk.reshape(bsz, seq, D_HEADS, HEAD_DIM),
                                   dv.reshape(bsz, seq, D_HEADS, HEAD_DIM)).reshape(bsz, seq, -1)
    o = jnp.concatenate([oc, od], axis=-1) * jax.nn.silu(gate)
    return o @ w_out


def setup_inputs(seed: int = 0) -> dict:
    key = jax.random.key(seed)
    ks = jax.random.split(key, 13)
    n_even = (DEPTH + 1) // 2
    n_odd = DEPTH // 2

    def nrm(k, shape, s):
        return jax.random.normal(k, shape, jnp.float32) * s

    return {
        'x': nrm(ks[0], (BATCH, SEQ, D_MODEL), 1.0),
        'c': nrm(ks[1], (BATCH, D_MODEL), 1.0),
        'w_ada': nrm(ks[2], (DEPTH, D_MODEL, 3 * D_MODEL), 0.1 * D_MODEL ** -0.5),
        'b_ada': nrm(ks[3], (DEPTH, 3 * D_MODEL), 0.01),
        'w_in_even': nrm(ks[4], (n_even, D_MODEL, EVEN_IN), D_MODEL ** -0.5),
        'sink_logits': nrm(ks[5], (n_even, B_HEADS), 1.0),
        'w_in_odd': nrm(ks[6], (n_odd, D_MODEL, ODD_IN), D_MODEL ** -0.5),
        'kv_norm_g': 1.0 + nrm(ks[7], (n_odd, C_KV_LATENT), 0.02),
        'w_uk': nrm(ks[8], (n_odd, C_HEADS, C_KV_LATENT, C_NOPE_DIM), C_KV_LATENT ** -0.5),
        'w_uv': nrm(ks[9], (n_odd, C_HEADS, C_KV_LATENT, C_V_DIM), C_KV_LATENT ** -0.5),
        'w_out': nrm(ks[10], (DEPTH, MIX_WIDTH, D_MODEL), MIX_WIDTH ** -0.5 * DEEPNORM_BETA),
        'ln_g': 1.0 + nrm(ks[11], (DEPTH, D_MODEL), 0.02),
        'ln_b': nrm(ks[12], (DEPTH, D_MODEL), 0.02),
    }


def reference(x, c, w_ada, b_ada, w_in_even, sink_logits, w_in_odd, kv_norm_g, w_uk, w_uv, w_out, ln_g, ln_b):
    seq = x.shape[1]
    rope_h = _rope_tables(seq, HEAD_DIM)
    rope_r = _rope_tables(seq, C_ROPE_DIM)
    rope_i = _rope_tables(seq, IDX_DIM)
    cond = jax.nn.silu(c)
    for layer in range(DEPTH):
        shift, scale, gate = jnp.split(cond @ w_ada[layer] + b_ada[layer], 3, axis=-1)
        h = x * (1.0 + scale[:, None, :]) + shift[:, None, :]
        if layer % 2 == 0:
            y = _even_mixer(h, w_in_even[layer // 2], sink_logits[layer // 2], w_out[layer], rope_h)
        else:
            j = layer // 2
            y = _odd_mixer(h, w_in_odd[j], kv_norm_g[j], w_uk[j], w_uv[j], w_out[layer], rope_r, rope_i)
        x = _layer_norm(DEEPNORM_ALPHA * x + (1.0 + gate[:, None, :]) * y, ln_g[layer], ln_b[layer])
    return x
```

```python
import functools

import jax
import jax.numpy as jnp
from jax import lax
from jax.experimental import pallas as pl
from jax.experimental.pallas import tpu as pltpu

D_MODEL = 1024
DEPTH = 2
HEAD_DIM = 64
ROPE_THETA = 10000.0
LN_EPS = 1e-5
A_HEADS = 8
MOBA_BLOCK = 256
MOBA_TOPK = 3
B_HEADS = 8
B_KV_HEADS = 2
SWA_WINDOW = 128
C_HEADS = 8
C_NOPE_DIM = 64
C_ROPE_DIM = 32
C_V_DIM = 64
C_KV_LATENT = 128
IDX_HEADS = 8
IDX_DIM = 32
DSA_TOPK = 256
D_HEADS = 8
MIX_WIDTH = 1024
EVEN_SIZES = (512, 512, 512, 512, 128, 128, 1024)
ODD_SIZES = (512, 256, 128, 32, 256, 32, 8, 512, 512, 512, 1024)
DEEPNORM_ALPHA = (2 * DEPTH) ** 0.25

LANES = 128
NEG_BIG = -1e30
INT_MIN = -2 ** 31
SB_UNDERFLOW = -104.0

PROJ_TM = 512
DSA_TQ = 128
DSA_CK = 512
SB_TQ = 128
VMEM_LIMIT = 48 * 1024 * 1024

_BF16 = jnp.bfloat16
_F32 = jnp.float32


def _cparams(sem):
    return pltpu.CompilerParams(dimension_semantics=sem, vmem_limit_bytes=VMEM_LIMIT)


def _dot_t(a, b):
    return lax.dot_general(a, b, (((1,), (1,)), ((), ())), preferred_element_type=_F32)


def _dot(a, b):
    return jnp.dot(a, b, preferred_element_type=_F32)


def _lane_iota(shape):
    return lax.broadcasted_iota(jnp.int32, shape, len(shape) - 1)


def _row_iota(shape):
    return lax.broadcasted_iota(jnp.int32, shape, len(shape) - 2)


def _ada_body(c_ref, w_ref, b_ref, o_ref):
    c = c_ref[...]
    cond = c * jax.nn.sigmoid(c)
    o_ref[0] = _dot(cond, w_ref[0]) + b_ref[0]


def _ada_call(c, w_ada, b_ada):
    depth, d, n3 = w_ada.shape
    bsz = c.shape[0]
    nb = n3 // d
    return pl.pallas_call(
        _ada_body,
        out_shape=jax.ShapeDtypeStruct((depth, bsz, n3), _F32),
        grid=(depth, nb),
        in_specs=[
            pl.BlockSpec((bsz, d), lambda l, j: (0, 0)),
            pl.BlockSpec((1, d, d), lambda l, j: (l, 0, j)),
            pl.BlockSpec((1, 1, d), lambda l, j: (l, 0, j)),
        ],
        out_specs=pl.BlockSpec((1, bsz, d), lambda l, j: (l, 0, j)),
        compiler_params=_cparams(("arbitrary", "arbitrary")),
        name="ada_mod",
    )(c, w_ada, b_ada.reshape(depth, 1, n3))


def _rope_tables(seq, dim, period_lanes):
    inv = 1.0 / (ROPE_THETA ** (jnp.arange(0, dim, 2, dtype=_F32) / dim))
    ang = jnp.arange(seq, dtype=_F32)[:, None] * inv[None, :]
    cos, sin = jnp.cos(ang), jnp.sin(ang)
    cos_h = jnp.concatenate([cos, cos], axis=1)
    sin_h = jnp.concatenate([-sin, sin], axis=1)
    reps = period_lanes // dim
    return jnp.tile(cos_h, (1, reps)), jnp.tile(sin_h, (1, reps))


def _rope_piece(x, cos, sin_signed, dim):
    half = dim // 2
    first = (_lane_iota(x.shape) % dim) < half
    partner = jnp.where(first, pltpu.roll(x, LANES - half, 1), pltpu.roll(x, half, 1))
    return x * cos + partner * sin_signed


def _modulated(x_ref, mod_ref):
    x = x_ref[0]
    shift = mod_ref[0, 0:1, :]
    scale = mod_ref[0, 1:2, :]
    return (x * (1.0 + scale) + shift).astype(_BF16)


EVEN_COLS = 3584


def _prep_w_even(w):
    offs = [0]
    for n in EVEN_SIZES:
        offs.append(offs[-1] + n)
    aq, ak, av, bq, bk, bv, gate = [w[:, offs[i]:offs[i + 1]] for i in range(len(EVEN_SIZES))]

    def dup(t):
        parts = []
        for g in range(B_KV_HEADS):
            blk = t[:, g * HEAD_DIM:(g + 1) * HEAD_DIM]
            parts += [blk, blk]
        return jnp.concatenate(parts, axis=1)

    return jnp.concatenate([aq, ak, bq, dup(bk), av, dup(bv), gate], axis=1).astype(_BF16)


def _inproj_even_body(x_ref, mod_ref, w_ref, cos_ref, sin_ref,
                      aq_ref, ak_ref, bq_ref, bk_ref, av_ref, bv_ref, g_ref):
    h = _modulated(x_ref, mod_ref)
    cos = cos_ref[...]
    sin = sin_ref[...]
    q_scale = HEAD_DIM ** -0.5

    def roped(col0, ncols, out_ref, scale):
        acc = _dot(h, w_ref[:, col0:col0 + ncols])
        for p in range(ncols // LANES):
            piece = _rope_piece(acc[:, p * LANES:(p + 1) * LANES], cos, sin, HEAD_DIM)
            if scale != 1.0:
                piece = piece * scale
            out_ref[0, :, p * LANES:(p + 1) * LANES] = piece.astype(_BF16)

    roped(0, 512, aq_ref, q_scale)
    roped(512, 512, ak_ref, 1.0)
    roped(1024, 512, bq_ref, q_scale)
    roped(1536, 256, bk_ref, 1.0)
    av_ref[0] = _dot(h, w_ref[:, 1792:2304]).astype(_BF16)
    bv_ref[0] = _dot(h, w_ref[:, 2304:2560]).astype(_BF16)
    for p in range(2):
        gate = _dot(h, w_ref[:, 2560 + 512 * p:3072 + 512 * p])
        g_ref[0, :, 512 * p:512 * (p + 1)] = (gate * jax.nn.sigmoid(gate)).astype(_BF16)


def _inproj_even_call(x, mod, w, cos, sin):
    bsz, seq, d = x.shape
    tm = PROJ_TM
    row = lambda n: pl.BlockSpec((1, tm, n), lambda b, i: (b, i, 0))
    outs = [(512, _BF16), (512, _BF16), (512, _BF16), (256, _BF16), (512, _BF16), (256, _BF16),
            (1024, _BF16)]
    return pl.pallas_call(
        _inproj_even_body,
        out_shape=[jax.ShapeDtypeStruct((bsz, seq, n), dt) for n, dt in outs],
        grid=(bsz, seq // tm),
        in_specs=[
            row(d),
            pl.BlockSpec((1, 3, d), lambda b, i: (b, 0, 0)),
            pl.BlockSpec((d, EVEN_COLS), lambda b, i: (0, 0)),
            pl.BlockSpec((tm, LANES), lambda b, i: (i, 0)),
            pl.BlockSpec((tm, LANES), lambda b, i: (i, 0)),
        ],
        out_specs=[row(n) for n, _ in outs],
        compiler_params=_cparams(("arbitrary", "arbitrary")),
        name="inproj_even",
    )(x, mod, w, cos, sin)


def _moba_body(q_ref, k_ref, v_ref, o_ref, km_ref):
    qi = pl.program_id(2)
    tq = MOBA_BLOCK
    nblk = km_ref.shape[0]

    @pl.when(qi == 0)
    def _():
        for r in range(nblk):
            blk_k = k_ref[0, r * tq:(r + 1) * tq, :].astype(_F32)
            km_ref[r:r + 1, :] = jnp.mean(blk_k, axis=0, keepdims=True)

    q = q_ref[0]
    lane = _lane_iota((1, LANES))
    km = km_ref[...].astype(_BF16)
    blk = _lane_iota((tq, nblk))
    blkf = blk.astype(_F32)
    row = _row_iota((tq, tq))
    col = _lane_iota((tq, tq))
    outs = []
    for hh in range(2):
        qh = jnp.where((lane // HEAD_DIM) == hh, q, jnp.zeros_like(q))
        valid = blk < qi
        gate = jnp.where(valid, _dot_t(qh, km), -jnp.inf)
        sel = jnp.zeros((tq, nblk), _F32)
        for _ in range(MOBA_TOPK):
            top = jnp.max(gate, axis=1, keepdims=True)
            first = jnp.min(jnp.where(gate == top, blkf, float(nblk)), axis=1, keepdims=True)
            pick = blkf == first
            sel = jnp.where(pick, 1.0, sel)
            gate = jnp.where(pick, -jnp.inf, gate)
        sel = jnp.where(valid, sel, 0.0)

        own = pl.multiple_of(qi * tq, tq)
        s = _dot_t(qh, k_ref[0, pl.ds(own, tq), :])
        s = jnp.where(col <= row, s, -jnp.inf)
        m = jnp.max(s, axis=1, keepdims=True)
        p = jnp.exp(s - m)
        l = jnp.sum(p, axis=1, keepdims=True)
        acc = _dot(p.astype(_BF16), v_ref[0, pl.ds(own, tq), :])

        def body(j, carry):
            m, l, acc = carry
            off = pl.multiple_of(j * tq, tq)
            s = _dot_t(qh, k_ref[0, pl.ds(off, tq), :])
            chosen = jnp.max(jnp.where(blk == j, sel, 0.0), axis=1, keepdims=True) > 0.0
            s = jnp.where(chosen, s, NEG_BIG)
            m_new = jnp.maximum(m, jnp.max(s, axis=1, keepdims=True))
            alpha = jnp.exp(m - m_new)
            p = jnp.exp(s - m_new)
            l = alpha * l + jnp.sum(p, axis=1, keepdims=True)
            acc = alpha * acc + _dot(p.astype(_BF16), v_ref[0, pl.ds(off, tq), :])
            return m_new, l, acc

        m, l, acc = lax.fori_loop(0, qi, body, (m, l, acc))
        outs.append(acc / l)
    o_ref[0] = jnp.where(lane < HEAD_DIM, outs[0], outs[1]).astype(o_ref.dtype)


def _moba_call(q, k, v):
    bsz, seq, width = q.shape
    tq = MOBA_BLOCK
    nblk = seq // tq
    return pl.pallas_call(
        _moba_body,
        out_shape=jax.ShapeDtypeStruct((bsz, seq, width), _BF16),
        grid=(bsz, width // LANES, nblk),
        in_specs=[
            pl.BlockSpec((1, tq, LANES), lambda b, h, i: (b, i, h)),
            pl.BlockSpec((1, seq, LANES), lambda b, h, i: (b, 0, h)),
            pl.BlockSpec((1, seq, LANES), lambda b, h, i: (b, 0, h)),
        ],
        out_specs=pl.BlockSpec((1, tq, LANES), lambda b, h, i: (b, i, h)),
        scratch_shapes=[pltpu.VMEM((nblk, LANES), _F32)],
        compiler_params=_cparams(("arbitrary", "arbitrary", "arbitrary")),
        name="moba",
    )(q, k, v)


def _swa_body(sink_ref, q_ref, kp_ref, kc_ref, vp_ref, vc_ref, o_ref):
    i = pl.program_id(1)
    w = SWA_WINDOW
    lane = _lane_iota((1, LANES))
    row = _row_iota((w, 2 * w))
    col = _lane_iota((w, 2 * w))
    mask = (col > row) & (col <= row + w) & ((col >= w) | (i > 0))
    pairs_per_kv = (B_HEADS // B_KV_HEADS) // 2
    for p in range(B_HEADS // 2):
        g = p // pairs_per_kv
        qp = q_ref[0, :, p * LANES:(p + 1) * LANES]
        k = jnp.concatenate([kp_ref[0, :, g * LANES:(g + 1) * LANES],
                             kc_ref[0, :, g * LANES:(g + 1) * LANES]], axis=0)
        v = jnp.concatenate([vp_ref[0, :, g * LANES:(g + 1) * LANES],
                             vc_ref[0, :, g * LANES:(g + 1) * LANES]], axis=0)
        outs = []
        for hh in range(2):
            qh = jnp.where((lane // HEAD_DIM) == hh, qp, jnp.zeros_like(qp))
            sink = sink_ref[2 * p + hh]
            s = jnp.where(mask, _dot_t(qh, k), -jnp.inf)
            m = jnp.maximum(jnp.max(s, axis=1, keepdims=True), sink)
            e = jnp.exp(s - m)
            l = jnp.sum(e, axis=1, keepdims=True) + jnp.exp(sink - m)
            outs.append(_dot(e.astype(_BF16), v) / l)
        o_ref[0, :, p * LANES:(p + 1) * LANES] = jnp.where(lane < HEAD_DIM, outs[0], outs[1]).astype(o_ref.dtype)


def _swa_call(q, kdup, vdup, sinks):
    bsz, seq, width = q.shape
    w = SWA_WINDOW
    kvw = kdup.shape[2]
    prev = pl.BlockSpec((1, w, kvw), lambda b, i: (b, jnp.maximum(i - 1, 0), 0))
    cur = pl.BlockSpec((1, w, kvw), lambda b, i: (b, i, 0))
    return pl.pallas_call(
        _swa_body,
        out_shape=jax.ShapeDtypeStruct((bsz, seq, width), _BF16),
        grid=(bsz, seq // w),
        in_specs=[
            pl.BlockSpec(memory_space=pltpu.SMEM),
            pl.BlockSpec((1, w, width), lambda b, i: (b, i, 0)),
            prev, cur, prev, cur,
        ],
        out_specs=pl.BlockSpec((1, w, width), lambda b, i: (b, i, 0)),
        compiler_params=_cparams(("arbitrary", "arbitrary")),
        name="swa_sink",
    )(sinks, q, kdup, kdup, vdup, vdup)


def _outproj_body(o1_ref, o2_ref, g_ref, x_ref, mod_ref, w_ref, lng_ref, lnb_ref, y_ref):
    half = o1_ref.shape[2]
    a1 = o1_ref[0] * g_ref[0, :, :half]
    a2 = o2_ref[0] * g_ref[0, :, half:]
    y = _dot(a1, w_ref[:half, :]) + _dot(a2, w_ref[half:, :])
    gate = mod_ref[0, 2:3, :]
    z = DEEPNORM_ALPHA * x_ref[0] + (1.0 + gate) * y
    mu = jnp.mean(z, axis=1, keepdims=True)
    zc = z - mu
    var = jnp.mean(zc * zc, axis=1, keepdims=True)
    y_ref[0] = zc * lax.rsqrt(var + LN_EPS) * lng_ref[...] + lnb_ref[...]


def _outproj_call(o1, o2, g, x, mod, w_out, ln_g, ln_b):
    bsz, seq, d = x.shape
    tm = PROJ_TM
    half = o1.shape[2]
    row = lambda n: pl.BlockSpec((1, tm, n), lambda b, i: (b, i, 0))
    return pl.pallas_call(
        _outproj_body,
        out_shape=jax.ShapeDtypeStruct((bsz, seq, d), _F32),
        grid=(bsz, seq // tm),
        in_specs=[
            row(half), row(half), row(2 * half), row(d),
            pl.BlockSpec((1, 3, d), lambda b, i: (b, 0, 0)),
            pl.BlockSpec((2 * half, d), lambda b, i: (0, 0)),
            pl.BlockSpec((1, d), lambda b, i: (0, 0)),
            pl.BlockSpec((1, d), lambda b, i: (0, 0)),
        ],
        out_specs=row(d),
        compiler_params=_cparams(("arbitrary", "arbitrary")),
        name="outproj_deepnorm",
    )(o1, o2, g, x, mod, w_out.astype(_BF16), ln_g.reshape(1, d), ln_b.reshape(1, d))


ODD_COLS = 4096


def _prep_w_odd(w):
    offs = [0]
    for n in ODD_SIZES:
        offs.append(offs[-1] + n)
    cqn, cqr, ckv, ckr, iq, ik, iw, dq, dk, dv, gate = [w[:, offs[i]:offs[i + 1]] for i in range(len(ODD_SIZES))]
    iw_blk = jnp.concatenate([iw, jnp.zeros((w.shape[0], LANES - IDX_HEADS), w.dtype)], axis=1)
    return jnp.concatenate([cqn, cqr, ckv, jnp.tile(ckr, (1, 4)), iq, jnp.tile(ik, (1, 4)), iw_blk,
                            dq, dk, dv, gate], axis=1).astype(_BF16)


def _inproj_odd_body(x_ref, mod_ref, w_ref, cosr_ref, sinr_ref, cosi_ref, sini_ref, kvg_ref,
                     cqn_ref, cqr_ref, kvc_ref, iq_ref, ik_ref, iw_ref, dq_ref, dk_ref, dv_ref, g_ref):
    h = _modulated(x_ref, mod_ref)
    cosr, sinr = cosr_ref[...], sinr_ref[...]
    cosi, sini = cosi_ref[...], sini_ref[...]

    cqn_ref[0] = _dot(h, w_ref[:, 0:512]).astype(_BF16)
    acc = _dot(h, w_ref[:, 512:1024])
    for p in range(2):
        cqr_ref[0, :, p * LANES:(p + 1) * LANES] = _rope_piece(
            acc[:, p * LANES:(p + 1) * LANES], cosr, sinr, C_ROPE_DIM).astype(_BF16)
    ckv = acc[:, 256:384]
    ckv = ckv * lax.rsqrt(jnp.mean(ckv * ckv, axis=1, keepdims=True) + LN_EPS) * kvg_ref[...]
    kvc_ref[0, :, 0:LANES] = ckv.astype(_BF16)
    kvc_ref[0, :, LANES:2 * LANES] = _rope_piece(acc[:, 384:512], cosr, sinr, C_ROPE_DIM).astype(_BF16)
    acc = _dot(h, w_ref[:, 1024:1536])
    for p in range(2):
        iq_ref[0, :, p * LANES:(p + 1) * LANES] = _rope_piece(
            acc[:, p * LANES:(p + 1) * LANES], cosi, sini, IDX_DIM).astype(_BF16)
    ik_ref[0] = _rope_piece(acc[:, 256:384], cosi, sini, IDX_DIM).astype(_BF16)
    iw_ref[0] = acc[:, 384:512]
    dq_ref[0] = (_dot(h, w_ref[:, 1536:2048]) * (HEAD_DIM ** -0.5)).astype(_BF16)
    dk_ref[0] = _dot(h, w_ref[:, 2048:2560]).astype(_BF16)
    dv_ref[0] = _dot(h, w_ref[:, 2560:3072]).astype(_BF16)
    for p in range(2):
        gate = _dot(h, w_ref[:, 3072 + 512 * p:3584 + 512 * p])
        g_ref[0, :, 512 * p:512 * (p + 1)] = (gate * jax.nn.sigmoid(gate)).astype(_BF16)


def _inproj_odd_call(x, mod, w, rope_r, rope_i, kv_g):
    bsz, seq, d = x.shape
    tm = PROJ_TM
    row = lambda n: pl.BlockSpec((1, tm, n), lambda b, i: (b, i, 0))
    tab = pl.BlockSpec((tm, LANES), lambda b, i: (i, 0))
    outs = [(512, _BF16), (256, _BF16), (256, _BF16), (256, _BF16), (128, _BF16), (128, _F32),
            (512, _BF16), (512, _BF16), (512, _BF16), (1024, _BF16)]
    return pl.pallas_call(
        _inproj_odd_body,
        out_shape=[jax.ShapeDtypeStruct((bsz, seq, n), dt) for n, dt in outs],
        grid=(bsz, seq // tm),
        in_specs=[
            row(d),
            pl.BlockSpec((1, 3, d), lambda b, i: (b, 0, 0)),
            pl.BlockSpec((d, ODD_COLS), lambda b, i: (0, 0)),
            tab, tab, tab, tab,
            pl.BlockSpec((1, LANES), lambda b, i: (0, 0)),
        ],
        out_specs=[row(n) for n, _ in outs],
        compiler_params=_cparams(("arbitrary", "arbitrary")),
        name="inproj_odd",
    )(x, mod, w, rope_r[0], rope_r[1], rope_i[0], rope_i[1], kv_g.reshape(1, LANES))


def _dsa_body(cqn_ref, cqr_ref, iq_ref, iw_ref, kvc_ref, ik_ref, wuk_ref, wuv_ref, o_ref,
              qcat_ref, iqm_ref, key_ref, bias_ref):
    i = pl.program_id(1)
    tq, ck = DSA_TQ, DSA_CK
    seq = kvc_ref.shape[1]
    n_sel = float(min(DSA_TOPK, seq // 4))
    nch = i // (ck // tq) + 1
    lane = _lane_iota((1, LANES))
    att_scale = (C_NOPE_DIM + C_ROPE_DIM) ** -0.5
    idx_scale = (IDX_DIM * IDX_HEADS) ** -0.5

    for h in range(C_HEADS):
        p, hh = h // 2, h % 2
        qn = cqn_ref[0, :, p * LANES:(p + 1) * LANES]
        qn = jnp.where((lane // C_NOPE_DIM) == hh, qn, jnp.zeros_like(qn))
        qlat = _dot(qn, wuk_ref[p * LANES:(p + 1) * LANES, :]) * att_scale
        blk4 = h // 4
        qr = cqr_ref[0, :, blk4 * LANES:(blk4 + 1) * LANES].astype(_F32)
        qr = jnp.where((lane // C_ROPE_DIM) == (h % 4), qr, 0.0) * att_scale
        qcat_ref[h, :, 0:LANES] = qlat.astype(_BF16)
        qcat_ref[h, :, LANES:2 * LANES] = qr.astype(_BF16)
        iqh = iq_ref[0, :, blk4 * LANES:(blk4 + 1) * LANES]
        iqm_ref[h] = jnp.where((lane // IDX_DIM) == (h % 4), iqh, jnp.zeros_like(iqh))

    t_pos = i * tq + _row_iota((tq, ck))

    def score_body(c, _):
        off = pl.multiple_of(c * ck, ck)
        ik = ik_ref[0, pl.ds(off, ck), :]
        score = jnp.zeros((tq, ck), _F32)
        for h in range(IDX_HEADS):
            rel = jnp.maximum(_dot_t(iqm_ref[h], ik), 0.0)
            score = score + iw_ref[0, :, h:h + 1] * rel
        score = score * idx_scale
        score = jnp.where(score == 0.0, 0.0, score)
        bits = pltpu.bitcast(score, jnp.int32)
        key = bits ^ ((bits >> 31) & 0x7FFFFFFF)
        s_pos = off + _lane_iota((tq, ck))
        key_ref[c] = jnp.where(s_pos <= t_pos, key, INT_MIN)
        return 0

    lax.fori_loop(0, nch, score_body, 0)

    def count(pred):
        def body(c, acc):
            off = pl.multiple_of(c * ck, ck)
            hit = jnp.where(pred(key_ref[c], off), 1.0, 0.0)
            for q in range(ck // LANES):
                acc = acc + hit[:, q * LANES:(q + 1) * LANES]
            return acc
        acc = lax.fori_loop(0, nch, body, jnp.zeros((tq, LANES), _F32))
        return jnp.sum(acc, axis=1, keepdims=True)

    def bit_body(b, carry):
        t, cnt_t = carry
        cand = t | jnp.left_shift(jnp.int32(1), 31 - b)
        cand_c = jnp.broadcast_to(cand ^ INT_MIN, (tq, ck))
        cnt = count(lambda kc, off: kc >= cand_c)
        ok = cnt >= n_sel
        return jnp.where(ok, cand, t), jnp.where(ok, cnt, cnt_t)

    t0 = jnp.zeros((tq, 1), jnp.int32)
    t, cnt_t = lax.fori_loop(0, 32, bit_body, (t0, jnp.full((tq, 1), float(seq), _F32)))
    thr = t ^ INT_MIN
    thr_c = jnp.broadcast_to(thr, (tq, ck))

    excess = jnp.max(jnp.where((cnt_t > n_sel) & (t != 0), 1.0, 0.0)) > 0.0

    def tie_limit():
        cnt_gt = count(lambda kc, off: kc > thr_c)
        room = n_sel - cnt_gt

        def lim_body(b, lim):
            cand = lim | jnp.left_shift(jnp.int32(1), 12 - b)
            cand_c = jnp.broadcast_to(cand, (tq, ck))
            cnt = count(lambda kc, off: (kc == thr_c) & ((off + _lane_iota((tq, ck))) < cand_c))
            return jnp.where(cnt <= room, cand, lim)

        return lax.fori_loop(0, 13, lim_body, jnp.zeros((tq, 1), jnp.int32))

    limit = lax.cond(excess, tie_limit, lambda: jnp.full((tq, 1), 2 * seq, jnp.int32))
    limit_c = jnp.broadcast_to(limit, (tq, ck))

    def bias_body(c, _):
        off = pl.multiple_of(c * ck, ck)
        kc = key_ref[c]
        s_pos = off + _lane_iota((tq, ck))
        chosen = (kc > thr_c) | ((kc == thr_c) & (s_pos < limit_c))
        chosen = chosen & (s_pos <= t_pos)
        bias_ref[c] = jnp.where(chosen, 0.0, NEG_BIG)
        return 0

    lax.fori_loop(0, nch, bias_body, 0)

    for p in range(C_HEADS // 2):
        out_pair = jnp.zeros((tq, LANES), _F32)
        for hh in range(2):
            h = 2 * p + hh

            def att_body(c, carry, h=h):
                m, l, acc = carry
                off = pl.multiple_of(c * ck, ck)
                kv = kvc_ref[0, pl.ds(off, ck), :]
                s = _dot_t(qcat_ref[h], kv) + bias_ref[c]
                m_new = jnp.maximum(m, jnp.max(s, axis=1, keepdims=True))
                alpha = jnp.exp(m - m_new)
                e = jnp.exp(s - m_new)
                l = alpha * l + jnp.sum(e, axis=1, keepdims=True)
                acc = alpha * acc + _dot(e.astype(_BF16), kv[:, 0:C_KV_LATENT])
                return m_new, l, acc

            m0 = jnp.full((tq, 1), NEG_BIG, _F32)
            l0 = jnp.zeros((tq, 1), _F32)
            a0 = jnp.zeros((tq, C_KV_LATENT), _F32)
            m, l, acc = lax.fori_loop(0, nch, att_body, (m0, l0, a0))
            out_pair = out_pair + _dot((acc / l).astype(_BF16), wuv_ref[h])
        o_ref[0, :, p * LANES:(p + 1) * LANES] = out_pair.astype(o_ref.dtype)


def _dsa_call(cqn, cqr, iq, iw, kvc, ik, wuk_t, wuv_x):
    bsz, seq, _ = cqn.shape
    tq = DSA_TQ
    row = lambda n: pl.BlockSpec((1, tq, n), lambda b, i: (b, i, 0))
    full = lambda n: pl.BlockSpec((1, seq, n), lambda b, i: (b, 0, 0))
    return pl.pallas_call(
        _dsa_body,
        out_shape=jax.ShapeDtypeStruct((bsz, seq, C_HEADS * C_V_DIM), _BF16),
        grid=(bsz, seq // tq),
        in_specs=[
            row(512), row(256), row(256), row(128), full(256), full(128),
            pl.BlockSpec(wuk_t.shape, lambda b, i: (0, 0)),
            pl.BlockSpec(wuv_x.shape, lambda b, i: (0, 0, 0)),
        ],
        out_specs=row(C_HEADS * C_V_DIM),
        scratch_shapes=[
            pltpu.VMEM((C_HEADS, tq, 2 * LANES), _BF16),
            pltpu.VMEM((IDX_HEADS, tq, LANES), _BF16),
            pltpu.VMEM((seq // DSA_CK, tq, DSA_CK), jnp.int32),
            pltpu.VMEM((seq // DSA_CK, tq, DSA_CK), _F32),
        ],
        compiler_params=_cparams(("arbitrary", "arbitrary")),
        name="dsa",
    )(cqn, cqr, iq, iw, kvc, ik, wuk_t, wuv_x)


def _sb_body(q_ref, k_ref, v_ref, o_ref):
    i = pl.program_id(2)
    t = SB_TQ
    q = q_ref[0]
    lane = _lane_iota((1, LANES))
    row = _row_iota((t, t))
    col = _lane_iota((t, t))
    later = jnp.where(row > col, 1.0, 0.0).astype(_BF16)
    outs = []
    for hh in range(2):
        qh = jnp.where((lane // HEAD_DIM) == hh, q, jnp.zeros_like(q))

        def cond(carry):
            j, _, _, top = carry
            return (j >= 0) & (top > SB_UNDERFLOW)

        def body(carry):
            j, rest, acc, _ = carry
            off = pl.multiple_of(j * t, t)
            z = _dot_t(qh, k_ref[0, pl.ds(off, t), :])
            past = (off + col) < (i * t + row)
            log_beta = jnp.minimum(z, 0.0) - jnp.log1p(jnp.exp(-jnp.abs(z)))
            log_keep = jnp.where(past, log_beta - z, 0.0)
            hi = log_keep.astype(_BF16)
            lo = (log_keep - hi.astype(_F32)).astype(_BF16)
            log_after = _dot(hi, later) + _dot(lo, later) + rest
            a = jnp.where(past, jnp.exp(log_beta + log_after), 0.0)
            acc = acc + _dot(a.astype(_BF16), v_ref[0, pl.ds(off, t), :])
            rest = rest + jnp.sum(log_keep, axis=1, keepdims=True)
            return j - 1, rest, acc, jnp.max(rest)

        init = (i, jnp.zeros((t, 1), _F32), jnp.zeros((t, LANES), _F32), jnp.float32(0.0))
        _, _, acc, _ = lax.while_loop(cond, body, init)
        outs.append(acc)
    o_ref[0] = jnp.where(lane < HEAD_DIM, outs[0], outs[1]).astype(o_ref.dtype)


def _sb_call(q, k, v):
    bsz, seq, width = q.shape
    t = SB_TQ
    return pl.pallas_call(
        _sb_body,
        out_shape=jax.ShapeDtypeStruct((bsz, seq, width), _BF16),
        grid=(bsz, width // LANES, seq // t),
        in_specs=[
            pl.BlockSpec((1, t, LANES), lambda b, h, i: (b, i, h)),
            pl.BlockSpec((1, seq, LANES), lambda b, h, i: (b, 0, h)),
            pl.BlockSpec((1, seq, LANES), lambda b, h, i: (b, 0, h)),
        ],
        out_specs=pl.BlockSpec((1, t, LANES), lambda b, h, i: (b, i, h)),
        compiler_params=_cparams(("arbitrary", "arbitrary", "arbitrary")),
        name="stick_breaking",
    )(q, k, v)


def _even_layer(x, mod, w_in, sinks, w_out, ln_g, ln_b, rope_h):
    aq, ak, bq, bk, av, bv, g = _inproj_even_call(x, mod, _prep_w_even(w_in), *rope_h)
    oa = _moba_call(aq, ak, av)
    ob = _swa_call(bq, bk, bv, sinks)
    return _outproj_call(oa, ob, g, x, mod, w_out, ln_g, ln_b)


def _odd_layer(x, mod, w_in, kv_g, w_uk, w_uv, w_out, ln_g, ln_b, rope_r, rope_i):
    cqn, cqr, kvc, iq, ik, iw, dq, dk, dv, g = _inproj_odd_call(x, mod, _prep_w_odd(w_in), rope_r, rope_i, kv_g)
    wuk_t = w_uk.transpose(0, 2, 1).reshape(C_HEADS * C_NOPE_DIM, C_KV_LATENT).astype(_BF16)
    zeros = jnp.zeros_like(w_uv)
    even = jnp.concatenate([w_uv, zeros], axis=2)
    odd = jnp.concatenate([zeros, w_uv], axis=2)
    is_odd = (jnp.arange(C_HEADS) % 2 == 1)[:, None, None]
    wuv_x = jnp.where(is_odd, odd, even).astype(_BF16)
    oc = _dsa_call(cqn, cqr, iq, iw, kvc, ik, wuk_t, wuv_x)
    od = _sb_call(dq, dk, dv)
    return _outproj_call(oc, od, g, x, mod, w_out, ln_g, ln_b)


def kernel(x, c, w_ada, b_ada, w_in_even, sink_logits, w_in_odd, kv_norm_g, w_uk, w_uv, w_out, ln_g, ln_b):
    bsz, seq, d = x.shape
    rope_h = _rope_tables(seq, HEAD_DIM, LANES)
    rope_r = _rope_tables(seq, C_ROPE_DIM, LANES)
    rope_i = _rope_tables(seq, IDX_DIM, LANES)
    mods = _ada_call(c, w_ada, b_ada).reshape(DEPTH, bsz, 3, d)
    for layer in range(DEPTH):
        mod = mods[layer]
        j = layer // 2
        if layer % 2 == 0:
            x = _even_layer(x, mod, w_in_even[j], sink_logits[j], w_out[layer], ln_g[layer], ln_b[layer], rope_h)
        else:
            x = _odd_layer(x, mod, w_in_odd[j], kv_norm_g[j], w_uk[j], w_uv[j], w_out[layer],
                           ln_g[layer], ln_b[layer], rope_r, rope_i)
    return x
```

```python
import functools

import jax
import jax.numpy as jnp
from jax import lax
from jax.experimental import pallas as pl
from jax.experimental.pallas import tpu as pltpu

D_MODEL = 1024
DEPTH = 2
HEAD_DIM = 64
ROPE_THETA = 10000.0
LN_EPS = 1e-5
A_HEADS = 8
MOBA_BLOCK = 256
MOBA_TOPK = 3
B_HEADS = 8
B_KV_HEADS = 2
SWA_WINDOW = 128
C_HEADS = 8
C_NOPE_DIM = 64
C_ROPE_DIM = 32
C_V_DIM = 64
C_KV_LATENT = 128
IDX_HEADS = 8
IDX_DIM = 32
DSA_TOPK = 256
D_HEADS = 8
MIX_WIDTH = 1024
EVEN_SIZES = (512, 512, 512, 512, 128, 128, 1024)
ODD_SIZES = (512, 256, 128, 32, 256, 32, 8, 512, 512, 512, 1024)
DEEPNORM_ALPHA = (2 * DEPTH) ** 0.25

LANES = 128
NEG_BIG = -1e30
INT_MIN = -2 ** 31
SB_UNDERFLOW = -104.0

PROJ_TM = 512
DSA_TQ = 128
DSA_CK = 512
SB_TQ = 128
VMEM_LIMIT = 48 * 1024 * 1024

_BF16 = jnp.bfloat16
_F32 = jnp.float32


def _cparams(sem):
    return pltpu.CompilerParams(dimension_semantics=sem, vmem_limit_bytes=VMEM_LIMIT)


def _dot_t(a, b):
    return lax.dot_general(a, b, (((1,), (1,)), ((), ())), preferred_element_type=_F32)


def _dot(a, b):
    return jnp.dot(a, b, preferred_element_type=_F32)


def _lane_iota(shape):
    return lax.broadcasted_iota(jnp.int32, shape, len(shape) - 1)


def _row_iota(shape):
    return lax.broadcasted_iota(jnp.int32, shape, len(shape) - 2)


def _ada_body(c_ref, w_ref, b_ref, o_ref):
    c = c_ref[...]
    cond = c * jax.nn.sigmoid(c)
    o_ref[0] = _dot(cond, w_ref[0]) + b_ref[0]


def _ada_call(c, w_ada, b_ada):
    depth, d, n3 = w_ada.shape
    bsz = c.shape[0]
    nb = n3 // d
    return pl.pallas_call(
        _ada_body,
        out_shape=jax.ShapeDtypeStruct((depth, bsz, n3), _F32),
        grid=(depth, nb),
        in_specs=[
            pl.BlockSpec((bsz, d), lambda l, j: (0, 0)),
            pl.BlockSpec((1, d, d), lambda l, j: (l, 0, j)),
            pl.BlockSpec((1, 1, d), lambda l, j: (l, 0, j)),
        ],
        out_specs=pl.BlockSpec((1, bsz, d), lambda l, j: (l, 0, j)),
        compiler_params=_cparams(("arbitrary", "arbitrary")),
        name="ada_mod",
    )(c, w_ada, b_ada.reshape(depth, 1, n3))


def _rope_tables(seq, dim, period_lanes):
    inv = 1.0 / (ROPE_THETA ** (jnp.arange(0, dim, 2, dtype=_F32) / dim))
    ang = jnp.arange(seq, dtype=_F32)[:, None] * inv[None, :]
    cos, sin = jnp.cos(ang), jnp.sin(ang)
    cos_h = jnp.concatenate([cos, cos], axis=1)
    sin_h = jnp.concatenate([-sin, sin], axis=1)
    reps = period_lanes // dim
    return jnp.tile(cos_h, (1, reps)), jnp.tile(sin_h, (1, reps))


def _rope_piece(x, cos, sin_signed, dim):
    half = dim // 2
    first = (_lane_iota(x.shape) % dim) < half
    partner = jnp.where(first, pltpu.roll(x, LANES - half, 1), pltpu.roll(x, half, 1))
    return x * cos + partner * sin_signed


def _modulated(x_ref, mod_ref):
    x = x_ref[0]
    shift = mod_ref[0, 0:1, :]
    scale = mod_ref[0, 1:2, :]
    return (x * (1.0 + scale) + shift).astype(_BF16)


EVEN_COLS = 3584


def _prep_w_even(w):
    offs = [0]
    for n in EVEN_SIZES:
        offs.append(offs[-1] + n)
    aq, ak, av, bq, bk, bv, gate = [w[:, offs[i]:offs[i + 1]] for i in range(len(EVEN_SIZES))]

    def dup(t):
        parts = []
        for g in range(B_KV_HEADS):
            blk = t[:, g * HEAD_DIM:(g + 1) * HEAD_DIM]
            parts += [blk, blk]
        return jnp.concatenate(parts, axis=1)

    return jnp.concatenate([aq, ak, bq, dup(bk), av, dup(bv), gate], axis=1).astype(_BF16)


def _inproj_even_body(x_ref, mod_ref, w_ref, cos_ref, sin_ref,
                      aq_ref, ak_ref, bq_ref, bk_ref, av_ref, bv_ref, g_ref):
    h = _modulated(x_ref, mod_ref)
    cos = cos_ref[...]
    sin = sin_ref[...]
    q_scale = HEAD_DIM ** -0.5

    def roped(col0, ncols, out_ref, scale):
        acc = _dot(h, w_ref[:, col0:col0 + ncols])
        for p in range(ncols // LANES):
            piece = _rope_piece(acc[:, p * LANES:(p + 1) * LANES], cos, sin, HEAD_DIM)
            if scale != 1.0:
                piece = piece * scale
            out_ref[0, :, p * LANES:(p + 1) * LANES] = piece.astype(_BF16)

    roped(0, 512, aq_ref, q_scale)
    roped(512, 512, ak_ref, 1.0)
    roped(1024, 512, bq_ref, q_scale)
    roped(1536, 256, bk_ref, 1.0)
    av_ref[0] = _dot(h, w_ref[:, 1792:2304]).astype(_BF16)
    bv_ref[0] = _dot(h, w_ref[:, 2304:2560]).astype(_BF16)
    for p in range(2):
        gate = _dot(h, w_ref[:, 2560 + 512 * p:3072 + 512 * p])
        g_ref[0, :, 512 * p:512 * (p + 1)] = (gate * jax.nn.sigmoid(gate)).astype(_BF16)


def _inproj_even_call(x, mod, w, cos, sin):
    bsz, seq, d = x.shape
    tm = PROJ_TM
    row = lambda n: pl.BlockSpec((1, tm, n), lambda b, i: (b, i, 0))
    outs = [(512, _BF16), (512, _BF16), (512, _BF16), (256, _BF16), (512, _BF16), (256, _BF16),
            (1024, _BF16)]
    return pl.pallas_call(
        _inproj_even_body,
        out_shape=[jax.ShapeDtypeStruct((bsz, seq, n), dt) for n, dt in outs],
        grid=(bsz, seq // tm),
        in_specs=[
            row(d),
            pl.BlockSpec((1, 3, d), lambda b, i: (b, 0, 0)),
            pl.BlockSpec((d, EVEN_COLS), lambda b, i: (0, 0)),
            pl.BlockSpec((tm, LANES), lambda b, i: (i, 0)),
            pl.BlockSpec((tm, LANES), lambda b, i: (i, 0)),
        ],
        out_specs=[row(n) for n, _ in outs],
        compiler_params=_cparams(("arbitrary", "arbitrary")),
        name="inproj_even",
    )(x, mod, w, cos, sin)


def _moba_body(q_ref, k_ref, v_ref, o_ref, km_ref):
    qi = pl.program_id(2)
    tq = MOBA_BLOCK
    nblk = km_ref.shape[0]

    @pl.when(qi == 0)
    def _():
        for r in range(nblk):
            blk_k = k_ref[0, r * tq:(r + 1) * tq, :].astype(_F32)
            km_ref[r:r + 1, :] = jnp.mean(blk_k, axis=0, keepdims=True)

    q = q_ref[0]
    lane = _lane_iota((1, LANES))
    km = km_ref[...].astype(_BF16)
    blk = _lane_iota((tq, nblk))
    blkf = blk.astype(_F32)
    row = _row_iota((tq, tq))
    col = _lane_iota((tq, tq))
    outs = []
    for hh in range(2):
        qh = jnp.where((lane // HEAD_DIM) == hh, q, jnp.zeros_like(q))
        valid = blk < qi
        gate = jnp.where(valid, _dot_t(qh, km), -jnp.inf)
        sel = jnp.zeros((tq, nblk), _F32)
        for _ in range(MOBA_TOPK):
            top = jnp.max(gate, axis=1, keepdims=True)
            first = jnp.min(jnp.where(gate == top, blkf, float(nblk)), axis=1, keepdims=True)
            pick = blkf == first
            sel = jnp.where(pick, 1.0, sel)
            gate = jnp.where(pick, -jnp.inf, gate)
        sel = jnp.where(valid, sel, 0.0)

        own = pl.multiple_of(qi * tq, tq)
        s = _dot_t(qh, k_ref[0, pl.ds(own, tq), :])
        s = jnp.where(col <= row, s, -jnp.inf)
        m = jnp.max(s, axis=1, keepdims=True)
        p = jnp.exp(s - m)
        l = jnp.sum(p, axis=1, keepdims=True)
        acc = _dot(p.astype(_BF16), v_ref[0, pl.ds(own, tq), :])

        def body(j, carry):
            m, l, acc = carry
            off = pl.multiple_of(j * tq, tq)
            s = _dot_t(qh, k_ref[0, pl.ds(off, tq), :])
            chosen = jnp.max(jnp.where(blk == j, sel, 0.0), axis=1, keepdims=True) > 0.0
            s = jnp.where(chosen, s, NEG_BIG)
            m_new = jnp.maximum(m, jnp.max(s, axis=1, keepdims=True))
            alpha = jnp.exp(m - m_new)
            p = jnp.exp(s - m_new)
            l = alpha * l + jnp.sum(p, axis=1, keepdims=True)
            acc = alpha * acc + _dot(p.astype(_BF16), v_ref[0, pl.ds(off, tq), :])
            return m_new, l, acc

        m, l, acc = lax.fori_loop(0, qi, body, (m, l, acc))
        outs.append(acc / l)
    o_ref[0] = jnp.where(lane < HEAD_DIM, outs[0], outs[1]).astype(o_ref.dtype)


def _moba_call(q, k, v):
    bsz, seq, width = q.shape
    tq = MOBA_BLOCK
    nblk = seq // tq
    return pl.pallas_call(
        _moba_body,
        out_shape=jax.ShapeDtypeStruct((bsz, seq, width), _BF16),
        grid=(bsz, width // LANES, nblk),
        in_specs=[
            pl.BlockSpec((1, tq, LANES), lambda b, h, i: (b, i, h)),
            pl.BlockSpec((1, seq, LANES), lambda b, h, i: (b, 0, h)),
            pl.BlockSpec((1, seq, LANES), lambda b, h, i: (b, 0, h)),
        ],
        out_specs=pl.BlockSpec((1, tq, LANES), lambda b, h, i: (b, i, h)),
        scratch_shapes=[pltpu.VMEM((nblk, LANES), _F32)],
        compiler_params=_cparams(("arbitrary", "arbitrary", "arbitrary")),
        name="moba",
    )(q, k, v)


def _swa_body(sink_ref, q_ref, kp_ref, kc_ref, vp_ref, vc_ref, o_ref):
    i = pl.program_id(1)
    w = SWA_WINDOW
    lane = _lane_iota((1, LANES))
    row = _row_iota((w, 2 * w))
    col = _lane_iota((w, 2 * w))
    mask = (col > row) & (col <= row + w) & ((col >= w) | (i > 0))
    pairs_per_kv = (B_HEADS // B_KV_HEADS) // 2
    for p in range(B_HEADS // 2):
        g = p // pairs_per_kv
        qp = q_ref[0, :, p * LANES:(p + 1) * LANES]
        k = jnp.concatenate([kp_ref[0, :, g * LANES:(g + 1) * LANES],
                             kc_ref[0, :, g * LANES:(g + 1) * LANES]], axis=0)
        v = jnp.concatenate([vp_ref[0, :, g * LANES:(g + 1) * LANES],
                             vc_ref[0, :, g * LANES:(g + 1) * LANES]], axis=0)
        outs = []
        for hh in range(2):
            qh = jnp.where((lane // HEAD_DIM) == hh, qp, jnp.zeros_like(qp))
            sink = sink_ref[2 * p + hh]
            s = jnp.where(mask, _dot_t(qh, k), -jnp.inf)
            m = jnp.maximum(jnp.max(s, axis=1, keepdims=True), sink)
            e = jnp.exp(s - m)
            l = jnp.sum(e, axis=1, keepdims=True) + jnp.exp(sink - m)
            outs.append(_dot(e.astype(_BF16), v) / l)
        o_ref[0, :, p * LANES:(p + 1) * LANES] = jnp.where(lane < HEAD_DIM, outs[0], outs[1]).astype(o_ref.dtype)


def _swa_call(q, kdup, vdup, sinks):
    bsz, seq, width = q.shape
    w = SWA_WINDOW
    kvw = kdup.shape[2]
    prev = pl.BlockSpec((1, w, kvw), lambda b, i: (b, jnp.maximum(i - 1, 0), 0))
    cur = pl.BlockSpec((1, w, kvw), lambda b, i: (b, i, 0))
    return pl.pallas_call(
        _swa_body,
        out_shape=jax.ShapeDtypeStruct((bsz, seq, width), _BF16),
        grid=(bsz, seq // w),
        in_specs=[
            pl.BlockSpec(memory_space=pltpu.SMEM),
            pl.BlockSpec((1, w, width), lambda b, i: (b, i, 0)),
            prev, cur, prev, cur,
        ],
        out_specs=pl.BlockSpec((1, w, width), lambda b, i: (b, i, 0)),
        compiler_params=_cparams(("arbitrary", "arbitrary")),
        name="swa_sink",
    )(sinks, q, kdup, kdup, vdup, vdup)


def _outproj_body(o1_ref, o2_ref, g_ref, x_ref, mod_ref, w_ref, lng_ref, lnb_ref, y_ref):
    half = o1_ref.shape[2]
    a1 = o1_ref[0] * g_ref[0, :, :half]
    a2 = o2_ref[0] * g_ref[0, :, half:]
    y = _dot(a1, w_ref[:half, :]) + _dot(a2, w_ref[half:, :])
    gate = mod_ref[0, 2:3, :]
    z = DEEPNORM_ALPHA * x_ref[0] + (1.0 + gate) * y
    mu = jnp.mean(z, axis=1, keepdims=True)
    zc = z - mu
    var = jnp.mean(zc * zc, axis=1, keepdims=True)
    y_ref[0] = zc * lax.rsqrt(var + LN_EPS) * lng_ref[...] + lnb_ref[...]


def _outproj_call(o1, o2, g, x, mod, w_out, ln_g, ln_b):
    bsz, seq, d = x.shape
    tm = PROJ_TM
    half = o1.shape[2]
    row = lambda n: pl.BlockSpec((1, tm, n), lambda b, i: (b, i, 0))
    return pl.pallas_call(
        _outproj_body,
        out_shape=jax.ShapeDtypeStruct((bsz, seq, d), _F32),
        grid=(bsz, seq // tm),
        in_specs=[
            row(half), row(half), row(2 * half), row(d),
            pl.BlockSpec((1, 3, d), lambda b, i: (b, 0, 0)),
            pl.BlockSpec((2 * half, d), lambda b, i: (0, 0)),
            pl.BlockSpec((1, d), lambda b, i: (0, 0)),
            pl.BlockSpec((1, d), lambda b, i: (0, 0)),
        ],
        out_specs=row(d),
        compiler_params=_cparams(("arbitrary", "arbitrary")),
        name="outproj_deepnorm",
    )(o1, o2, g, x, mod, w_out.astype(_BF16), ln_g.reshape(1, d), ln_b.reshape(1, d))


ODD_COLS = 4096


def _prep_w_odd(w):
    offs = [0]
    for n in ODD_SIZES:
        offs.append(offs[-1] + n)
    cqn, cqr, ckv, ckr, iq, ik, iw, dq, dk, dv, gate = [w[:, offs[i]:offs[i + 1]] for i in range(len(ODD_SIZES))]
    iw_blk = jnp.concatenate([iw, jnp.zeros((w.shape[0], LANES - IDX_HEADS), w.dtype)], axis=1)
    return jnp.concatenate([cqn, cqr, ckv, jnp.tile(ckr, (1, 4)), iq, jnp.tile(ik, (1, 4)), iw_blk,
                            dq, dk, dv, gate], axis=1).astype(_BF16)


def _inproj_odd_body(x_ref, mod_ref, w_ref, cosr_ref, sinr_ref, cosi_ref, sini_ref, kvg_ref,
                     cqn_ref, cqr_ref, kvc_ref, iq_ref, ik_ref, iw_ref, dq_ref, dk_ref, dv_ref, g_ref):
    h = _modulated(x_ref, mod_ref)
    cosr, sinr = cosr_ref[...], sinr_ref[...]
    cosi, sini = cosi_ref[...], sini_ref[...]

    cqn_ref[0] = _dot(h, w_ref[:, 0:512]).astype(_BF16)
    acc = _dot(h, w_ref[:, 512:1024])
    for p in range(2):
        cqr_ref[0, :, p * LANES:(p + 1) * LANES] = _rope_piece(
            acc[:, p * LANES:(p + 1) * LANES], cosr, sinr, C_ROPE_DIM).astype(_BF16)
    ckv = acc[:, 256:384]
    ckv = ckv * lax.rsqrt(jnp.mean(ckv * ckv, axis=1, keepdims=True) + LN_EPS) * kvg_ref[...]
    kvc_ref[0, :, 0:LANES] = ckv.astype(_BF16)
    kvc_ref[0, :, LANES:2 * LANES] = _rope_piece(acc[:, 384:512], cosr, sinr, C_ROPE_DIM).astype(_BF16)
    acc = _dot(h, w_ref[:, 1024:1536])
    for p in range(2):
        iq_ref[0, :, p * LANES:(p + 1) * LANES] = _rope_piece(
            acc[:, p * LANES:(p + 1) * LANES], cosi, sini, IDX_DIM).astype(_BF16)
    ik_ref[0] = _rope_piece(acc[:, 256:384], cosi, sini, IDX_DIM).astype(_BF16)
    iw_ref[0] = acc[:, 384:512]
    dq_ref[0] = (_dot(h, w_ref[:, 1536:2048]) * (HEAD_DIM ** -0.5)).astype(_BF16)
    dk_ref[0] = _dot(h, w_ref[:, 2048:2560]).astype(_BF16)
    dv_ref[0] = _dot(h, w_ref[:, 2560:3072]).astype(_BF16)
    for p in range(2):
        gate = _dot(h, w_ref[:, 3072 + 512 * p:3584 + 512 * p])
        g_ref[0, :, 512 * p:512 * (p + 1)] = (gate * jax.nn.sigmoid(gate)).astype(_BF16)


def _inproj_odd_call(x, mod, w, rope_r, rope_i, kv_g):
    bsz, seq, d = x.shape
    tm = PROJ_TM
    row = lambda n: pl.BlockSpec((1, tm, n), lambda b, i: (b, i, 0))
    tab = pl.BlockSpec((tm, LANES), lambda b, i: (i, 0))
    outs = [(512, _BF16), (256, _BF16), (256, _BF16), (256, _BF16), (128, _BF16), (128, _F32),
            (512, _BF16), (512, _BF16), (512, _BF16), (1024, _BF16)]
    return pl.pallas_call(
        _inproj_odd_body,
        out_shape=[jax.ShapeDtypeStruct((bsz, seq, n), dt) for n, dt in outs],
        grid=(bsz, seq // tm),
        in_specs=[
            row(d),
            pl.BlockSpec((1, 3, d), lambda b, i: (b, 0, 0)),
            pl.BlockSpec((d, ODD_COLS), lambda b, i: (0, 0)),
            tab, tab, tab, tab,
            pl.BlockSpec((1, LANES), lambda b, i: (0, 0)),
        ],
        out_specs=[row(n) for n, _ in outs],
        compiler_params=_cparams(("arbitrary", "arbitrary")),
        name="inproj_odd",
    )(x, mod, w, rope_r[0], rope_r[1], rope_i[0], rope_i[1], kv_g.reshape(1, LANES))


def _dsa_body(cqn_ref, cqr_ref, iq_ref, iw_ref, kvc_ref, ik_ref, wuk_ref, wuv_ref, o_ref,
              qcat_ref, iqm_ref, key_ref, bias_ref):
    i = pl.program_id(1)
    tq, ck = DSA_TQ, DSA_CK
    seq = kvc_ref.shape[1]
    n_sel = float(min(DSA_TOPK, seq // 4))
    nch = i // (ck // tq) + 1
    lane = _lane_iota((1, LANES))
    att_scale = (C_NOPE_DIM + C_ROPE_DIM) ** -0.5
    idx_scale = (IDX_DIM * IDX_HEADS) ** -0.5

    for h in range(C_HEADS):
        p, hh = h // 2, h % 2
        qn = cqn_ref[0, :, p * LANES:(p + 1) * LANES]
        qn = jnp.where((lane // C_NOPE_DIM) == hh, qn, jnp.zeros_like(qn))
        qlat = _dot(qn, wuk_ref[p * LANES:(p + 1) * LANES, :]) * att_scale
        blk4 = h // 4
        qr = cqr_ref[0, :, blk4 * LANES:(blk4 + 1) * LANES].astype(_F32)
        qr = jnp.where((lane // C_ROPE_DIM) == (h % 4), qr, 0.0) * att_scale
        qcat_ref[h, :, 0:LANES] = qlat.astype(_BF16)
        qcat_ref[h, :, LANES:2 * LANES] = qr.astype(_BF16)
        iqh = iq_ref[0, :, blk4 * LANES:(blk4 + 1) * LANES]
        iqm_ref[h] = jnp.where((lane // IDX_DIM) == (h % 4), iqh, jnp.zeros_like(iqh))

    t_pos = i * tq + _row_iota((tq, ck))

    def score_body(c, _):
        off = pl.multiple_of(c * ck, ck)
        ik = ik_ref[0, pl.ds(off, ck), :]
        score = jnp.zeros((tq, ck), _F32)
        for h in range(IDX_HEADS):
            rel = jnp.maximum(_dot_t(iqm_ref[h], ik), 0.0)
            score = score + iw_ref[0, :, h:h + 1] * rel
        score = score * idx_scale
        score = jnp.where(score == 0.0, 0.0, score)
        bits = pltpu.bitcast(score, jnp.int32)
        key = bits ^ ((bits >> 31) & 0x7FFFFFFF)
        s_pos = off + _lane_iota((tq, ck))
        key_ref[c] = jnp.where(s_pos <= t_pos, key, INT_MIN)
        return 0

    lax.fori_loop(0, nch, score_body, 0)

    def count(pred):
        def body(c, acc):
            off = pl.multiple_of(c * ck, ck)
            hit = jnp.where(pred(key_ref[c], off), 1.0, 0.0)
            for q in range(ck // LANES):
                acc = acc + hit[:, q * LANES:(q + 1) * LANES]
            return acc
        acc = lax.fori_loop(0, nch, body, jnp.zeros((tq, LANES), _F32))
        return jnp.sum(acc, axis=1, keepdims=True)

    def bit_body(b, carry):
        t, cnt_t = carry
        cand = t | jnp.left_shift(jnp.int32(1), 31 - b)
        cand_c = jnp.broadcast_to(cand ^ INT_MIN, (tq, ck))
        cnt = count(lambda kc, off: kc >= cand_c)
        ok = cnt >= n_sel
        return jnp.where(ok, cand, t), jnp.where(ok, cnt, cnt_t)

    t0 = jnp.zeros((tq, 1), jnp.int32)
    t, cnt_t = lax.fori_loop(0, 32, bit_body, (t0, jnp.full((tq, 1), float(seq), _F32)))
    thr = t ^ INT_MIN
    thr_c = jnp.broadcast_to(thr, (tq, ck))

    excess = jnp.max(jnp.where((cnt_t > n_sel) & (t != 0), 1.0, 0.0)) > 0.0

    def tie_limit():
        cnt_gt = count(lambda kc, off: kc > thr_c)
        room = n_sel - cnt_gt

        def lim_body(b, lim):
            cand = lim | jnp.left_shift(jnp.int32(1), 12 - b)
            cand_c = jnp.broadcast_to(cand, (tq, ck))
            cnt = count(lambda kc, off: (kc == thr_c) & ((off + _lane_iota((tq, ck))) < cand_c))
            return jnp.where(cnt <= room, cand, lim)

        return lax.fori_loop(0, 13, lim_body, jnp.zeros((tq, 1), jnp.int32))

    limit = lax.cond(excess, tie_limit, lambda: jnp.full((tq, 1), 2 * seq, jnp.int32))
    limit_c = jnp.broadcast_to(limit, (tq, ck))

    def bias_body(c, _):
        off = pl.multiple_of(c * ck, ck)
        kc = key_ref[c]
        s_pos = off + _lane_iota((tq, ck))
        chosen = (kc > thr_c) | ((kc == thr_c) & (s_pos < limit_c))
        chosen = chosen & (s_pos <= t_pos)
        bias_ref[c] = jnp.where(chosen, 0.0, NEG_BIG)
        return 0

    lax.fori_loop(0, nch, bias_body, 0)

    for p in range(C_HEADS // 2):
        out_pair = jnp.zeros((tq, LANES), _F32)
        for hh in range(2):
            h = 2 * p + hh

            def att_body(c, carry, h=h):
                m, l, acc = carry
                off = pl.multiple_of(c * ck, ck)
                kv = kvc_ref[0, pl.ds(off, ck), :]
                s = _dot_t(qcat_ref[h], kv) + bias_ref[c]
                m_new = jnp.maximum(m, jnp.max(s, axis=1, keepdims=True))
                alpha = jnp.exp(m - m_new)
                e = jnp.exp(s - m_new)
                l = alpha * l + jnp.sum(e, axis=1, keepdims=True)
                acc = alpha * acc + _dot(e.astype(_BF16), kv[:, 0:C_KV_LATENT])
                return m_new, l, acc

            m0 = jnp.full((tq, 1), NEG_BIG, _F32)
            l0 = jnp.zeros((tq, 1), _F32)
            a0 = jnp.zeros((tq, C_KV_LATENT), _F32)
            m, l, acc = lax.fori_loop(0, nch, att_body, (m0, l0, a0))
            out_pair = out_pair + _dot((acc / l).astype(_BF16), wuv_ref[h])
        o_ref[0, :, p * LANES:(p + 1) * LANES] = out_pair.astype(o_ref.dtype)


def _dsa_call(cqn, cqr, iq, iw, kvc, ik, wuk_t, wuv_x):
    bsz, seq, _ = cqn.shape
    tq = DSA_TQ
    row = lambda n: pl.BlockSpec((1, tq, n), lambda b, i: (b, i, 0))
    full = lambda n: pl.BlockSpec((1, seq, n), lambda b, i: (b, 0, 0))
    return pl.pallas_call(
        _dsa_body,
        out_shape=jax.ShapeDtypeStruct((bsz, seq, C_HEADS * C_V_DIM), _BF16),
        grid=(bsz, seq // tq),
        in_specs=[
            row(512), row(256), row(256), row(128), full(256), full(128),
            pl.BlockSpec(wuk_t.shape, lambda b, i: (0, 0)),
            pl.BlockSpec(wuv_x.shape, lambda b, i: (0, 0, 0)),
        ],
        out_specs=row(C_HEADS * C_V_DIM),
        scratch_shapes=[
            pltpu.VMEM((C_HEADS, tq, 2 * LANES), _BF16),
            pltpu.VMEM((IDX_HEADS, tq, LANES), _BF16),
            pltpu.VMEM((seq // DSA_CK, tq, DSA_CK), jnp.int32),
            pltpu.VMEM((seq // DSA_CK, tq, DSA_CK), _F32),
        ],
        compiler_params=_cparams(("arbitrary", "arbitrary")),
        name="dsa",
    )(cqn, cqr, iq, iw, kvc, ik, wuk_t, wuv_x)


def _inproj_oddT_body(x_ref, mod_ref, w_ref, cosr_ref, sinr_ref, cosi_ref, sini_ref, kvg_ref,
                      cqn_ref, cqr_ref, kvc_ref, ckvt_ref, iq_ref, ik_ref, iw_ref,
                      dq_ref, dk_ref, dv_ref, g_ref):
    h = _modulated(x_ref, mod_ref)
    cosr, sinr = cosr_ref[...], sinr_ref[...]
    cosi, sini = cosi_ref[...], sini_ref[...]

    acc = _dot(h, w_ref[:, 0:512])
    for p in range(4):
        cqn_ref[0, p * LANES:(p + 1) * LANES, :] = acc[:, p * LANES:(p + 1) * LANES].T.astype(_BF16)
    acc = _dot(h, w_ref[:, 512:1024])
    for p in range(2):
        piece = _rope_piece(acc[:, p * LANES:(p + 1) * LANES], cosr, sinr, C_ROPE_DIM)
        cqr_ref[0, p * LANES:(p + 1) * LANES, :] = piece.T.astype(_BF16)
    ckv = acc[:, 256:384]
    ckv = ckv * lax.rsqrt(jnp.mean(ckv * ckv, axis=1, keepdims=True) + LN_EPS) * kvg_ref[...]
    kvc_ref[0, :, 0:LANES] = ckv.astype(_BF16)
    ckvt_ref[0, 0] = ckv.T.astype(_BF16)
    kvc_ref[0, :, LANES:2 * LANES] = _rope_piece(acc[:, 384:512], cosr, sinr, C_ROPE_DIM).astype(_BF16)
    acc = _dot(h, w_ref[:, 1024:1536])
    for p in range(2):
        piece = _rope_piece(acc[:, p * LANES:(p + 1) * LANES], cosi, sini, IDX_DIM)
        iq_ref[0, p * LANES:(p + 1) * LANES, :] = piece.T.astype(_BF16)
    ik_ref[0] = _rope_piece(acc[:, 256:384], cosi, sini, IDX_DIM).astype(_BF16)
    iw_ref[0] = acc[:, 384:512].T[0:IDX_HEADS, :]
    dq_ref[0] = (_dot(h, w_ref[:, 1536:2048]) * (HEAD_DIM ** -0.5)).astype(_BF16)
    dk_ref[0] = _dot(h, w_ref[:, 2048:2560]).astype(_BF16)
    dv_ref[0] = _dot(h, w_ref[:, 2560:3072]).astype(_BF16)
    for p in range(2):
        gate = _dot(h, w_ref[:, 3072 + 512 * p:3584 + 512 * p])
        g_ref[0, :, 512 * p:512 * (p + 1)] = (gate * jax.nn.sigmoid(gate)).astype(_BF16)


def _inproj_oddT_call(x, mod, w, rope_r, rope_i, kv_g):
    bsz, seq, d = x.shape
    tm = DSA_CK
    row = lambda n: pl.BlockSpec((1, tm, n), lambda b, i: (b, i, 0))
    col = lambda n: pl.BlockSpec((1, n, tm), lambda b, i: (b, 0, i))
    tab = pl.BlockSpec((tm, LANES), lambda b, i: (i, 0))
    tok = lambda n, dt: jax.ShapeDtypeStruct((bsz, seq, n), dt)
    feat = lambda n, dt: jax.ShapeDtypeStruct((bsz, n, seq), dt)
    return pl.pallas_call(
        _inproj_oddT_body,
        out_shape=[
            feat(512, _BF16), feat(256, _BF16), tok(256, _BF16),
            jax.ShapeDtypeStruct((bsz, seq // tm, C_KV_LATENT, tm), _BF16),
            feat(256, _BF16), tok(128, _BF16), feat(IDX_HEADS, _F32),
            tok(512, _BF16), tok(512, _BF16), tok(512, _BF16), tok(1024, _BF16),
        ],
        grid=(bsz, seq // tm),
        in_specs=[
            row(d),
            pl.BlockSpec((1, 3, d), lambda b, i: (b, 0, 0)),
            pl.BlockSpec((d, ODD_COLS), lambda b, i: (0, 0)),
            tab, tab, tab, tab,
            pl.BlockSpec((1, LANES), lambda b, i: (0, 0)),
        ],
        out_specs=[
            col(512), col(256), row(256),
            pl.BlockSpec((1, 1, C_KV_LATENT, tm), lambda b, i: (b, i, 0, 0)),
            col(256), row(128), col(IDX_HEADS),
            row(512), row(512), row(512), row(1024),
        ],
        compiler_params=_cparams(("arbitrary", "arbitrary")),
        name="inproj_odd",
    )(x, mod, w, rope_r[0], rope_r[1], rope_i[0], rope_i[1], kv_g.reshape(1, LANES))


def _dsaT_body(cqn_ref, cqr_ref, iq_ref, iw_ref, kvc_ref, ckvt_ref, ik_ref, wuk_ref, wuvt_ref, o_ref,
               qct_ref, iqt_ref, sc_ref, acc_ref):
    i = pl.program_id(1)
    tq, ck = DSA_TQ, DSA_CK
    seq = kvc_ref.shape[1]
    n_sel = float(min(DSA_TOPK, seq // 4))
    nch = i // (ck // tq) + 1
    att_scale = (C_NOPE_DIM + C_ROPE_DIM) ** -0.5
    idx_scale = (IDX_DIM * IDX_HEADS) ** -0.5
    hq = C_HEADS * tq

    row128 = _row_iota((LANES, tq))
    for h in range(C_HEADS):
        cols = slice(h * tq, (h + 1) * tq)
        qn_t = cqn_ref[0, h * C_NOPE_DIM:(h + 1) * C_NOPE_DIM, :]
        qlat_t = _dot(wuk_ref[h], qn_t) * att_scale
        qct_ref[0:C_KV_LATENT, cols] = qlat_t.astype(_BF16)
        qr_t = cqr_ref[0, h * C_ROPE_DIM:(h + 1) * C_ROPE_DIM, :].astype(_F32) * att_scale
        qct_ref[C_KV_LATENT:C_KV_LATENT + C_ROPE_DIM, cols] = qr_t.astype(_BF16)
        qct_ref[C_KV_LATENT + C_ROPE_DIM:, cols] = jnp.zeros((LANES - C_ROPE_DIM, tq), _BF16)
        blk4 = h // 4
        iq_t = iq_ref[0, blk4 * LANES:(blk4 + 1) * LANES, :]
        iqt_ref[:, cols] = jnp.where((row128 // IDX_DIM) == (h % 4), iq_t, jnp.zeros_like(iq_t))

    t_pos = i * tq + _lane_iota((ck, tq))
    key_row = _row_iota((ck, tq))

    def score_body(c, _):
        off = pl.multiple_of(c * ck, ck)
        rel = _dot(ik_ref[0, pl.ds(off, ck), :], iqt_ref[...])
        score = jnp.zeros((ck, tq), _F32)
        for h in range(IDX_HEADS):
            score = score + jnp.maximum(rel[:, h * tq:(h + 1) * tq], 0.0) * iw_ref[0, h:h + 1, :]
        score = score * idx_scale
        sc_ref[c] = jnp.where(off + key_row <= t_pos, score, -jnp.inf)
        return 0

    lax.fori_loop(0, nch, score_body, 0)

    def count(pred):
        lanes_acc = 64
        def body(c, acc):
            hit = jnp.where(pred(sc_ref[c], c * ck + key_row), 1.0, 0.0)
            return acc + jnp.sum(hit.reshape(ck // lanes_acc, lanes_acc, tq), axis=0)
        acc = lax.fori_loop(0, nch, body, jnp.zeros((lanes_acc, tq), _F32))
        return jnp.sum(acc, axis=0, keepdims=True)

    def ordered_float(u):
        key = u ^ INT_MIN
        return pltpu.bitcast(key ^ ((key >> 31) & 0x7FFFFFFF), _F32)

    def bit_body(b, carry):
        t, cnt_t = carry
        cand = t | jnp.left_shift(jnp.int32(1), 31 - b)
        cand_f = ordered_float(cand)
        cnt = count(lambda sc, pos: sc >= cand_f)
        ok = cnt >= n_sel
        return jnp.where(ok, cand, t), jnp.where(ok, cnt, cnt_t)

    t0 = jnp.zeros((1, tq), jnp.int32)
    t, cnt_t = lax.fori_loop(0, 32, bit_body, (t0, jnp.full((1, tq), float(seq), _F32)))
    thr = jnp.where(t == 0, -jnp.inf, ordered_float(t))

    excess = jnp.max(jnp.where((cnt_t > n_sel) & (t != 0), 1.0, 0.0)) > 0.0

    def tie_limit():
        room = n_sel - count(lambda sc, pos: sc > thr)

        def lim_body(b, lim):
            cand = lim | jnp.left_shift(jnp.int32(1), 12 - b)
            cnt = count(lambda sc, pos: (sc == thr) & (pos < cand))
            return jnp.where(cnt <= room, cand, lim)

        return lax.fori_loop(0, 13, lim_body, jnp.zeros((1, tq), jnp.int32))

    limit = lax.cond(excess, tie_limit, lambda: jnp.full((1, tq), 2 * seq, jnp.int32))

    def bias_body(c, _):
        sc = sc_ref[c]
        pos = c * ck + key_row
        chosen = ((sc > thr) | ((sc == thr) & (pos < limit))) & (pos <= t_pos)
        sc_ref[c] = jnp.where(chosen, 0.0, NEG_BIG)
        return 0

    lax.fori_loop(0, nch, bias_body, 0)

    acc_ref[...] = jnp.zeros_like(acc_ref)
    grp = 2 * tq

    def att_body(c, carry):
        off = pl.multiple_of(c * ck, ck)
        kv = kvc_ref[0, pl.ds(off, ck), :]
        bias = sc_ref[c]
        bias2 = jnp.concatenate([bias, bias], axis=1)
        ngrp = hq // grp
        gcols = [slice(g * grp, (g + 1) * grp) for g in range(ngrp)]
        ss = [_dot(kv, qct_ref[:, cols]) + bias2 for cols in gcols]
        ms, ls, es, alphas = [], [], [], []
        for g, cols in enumerate(gcols):
            m, l = carry[0][:, cols], carry[1][:, cols]
            m_new = jnp.maximum(m, jnp.max(ss[g], axis=0, keepdims=True))
            alpha = jnp.exp(m - m_new)
            e = jnp.exp(ss[g] - m_new)
            ls.append(alpha * l + jnp.sum(e, axis=0, keepdims=True))
            ms.append(m_new)
            alphas.append(alpha)
            es.append(e.astype(_BF16))
        kv_t = ckvt_ref[0, c]
        for g, cols in enumerate(gcols):
            acc_ref[:, cols] = alphas[g] * acc_ref[:, cols] + _dot(kv_t, es[g])
        return jnp.concatenate(ms, axis=1), jnp.concatenate(ls, axis=1)

    m0 = jnp.full((1, hq), NEG_BIG, _F32)
    l0 = jnp.zeros((1, hq), _F32)
    _, l = lax.fori_loop(0, nch, att_body, (m0, l0))
    o_lat_t = (acc_ref[...] / l).astype(_BF16)
    outs = [_dot(wuvt_ref[h], o_lat_t[:, h * tq:(h + 1) * tq]) for h in range(C_HEADS)]
    o_ref[0] = jnp.concatenate(outs, axis=0).T.astype(o_ref.dtype)


def _dsaT_call(cqn_t, cqr_t, iq_t, iw_t, kvc, ckv_t, ik, w_uk, wuv_t):
    bsz, seq, _ = kvc.shape
    tq = DSA_TQ
    qcol = lambda n: pl.BlockSpec((1, n, tq), lambda b, i: (b, 0, i))
    full = lambda n: pl.BlockSpec((1, seq, n), lambda b, i: (b, 0, 0))
    whole = lambda a: pl.BlockSpec(a.shape, lambda b, i: (0,) * a.ndim)
    width = C_HEADS * C_V_DIM
    return pl.pallas_call(
        _dsaT_body,
        out_shape=jax.ShapeDtypeStruct((bsz, seq, width), _BF16),
        grid=(bsz, seq // tq),
        in_specs=[
            qcol(512), qcol(256), qcol(256), qcol(IDX_HEADS), full(256),
            pl.BlockSpec((1,) + ckv_t.shape[1:], lambda b, i: (b, 0, 0, 0)),
            full(128), whole(w_uk), whole(wuv_t),
        ],
        out_specs=pl.BlockSpec((1, tq, width), lambda b, i: (b, i, 0)),
        scratch_shapes=[
            pltpu.VMEM((2 * LANES, C_HEADS * tq), _BF16),
            pltpu.VMEM((LANES, IDX_HEADS * tq), _BF16),
            pltpu.VMEM((seq // DSA_CK, DSA_CK, tq), _F32),
            pltpu.VMEM((C_KV_LATENT, C_HEADS * tq), _F32),
        ],
        compiler_params=_cparams(("arbitrary", "arbitrary")),
        name="dsa",
    )(cqn_t, cqr_t, iq_t, iw_t, kvc, ckv_t, ik, w_uk, wuv_t)


def _sb_body(q_ref, k_ref, v_ref, o_ref):
    i = pl.program_id(2)
    t = SB_TQ
    q = q_ref[0]
    lane = _lane_iota((1, LANES))
    row = _row_iota((t, t))
    col = _lane_iota((t, t))
    later = jnp.where(row > col, 1.0, 0.0).astype(_BF16)
    outs = []
    for hh in range(2):
        qh = jnp.where((lane // HEAD_DIM) == hh, q, jnp.zeros_like(q))

        def cond(carry):
            j, _, _, top = carry
            return (j >= 0) & (top > SB_UNDERFLOW)

        def body(carry):
            j, rest, acc, _ = carry
            off = pl.multiple_of(j * t, t)
            z = _dot_t(qh, k_ref[0, pl.ds(off, t), :])
            past = (off + col) < (i * t + row)
            log_beta = jnp.minimum(z, 0.0) - jnp.log1p(jnp.exp(-jnp.abs(z)))
            log_keep = jnp.where(past, log_beta - z, 0.0)
            hi = log_keep.astype(_BF16)
            lo = (log_keep - hi.astype(_F32)).astype(_BF16)
            log_after = _dot(hi, later) + _dot(lo, later) + rest
            a = jnp.where(past, jnp.exp(log_beta + log_after), 0.0)
            acc = acc + _dot(a.astype(_BF16), v_ref[0, pl.ds(off, t), :])
            rest = rest + jnp.sum(log_keep, axis=1, keepdims=True)
            return j - 1, rest, acc, jnp.max(rest)

        init = (i, jnp.zeros((t, 1), _F32), jnp.zeros((t, LANES), _F32), jnp.float32(0.0))
        _, _, acc, _ = lax.while_loop(cond, body, init)
        outs.append(acc)
    o_ref[0] = jnp.where(lane < HEAD_DIM, outs[0], outs[1]).astype(o_ref.dtype)


def _sb_call(q, k, v):
    bsz, seq, width = q.shape
    t = SB_TQ
    return pl.pallas_call(
        _sb_body,
        out_shape=jax.ShapeDtypeStruct((bsz, seq, width), _BF16),
        grid=(bsz, width // LANES, seq // t),
        in_specs=[
            pl.BlockSpec((1, t, LANES), lambda b, h, i: (b, i, h)),
            pl.BlockSpec((1, seq, LANES), lambda b, h, i: (b, 0, h)),
            pl.BlockSpec((1, seq, LANES), lambda b, h, i: (b, 0, h)),
        ],
        out_specs=pl.BlockSpec((1, t, LANES), lambda b, h, i: (b, i, h)),
        compiler_params=_cparams(("arbitrary", "arbitrary", "arbitrary")),
        name="stick_breaking",
    )(q, k, v)


def _even_layer(x, mod, w_in, sinks, w_out, ln_g, ln_b, rope_h):
    aq, ak, bq, bk, av, bv, g = _inproj_even_call(x, mod, _prep_w_even(w_in), *rope_h)
    oa = _moba_call(aq, ak, av)
    ob = _swa_call(bq, bk, bv, sinks)
    return _outproj_call(oa, ob, g, x, mod, w_out, ln_g, ln_b)


def _odd_layer(x, mod, w_in, kv_g, w_uk, w_uv, w_out, ln_g, ln_b, rope_r, rope_i):
    cqn_t, cqr_t, kvc, ckv_t, iq_t, ik, iw_t, dq, dk, dv, g = _inproj_oddT_call(
        x, mod, _prep_w_odd(w_in), rope_r, rope_i, kv_g)
    oc = _dsaT_call(cqn_t, cqr_t, iq_t, iw_t, kvc, ckv_t, ik,
                    w_uk.astype(_BF16), w_uv.transpose(0, 2, 1).astype(_BF16))
    od = _sb_call(dq, dk, dv)
    return _outproj_call(oc, od, g, x, mod, w_out, ln_g, ln_b)


def kernel(x, c, w_ada, b_ada, w_in_even, sink_logits, w_in_odd, kv_norm_g, w_uk, w_uv, w_out, ln_g, ln_b):
    bsz, seq, d = x.shape
    rope_h = _rope_tables(seq, HEAD_DIM, LANES)
    rope_r = _rope_tables(seq, C_ROPE_DIM, LANES)
    rope_i = _rope_tables(seq, IDX_DIM, LANES)
    mods = _ada_call(c, w_ada, b_ada).reshape(DEPTH, bsz, 3, d)
    for layer in range(DEPTH):
        mod = mods[layer]
        j = layer // 2
        if layer % 2 == 0:
            x = _even_layer(x, mod, w_in_even[j], sink_logits[j], w_out[layer], ln_g[layer], ln_b[layer], rope_h)
        else:
            x = _odd_layer(x, mod, w_in_odd[j], kv_norm_g[j], w_uk[j], w_uv[j], w_out[layer],
                           ln_g[layer], ln_b[layer], rope_r, rope_i)
    return x
```

```python
import jax
import jax.numpy as jnp
from jax import lax
from jax.experimental import pallas as pl
from jax.experimental.pallas import tpu as pltpu

D_MODEL = 1024
DEPTH = 2
HEAD_DIM = 64
ROPE_THETA = 10000.0
LN_EPS = 1e-5
A_HEADS = 8
MOBA_BLOCK = 256
MOBA_TOPK = 3
B_HEADS = 8
B_KV_HEADS = 2
SWA_WINDOW = 128
C_HEADS = 8
C_NOPE_DIM = 64
C_ROPE_DIM = 32
C_V_DIM = 64
C_KV_LATENT = 128
IDX_HEADS = 8
IDX_DIM = 32
DSA_TOPK = 256
D_HEADS = 8
MIX_WIDTH = 1024
EVEN_SIZES = (512, 512, 512, 512, 128, 128, 1024)
ODD_SIZES = (512, 256, 128, 32, 256, 32, 8, 512, 512, 512, 1024)
DEEPNORM_ALPHA = (2 * DEPTH) ** 0.25

LANES = 128
NEG_BIG = -1e30
INT_MIN = -2 ** 31
SB_UNDERFLOW = -104.0

PROJ_TM = 512
DSA_TQ = 128
DSA_CK = 512
SB_T = 128
VMEM_LIMIT = 48 * 1024 * 1024

_BF16 = jnp.bfloat16
_F32 = jnp.float32


def _cparams(sem):
    return pltpu.CompilerParams(dimension_semantics=sem, vmem_limit_bytes=VMEM_LIMIT)


def _dot_t(a, b):
    return lax.dot_general(a, b, (((1,), (1,)), ((), ())), preferred_element_type=_F32)


def _dot(a, b):
    return jnp.dot(a, b, preferred_element_type=_F32)


def _lane_iota(shape):
    return lax.broadcasted_iota(jnp.int32, shape, len(shape) - 1)


def _row_iota(shape):
    return lax.broadcasted_iota(jnp.int32, shape, len(shape) - 2)


def _head_pair_rhs(q_t, head_dim):
    frow = _row_iota(q_t.shape)
    return jnp.concatenate(
        [jnp.where((frow // head_dim) == hh, q_t, jnp.zeros_like(q_t)) for hh in range(2)], axis=1)


def _ada_body(c_ref, w_ref, b_ref, o_ref):
    c = c_ref[...]
    cond = c * jax.nn.sigmoid(c)
    o_ref[0] = _dot(cond, w_ref[0]) + b_ref[0]


def _ada_call(c, w_ada, b_ada):
    depth, d, n3 = w_ada.shape
    bsz = c.shape[0]
    nb = n3 // d
    return pl.pallas_call(
        _ada_body,
        out_shape=jax.ShapeDtypeStruct((depth, bsz, n3), _F32),
        grid=(depth, nb),
        in_specs=[
            pl.BlockSpec((bsz, d), lambda l, j: (0, 0)),
            pl.BlockSpec((1, d, d), lambda l, j: (l, 0, j)),
            pl.BlockSpec((1, 1, d), lambda l, j: (l, 0, j)),
        ],
        out_specs=pl.BlockSpec((1, bsz, d), lambda l, j: (l, 0, j)),
        compiler_params=_cparams(("arbitrary", "arbitrary")),
        name="ada_mod",
    )(c, w_ada, b_ada.reshape(depth, 1, n3))


def _rope_tables(seq, dim, period_lanes):
    inv = 1.0 / (ROPE_THETA ** (jnp.arange(0, dim, 2, dtype=_F32) / dim))
    ang = jnp.arange(seq, dtype=_F32)[:, None] * inv[None, :]
    cos, sin = jnp.cos(ang), jnp.sin(ang)
    cos_h = jnp.concatenate([cos, cos], axis=1)
    sin_h = jnp.concatenate([-sin, sin], axis=1)
    reps = period_lanes // dim
    return jnp.tile(cos_h, (1, reps)), jnp.tile(sin_h, (1, reps))


def _rope_piece(x, cos, sin_signed, dim):
    half = dim // 2
    first = (_lane_iota(x.shape) % dim) < half
    partner = jnp.where(first, pltpu.roll(x, LANES - half, 1), pltpu.roll(x, half, 1))
    return x * cos + partner * sin_signed


def _modulated(x_ref, mod_ref):
    x = x_ref[0]
    shift = mod_ref[0, 0:1, :]
    scale = mod_ref[0, 1:2, :]
    return (x * (1.0 + scale) + shift).astype(_BF16)


def _silu_gate(h, w_ref, col0, g_ref):
    for p in range(2):
        gate = _dot(h, w_ref[:, col0 + 512 * p:col0 + 512 * (p + 1)])
        g_ref[0, :, 512 * p:512 * (p + 1)] = (gate * jax.nn.sigmoid(gate)).astype(_BF16)


def _store_blocks_t(acc, out_ref, blk):
    tm, n = acc.shape
    for r in range(tm // blk):
        for p in range(n // LANES):
            out_ref[0, r, p * LANES:(p + 1) * LANES, :] = (
                acc[r * blk:(r + 1) * blk, p * LANES:(p + 1) * LANES].T.astype(_BF16))


EVEN_COLS = 3584


def _prep_w_even(w):
    offs = [0]
    for n in EVEN_SIZES:
        offs.append(offs[-1] + n)
    aq, ak, av, bq, bk, bv, gate = [w[:, offs[i]:offs[i + 1]] for i in range(len(EVEN_SIZES))]

    def dup(t):
        parts = []
        for g in range(B_KV_HEADS):
            blk = t[:, g * HEAD_DIM:(g + 1) * HEAD_DIM]
            parts += [blk, blk]
        return jnp.concatenate(parts, axis=1)

    return jnp.concatenate([aq, ak, bq, dup(bk), av, dup(bv), gate], axis=1).astype(_BF16)


def _inproj_even_body(x_ref, mod_ref, w_ref, cos_ref, sin_ref,
                      aqt_ref, ak_ref, bq_ref, bk_ref, avt_ref, bv_ref, g_ref):
    h = _modulated(x_ref, mod_ref)
    cos = cos_ref[...]
    sin = sin_ref[...]
    q_scale = HEAD_DIM ** -0.5

    def roped(col0, ncols, out_ref, scale, transposed=False):
        acc = _dot(h, w_ref[:, col0:col0 + ncols])
        for p in range(ncols // LANES):
            piece = _rope_piece(acc[:, p * LANES:(p + 1) * LANES], cos, sin, HEAD_DIM)
            if scale != 1.0:
                piece = piece * scale
            if transposed:
                out_ref[0, p * LANES:(p + 1) * LANES, :] = piece.T.astype(_BF16)
            else:
                out_ref[0, :, p * LANES:(p + 1) * LANES] = piece.astype(_BF16)

    roped(0, 512, aqt_ref, q_scale, transposed=True)
    roped(512, 512, ak_ref, 1.0)
    roped(1024, 512, bq_ref, q_scale)
    roped(1536, 256, bk_ref, 1.0)
    _store_blocks_t(_dot(h, w_ref[:, 1792:2304]), avt_ref, MOBA_BLOCK)
    bv_ref[0] = _dot(h, w_ref[:, 2304:2560]).astype(_BF16)
    _silu_gate(h, w_ref, 2560, g_ref)


def _inproj_even_call(x, mod, w, cos, sin):
    bsz, seq, d = x.shape
    tm = PROJ_TM
    row = lambda n: pl.BlockSpec((1, tm, n), lambda b, i: (b, i, 0))
    tok = lambda n: jax.ShapeDtypeStruct((bsz, seq, n), _BF16)
    per_tile = tm // MOBA_BLOCK
    return pl.pallas_call(
        _inproj_even_body,
        out_shape=[
            jax.ShapeDtypeStruct((bsz, 512, seq), _BF16), tok(512), tok(512), tok(256),
            jax.ShapeDtypeStruct((bsz, seq // MOBA_BLOCK, 512, MOBA_BLOCK), _BF16), tok(256), tok(1024),
        ],
        grid=(bsz, seq // tm),
        in_specs=[
            row(d),
            pl.BlockSpec((1, 3, d), lambda b, i: (b, 0, 0)),
            pl.BlockSpec((d, EVEN_COLS), lambda b, i: (0, 0)),
            pl.BlockSpec((tm, LANES), lambda b, i: (i, 0)),
            pl.BlockSpec((tm, LANES), lambda b, i: (i, 0)),
        ],
        out_specs=[
            pl.BlockSpec((1, 512, tm), lambda b, i: (b, 0, i)), row(512), row(512), row(256),
            pl.BlockSpec((1, per_tile, 512, MOBA_BLOCK), lambda b, i: (b, i, 0, 0)), row(256), row(1024),
        ],
        compiler_params=_cparams(("arbitrary", "arbitrary")),
        name="inproj_even",
    )(x, mod, w, cos, sin)


def _moba_body(qt_ref, k_ref, vt_ref, o_ref, km_ref, selb_ref, acc_ref):
    qi = pl.program_id(2)
    tq = MOBA_BLOCK
    nblk = km_ref.shape[0]
    width = 2 * tq

    @pl.when(qi == 0)
    def _():
        for r in range(nblk):
            blk_k = k_ref[0, r * tq:(r + 1) * tq, :].astype(_F32)
            km_ref[r:r + 1, :] = jnp.mean(blk_k, axis=0, keepdims=True)

    qcat = _head_pair_rhs(qt_ref[0], HEAD_DIM)

    blk = _row_iota((nblk, width))
    blkf = blk.astype(_F32)
    valid = blk < qi
    gate = jnp.where(valid, _dot(km_ref[...].astype(_BF16), qcat), -jnp.inf)
    sel = jnp.zeros((nblk, width), _F32)
    for _ in range(MOBA_TOPK):
        top = jnp.max(gate, axis=0, keepdims=True)
        first = jnp.min(jnp.where(gate == top, blkf, float(nblk)), axis=0, keepdims=True)
        pick = blkf == first
        sel = jnp.where(pick, 1.0, sel)
        gate = jnp.where(pick, -jnp.inf, gate)
    selb_ref[...] = jnp.where(valid & (sel > 0.0), 0.0, NEG_BIG)

    acc_ref[...] = jnp.zeros_like(acc_ref)

    def step(carry, blocks):
        m, l = carry
        ss = [_dot(k_ref[0, pl.ds(pl.multiple_of(j * tq, tq), tq), :], qcat) + bias for j, bias in blocks]
        m_new = jnp.maximum(m, jnp.maximum(jnp.max(ss[0], axis=0, keepdims=True),
                                           jnp.max(ss[1], axis=0, keepdims=True)))
        alpha = jnp.exp(m - m_new)
        es = [jnp.exp(s - m_new) for s in ss]
        l = alpha * l + jnp.sum(es[0], axis=0, keepdims=True) + jnp.sum(es[1], axis=0, keepdims=True)
        for hh in range(2):
            rows = slice(hh * HEAD_DIM, (hh + 1) * HEAD_DIM)
            cols = slice(hh * tq, (hh + 1) * tq)
            pv = sum(_dot(vt_ref[0, j, rows, :], e[:, cols].astype(_BF16)) for (j, _), e in zip(blocks, es))
            acc_ref[rows, :] = alpha[:, cols] * acc_ref[rows, :] + pv
        return m_new, l

    def pair_body(t, carry):
        return step(carry, [(2 * t, selb_ref[pl.ds(2 * t, 1), :]),
                            (2 * t + 1, selb_ref[pl.ds(2 * t + 1, 1), :])])

    carry = (jnp.full((1, width), NEG_BIG, _F32), jnp.zeros((1, width), _F32))
    carry = lax.fori_loop(0, qi // 2, pair_body, carry)
    key_row = _row_iota((tq, width))
    q_lane = _lane_iota((tq, width)) % tq
    causal = jnp.where(key_row <= q_lane, 0.0, NEG_BIG)
    last = jnp.maximum(qi - 1, 0)
    last_bias = jnp.where(qi % 2 == 1, selb_ref[pl.ds(last, 1), :], NEG_BIG)
    _, l = step(carry, [(qi, causal), (last, last_bias)])
    inv = 1.0 / l
    out_t = jnp.concatenate([acc_ref[hh * HEAD_DIM:(hh + 1) * HEAD_DIM, :] * inv[:, hh * tq:(hh + 1) * tq]
                             for hh in range(2)], axis=0)
    o_ref[0] = out_t.T.astype(o_ref.dtype)


def _moba_call(q_t, k, v_t):
    bsz, seq, width = k.shape
    tq = MOBA_BLOCK
    nblk = seq // tq
    return pl.pallas_call(
        _moba_body,
        out_shape=jax.ShapeDtypeStruct((bsz, seq, width), _BF16),
        grid=(bsz, width // LANES, nblk),
        in_specs=[
            pl.BlockSpec((1, LANES, tq), lambda b, h, i: (b, h, i)),
            pl.BlockSpec((1, seq, LANES), lambda b, h, i: (b, 0, h)),
            pl.BlockSpec((1, nblk, LANES, tq), lambda b, h, i: (b, 0, h, 0)),
        ],
        out_specs=pl.BlockSpec((1, tq, LANES), lambda b, h, i: (b, i, h)),
        scratch_shapes=[
            pltpu.VMEM((nblk, LANES), _F32),
            pltpu.VMEM((nblk, 2 * tq), _F32),
            pltpu.VMEM((LANES, tq), _F32),
        ],
        compiler_params=_cparams(("arbitrary", "arbitrary", "arbitrary")),
        name="moba",
    )(q_t, k, v_t)


def _swa_body(sink_ref, q_ref, kp_ref, kc_ref, vp_ref, vc_ref, o_ref):
    i = pl.program_id(1)
    w = SWA_WINDOW
    lane = _lane_iota((1, LANES))
    row = _row_iota((w, 2 * w))
    col = _lane_iota((w, 2 * w))
    mask = (col > row) & (col <= row + w) & ((col >= w) | (i > 0))
    pairs_per_kv = (B_HEADS // B_KV_HEADS) // 2
    for p in range(B_HEADS // 2):
        g = p // pairs_per_kv
        qp = q_ref[0, :, p * LANES:(p + 1) * LANES]
        k = jnp.concatenate([kp_ref[0, :, g * LANES:(g + 1) * LANES],
                             kc_ref[0, :, g * LANES:(g + 1) * LANES]], axis=0)
        v = jnp.concatenate([vp_ref[0, :, g * LANES:(g + 1) * LANES],
                             vc_ref[0, :, g * LANES:(g + 1) * LANES]], axis=0)
        outs = []
        for hh in range(2):
            qh = jnp.where((lane // HEAD_DIM) == hh, qp, jnp.zeros_like(qp))
            sink = sink_ref[2 * p + hh]
            s = jnp.where(mask, _dot_t(qh, k), -jnp.inf)
            m = jnp.maximum(jnp.max(s, axis=1, keepdims=True), sink)
            e = jnp.exp(s - m)
            l = jnp.sum(e, axis=1, keepdims=True) + jnp.exp(sink - m)
            outs.append(_dot(e.astype(_BF16), v) / l)
        o_ref[0, :, p * LANES:(p + 1) * LANES] = jnp.where(lane < HEAD_DIM, outs[0], outs[1]).astype(o_ref.dtype)


def _swa_call(q, kdup, vdup, sinks):
    bsz, seq, width = q.shape
    w = SWA_WINDOW
    kvw = kdup.shape[2]
    prev = pl.BlockSpec((1, w, kvw), lambda b, i: (b, jnp.maximum(i - 1, 0), 0))
    cur = pl.BlockSpec((1, w, kvw), lambda b, i: (b, i, 0))
    return pl.pallas_call(
        _swa_body,
        out_shape=jax.ShapeDtypeStruct((bsz, seq, width), _BF16),
        grid=(bsz, seq // w),
        in_specs=[
            pl.BlockSpec(memory_space=pltpu.SMEM),
            pl.BlockSpec((1, w, width), lambda b, i: (b, i, 0)),
            prev, cur, prev, cur,
        ],
        out_specs=pl.BlockSpec((1, w, width), lambda b, i: (b, i, 0)),
        compiler_params=_cparams(("arbitrary", "arbitrary")),
        name="swa_sink",
    )(sinks, q, kdup, kdup, vdup, vdup)


def _outproj_body(o1_ref, o2_ref, g_ref, x_ref, mod_ref, w_ref, lng_ref, lnb_ref, y_ref):
    half = o1_ref.shape[2]
    a1 = o1_ref[0] * g_ref[0, :, :half]
    a2 = o2_ref[0] * g_ref[0, :, half:]
    y = _dot(a1, w_ref[:half, :]) + _dot(a2, w_ref[half:, :])
    gate = mod_ref[0, 2:3, :]
    z = DEEPNORM_ALPHA * x_ref[0] + (1.0 + gate) * y
    mu = jnp.mean(z, axis=1, keepdims=True)
    zc = z - mu
    var = jnp.mean(zc * zc, axis=1, keepdims=True)
    y_ref[0] = zc * lax.rsqrt(var + LN_EPS) * lng_ref[...] + lnb_ref[...]


def _outproj_call(o1, o2, g, x, mod, w_out, ln_g, ln_b):
    bsz, seq, d = x.shape
    tm = PROJ_TM
    half = o1.shape[2]
    row = lambda n: pl.BlockSpec((1, tm, n), lambda b, i: (b, i, 0))
    return pl.pallas_call(
        _outproj_body,
        out_shape=jax.ShapeDtypeStruct((bsz, seq, d), _F32),
        grid=(bsz, seq // tm),
        in_specs=[
            row(half), row(half), row(2 * half), row(d),
            pl.BlockSpec((1, 3, d), lambda b, i: (b, 0, 0)),
            pl.BlockSpec((2 * half, d), lambda b, i: (0, 0)),
            pl.BlockSpec((1, d), lambda b, i: (0, 0)),
            pl.BlockSpec((1, d), lambda b, i: (0, 0)),
        ],
        out_specs=row(d),
        compiler_params=_cparams(("arbitrary", "arbitrary")),
        name="outproj_deepnorm",
    )(o1, o2, g, x, mod, w_out.astype(_BF16), ln_g.reshape(1, d), ln_b.reshape(1, d))


ODD_COLS = 4096


def _prep_w_odd(w):
    offs = [0]
    for n in ODD_SIZES:
        offs.append(offs[-1] + n)
    cqn, cqr, ckv, ckr, iq, ik, iw, dq, dk, dv, gate = [w[:, offs[i]:offs[i + 1]] for i in range(len(ODD_SIZES))]
    iw_blk = jnp.concatenate([iw, jnp.zeros((w.shape[0], LANES - IDX_HEADS), w.dtype)], axis=1)
    return jnp.concatenate([cqn, cqr, ckv, jnp.tile(ckr, (1, 4)), iq, jnp.tile(ik, (1, 4)), iw_blk,
                            dq, dk, dv, gate], axis=1).astype(_BF16)


def _inproj_odd_body(x_ref, mod_ref, w_ref, cosr_ref, sinr_ref, cosi_ref, sini_ref, kvg_ref,
                     cqn_ref, cqr_ref, kvc_ref, ckvt_ref, iq_ref, ik_ref, iw_ref,
                     dqt_ref, dk_ref, dvt_ref, g_ref):
    h = _modulated(x_ref, mod_ref)
    cosr, sinr = cosr_ref[...], sinr_ref[...]
    cosi, sini = cosi_ref[...], sini_ref[...]

    acc = _dot(h, w_ref[:, 0:512])
    for p in range(4):
        cqn_ref[0, p * LANES:(p + 1) * LANES, :] = acc[:, p * LANES:(p + 1) * LANES].T.astype(_BF16)
    acc = _dot(h, w_ref[:, 512:1024])
    for p in range(2):
        piece = _rope_piece(acc[:, p * LANES:(p + 1) * LANES], cosr, sinr, C_ROPE_DIM)
        cqr_ref[0, p * LANES:(p + 1) * LANES, :] = piece.T.astype(_BF16)
    ckv = acc[:, 256:384]
    ckv = ckv * lax.rsqrt(jnp.mean(ckv * ckv, axis=1, keepdims=True) + LN_EPS) * kvg_ref[...]
    kvc_ref[0, :, 0:LANES] = ckv.astype(_BF16)
    ckvt_ref[0, 0] = ckv.T.astype(_BF16)
    kvc_ref[0, :, LANES:2 * LANES] = _rope_piece(acc[:, 384:512], cosr, sinr, C_ROPE_DIM).astype(_BF16)
    acc = _dot(h, w_ref[:, 1024:1536])
    for p in range(2):
        piece = _rope_piece(acc[:, p * LANES:(p + 1) * LANES], cosi, sini, IDX_DIM)
        iq_ref[0, p * LANES:(p + 1) * LANES, :] = piece.T.astype(_BF16)
    ik_ref[0] = _rope_piece(acc[:, 256:384], cosi, sini, IDX_DIM).astype(_BF16)
    iw_ref[0] = acc[:, 384:512].T[0:IDX_HEADS, :]
    acc = _dot(h, w_ref[:, 1536:2048]) * (HEAD_DIM ** -0.5)
    for p in range(4):
        dqt_ref[0, p * LANES:(p + 1) * LANES, :] = acc[:, p * LANES:(p + 1) * LANES].T.astype(_BF16)
    dk_ref[0] = _dot(h, w_ref[:, 2048:2560]).astype(_BF16)
    _store_blocks_t(_dot(h, w_ref[:, 2560:3072]), dvt_ref, SB_T)
    _silu_gate(h, w_ref, 3072, g_ref)


def _inproj_odd_call(x, mod, w, rope_r, rope_i, kv_g):
    bsz, seq, d = x.shape
    tm = DSA_CK
    row = lambda n: pl.BlockSpec((1, tm, n), lambda b, i: (b, i, 0))
    col = lambda n: pl.BlockSpec((1, n, tm), lambda b, i: (b, 0, i))
    tab = pl.BlockSpec((tm, LANES), lambda b, i: (i, 0))
    tok = lambda n, dt: jax.ShapeDtypeStruct((bsz, seq, n), dt)
    feat = lambda n, dt: jax.ShapeDtypeStruct((bsz, n, seq), dt)
    return pl.pallas_call(
        _inproj_odd_body,
        out_shape=[
            feat(512, _BF16), feat(256, _BF16), tok(256, _BF16),
            jax.ShapeDtypeStruct((bsz, seq // tm, C_KV_LATENT, tm), _BF16),
            feat(256, _BF16), tok(128, _BF16), feat(IDX_HEADS, _F32),
            feat(512, _BF16), tok(512, _BF16),
            jax.ShapeDtypeStruct((bsz, seq // SB_T, 512, SB_T), _BF16), tok(1024, _BF16),
        ],
        grid=(bsz, seq // tm),
        in_specs=[
            row(d),
            pl.BlockSpec((1, 3, d), lambda b, i: (b, 0, 0)),
            pl.BlockSpec((d, ODD_COLS), lambda b, i: (0, 0)),
            tab, tab, tab, tab,
            pl.BlockSpec((1, LANES), lambda b, i: (0, 0)),
        ],
        out_specs=[
            col(512), col(256), row(256),
            pl.BlockSpec((1, 1, C_KV_LATENT, tm), lambda b, i: (b, i, 0, 0)),
            col(256), row(128), col(IDX_HEADS),
            col(512), row(512),
            pl.BlockSpec((1, tm // SB_T, 512, SB_T), lambda b, i: (b, i, 0, 0)), row(1024),
        ],
        compiler_params=_cparams(("arbitrary", "arbitrary")),
        name="inproj_odd",
    )(x, mod, w, rope_r[0], rope_r[1], rope_i[0], rope_i[1], kv_g.reshape(1, LANES))


def _dsa_body(cqn_ref, cqr_ref, iq_ref, iw_ref, kvc_ref, ckvt_ref, ik_ref, wuk_ref, wuvt_ref, o_ref,
              qct_ref, iqt_ref, sc_ref, acc_ref):
    i = pl.program_id(1)
    tq, ck = DSA_TQ, DSA_CK
    seq = kvc_ref.shape[1]
    n_sel = float(min(DSA_TOPK, seq // 4))
    nch = i // (ck // tq) + 1
    att_scale = (C_NOPE_DIM + C_ROPE_DIM) ** -0.5
    idx_scale = (IDX_DIM * IDX_HEADS) ** -0.5
    hq = C_HEADS * tq

    row128 = _row_iota((LANES, tq))
    for h in range(C_HEADS):
        cols = slice(h * tq, (h + 1) * tq)
        qn_t = cqn_ref[0, h * C_NOPE_DIM:(h + 1) * C_NOPE_DIM, :]
        qlat_t = _dot(wuk_ref[h], qn_t) * att_scale
        qct_ref[0:C_KV_LATENT, cols] = qlat_t.astype(_BF16)
        qr_t = cqr_ref[0, h * C_ROPE_DIM:(h + 1) * C_ROPE_DIM, :].astype(_F32) * att_scale
        qct_ref[C_KV_LATENT:C_KV_LATENT + C_ROPE_DIM, cols] = qr_t.astype(_BF16)
        qct_ref[C_KV_LATENT + C_ROPE_DIM:, cols] = jnp.zeros((LANES - C_ROPE_DIM, tq), _BF16)
        blk4 = h // 4
        iq_t = iq_ref[0, blk4 * LANES:(blk4 + 1) * LANES, :]
        iqt_ref[:, cols] = jnp.where((row128 // IDX_DIM) == (h % 4), iq_t, jnp.zeros_like(iq_t))

    t_pos = i * tq + _lane_iota((ck, tq))
    key_row = _row_iota((ck, tq))

    def score_body(c, _):
        off = pl.multiple_of(c * ck, ck)
        rel = _dot(ik_ref[0, pl.ds(off, ck), :], iqt_ref[...])
        score = jnp.zeros((ck, tq), _F32)
        for h in range(IDX_HEADS):
            score = score + jnp.maximum(rel[:, h * tq:(h + 1) * tq], 0.0) * iw_ref[0, h:h + 1, :]
        score = score * idx_scale
        sc_ref[c] = jnp.where(off + key_row <= t_pos, score, -jnp.inf)
        return 0

    lax.fori_loop(0, nch, score_body, 0)

    def count(pred):
        rows_acc = 64

        def body(c, acc):
            hit = jnp.where(pred(sc_ref[c], c * ck + key_row), 1.0, 0.0)
            return acc + jnp.sum(hit.reshape(ck // rows_acc, rows_acc, tq), axis=0)

        acc = lax.fori_loop(0, nch, body, jnp.zeros((rows_acc, tq), _F32))
        return jnp.sum(acc, axis=0, keepdims=True)

    def ordered_float(u):
        key = u ^ INT_MIN
        return pltpu.bitcast(key ^ ((key >> 31) & 0x7FFFFFFF), _F32)

    def bit_body(b, carry):
        t, cnt_t = carry
        cand = t | jnp.left_shift(jnp.int32(1), 31 - b)
        cand_f = ordered_float(cand)
        cnt = count(lambda sc, pos: sc >= cand_f)
        ok = cnt >= n_sel
        return jnp.where(ok, cand, t), jnp.where(ok, cnt, cnt_t)

    t0 = jnp.zeros((1, tq), jnp.int32)
    t, cnt_t = lax.fori_loop(0, 32, bit_body, (t0, jnp.full((1, tq), float(seq), _F32)))
    thr = jnp.where(t == 0, -jnp.inf, ordered_float(t))

    excess = jnp.max(jnp.where((cnt_t > n_sel) & (t != 0), 1.0, 0.0)) > 0.0

    def tie_limit():
        room = n_sel - count(lambda sc, pos: sc > thr)

        def lim_body(b, lim):
            cand = lim | jnp.left_shift(jnp.int32(1), 12 - b)
            cnt = count(lambda sc, pos: (sc == thr) & (pos < cand))
            return jnp.where(cnt <= room, cand, lim)

        return lax.fori_loop(0, 13, lim_body, jnp.zeros((1, tq), jnp.int32))

    limit = lax.cond(excess, tie_limit, lambda: jnp.full((1, tq), 2 * seq, jnp.int32))

    def bias_body(c, _):
        sc = sc_ref[c]
        pos = c * ck + key_row
        chosen = ((sc > thr) | ((sc == thr) & (pos < limit))) & (pos <= t_pos)
        sc_ref[c] = jnp.where(chosen, 0.0, NEG_BIG)
        return 0

    lax.fori_loop(0, nch, bias_body, 0)

    acc_ref[...] = jnp.zeros_like(acc_ref)
    grp = 2 * tq
    gcols = [slice(g * grp, (g + 1) * grp) for g in range(hq // grp)]

    def att_body(c, carry):
        off = pl.multiple_of(c * ck, ck)
        kv = kvc_ref[0, pl.ds(off, ck), :]
        bias = sc_ref[c]
        bias2 = jnp.concatenate([bias, bias], axis=1)
        ss = [_dot(kv, qct_ref[:, cols]) + bias2 for cols in gcols]
        ms, ls, es, alphas = [], [], [], []
        for g, cols in enumerate(gcols):
            m, l = carry[0][:, cols], carry[1][:, cols]
            m_new = jnp.maximum(m, jnp.max(ss[g], axis=0, keepdims=True))
            alpha = jnp.exp(m - m_new)
            e = jnp.exp(ss[g] - m_new)
            ls.append(alpha * l + jnp.sum(e, axis=0, keepdims=True))
            ms.append(m_new)
            alphas.append(alpha)
            es.append(e.astype(_BF16))
        kv_t = ckvt_ref[0, c]
        for g, cols in enumerate(gcols):
            acc_ref[:, cols] = alphas[g] * acc_ref[:, cols] + _dot(kv_t, es[g])
        return jnp.concatenate(ms, axis=1), jnp.concatenate(ls, axis=1)

    m0 = jnp.full((1, hq), NEG_BIG, _F32)
    l0 = jnp.zeros((1, hq), _F32)
    _, l = lax.fori_loop(0, nch, att_body, (m0, l0))
    o_lat_t = (acc_ref[...] / l).astype(_BF16)
    outs = [_dot(wuvt_ref[h], o_lat_t[:, h * tq:(h + 1) * tq]) for h in range(C_HEADS)]
    o_ref[0] = jnp.concatenate(outs, axis=0).T.astype(o_ref.dtype)


def _dsa_call(cqn_t, cqr_t, iq_t, iw_t, kvc, ckv_t, ik, w_uk, wuv_t):
    bsz, seq, _ = kvc.shape
    tq = DSA_TQ
    qcol = lambda n: pl.BlockSpec((1, n, tq), lambda b, i: (b, 0, i))
    full = lambda n: pl.BlockSpec((1, seq, n), lambda b, i: (b, 0, 0))
    whole = lambda a: pl.BlockSpec(a.shape, lambda b, i: (0,) * a.ndim)
    width = C_HEADS * C_V_DIM
    return pl.pallas_call(
        _dsa_body,
        out_shape=jax.ShapeDtypeStruct((bsz, seq, width), _BF16),
        grid=(bsz, seq // tq),
        in_specs=[
            qcol(512), qcol(256), qcol(256), qcol(IDX_HEADS), full(256),
            pl.BlockSpec((1,) + ckv_t.shape[1:], lambda b, i: (b, 0, 0, 0)),
            full(128), whole(w_uk), whole(wuv_t),
        ],
        out_specs=pl.BlockSpec((1, tq, width), lambda b, i: (b, i, 0)),
        scratch_shapes=[
            pltpu.VMEM((2 * LANES, C_HEADS * tq), _BF16),
            pltpu.VMEM((LANES, IDX_HEADS * tq), _BF16),
            pltpu.VMEM((seq // DSA_CK, DSA_CK, tq), _F32),
            pltpu.VMEM((C_KV_LATENT, C_HEADS * tq), _F32),
        ],
        compiler_params=_cparams(("arbitrary", "arbitrary")),
        name="dsa",
    )(cqn_t, cqr_t, iq_t, iw_t, kvc, ckv_t, ik, w_uk, wuv_t)


def _sb_body(qt_ref, k_ref, vt_ref, o_ref, acc_ref):
    i = pl.program_id(1)
    t = SB_T
    npair = D_HEADS // 2
    width = 2 * t
    qcats = [_head_pair_rhs(qt_ref[0, p * LANES:(p + 1) * LANES, :], HEAD_DIM) for p in range(npair)]
    key_row = _row_iota((t, width))
    q_lane = _lane_iota((t, width)) % t
    later = jnp.where(_lane_iota((t, t)) > _row_iota((t, t)), 1.0, 0.0).astype(_BF16)
    acc_ref[...] = jnp.zeros_like(acc_ref)

    def cond(carry):
        j, _, top = carry
        return (j >= 0) & (top > SB_UNDERFLOW)

    def body(carry):
        j, rest, _ = carry
        off = pl.multiple_of(j * t, t)
        kj = k_ref[0, pl.ds(off, t), :]
        past = (off + key_row) < (i * t + q_lane)
        zs = [_dot(kj[:, p * LANES:(p + 1) * LANES], qcats[p]) for p in range(npair)]
        rests = []
        for p in range(npair):
            z = zs[p]
            log_beta = jnp.minimum(z, 0.0) - jnp.log1p(jnp.exp(-jnp.abs(z)))
            log_keep = jnp.where(past, log_beta - z, 0.0)
            hi = log_keep.astype(_BF16)
            lo = (log_keep - hi.astype(_F32)).astype(_BF16)
            rest_p = rest[:, p * width:(p + 1) * width]
            log_after = _dot(later, hi) + _dot(later, lo) + rest_p
            a = jnp.where(past, jnp.exp(log_beta + log_after), 0.0).astype(_BF16)
            for hh in range(2):
                rows = slice((2 * p + hh) * HEAD_DIM, (2 * p + hh + 1) * HEAD_DIM)
                acc_ref[rows, :] = acc_ref[rows, :] + _dot(vt_ref[0, j, rows, :], a[:, hh * t:(hh + 1) * t])
            rests.append(rest_p + jnp.sum(log_keep, axis=0, keepdims=True))
        rest = jnp.concatenate(rests, axis=1)
        return j - 1, rest, jnp.max(rest)

    init = (i, jnp.zeros((1, npair * width), _F32), jnp.float32(0.0))
    lax.while_loop(cond, body, init)
    o_ref[0] = acc_ref[...].T.astype(o_ref.dtype)


def _sb_call(q_t, k, v_t):
    bsz, seq, width = k.shape
    t = SB_T
    return pl.pallas_call(
        _sb_body,
        out_shape=jax.ShapeDtypeStruct((bsz, seq, width), _BF16),
        grid=(bsz, seq // t),
        in_specs=[
            pl.BlockSpec((1, width, t), lambda b, i: (b, 0, i)),
            pl.BlockSpec((1, seq, width), lambda b, i: (b, 0, 0)),
            pl.BlockSpec((1, seq // t, width, t), lambda b, i: (b, 0, 0, 0)),
        ],
        out_specs=pl.BlockSpec((1, t, width), lambda b, i: (b, i, 0)),
        scratch_shapes=[pltpu.VMEM((width, t), _F32)],
        compiler_params=_cparams(("arbitrary", "arbitrary")),
        name="stick_breaking",
    )(q_t, k, v_t)


def _even_layer(x, mod, w_in, sinks, w_out, ln_g, ln_b, rope_h):
    aq_t, ak, bq, bk, av_t, bv, g = _inproj_even_call(x, mod, _prep_w_even(w_in), *rope_h)
    oa = _moba_call(aq_t, ak, av_t)
    ob = _swa_call(bq, bk, bv, sinks)
    return _outproj_call(oa, ob, g, x, mod, w_out, ln_g, ln_b)


def _odd_layer(x, mod, w_in, kv_g, w_uk, w_uv, w_out, ln_g, ln_b, rope_r, rope_i):
    cqn_t, cqr_t, kvc, ckv_t, iq_t, ik, iw_t, dq_t, dk, dv_t, g = _inproj_odd_call(
        x, mod, _prep_w_odd(w_in), rope_r, rope_i, kv_g)
    oc = _dsa_call(cqn_t, cqr_t, iq_t, iw_t, kvc, ckv_t, ik,
                   w_uk.astype(_BF16), w_uv.transpose(0, 2, 1).astype(_BF16))
    od = _sb_call(dq_t, dk, dv_t)
    return _outproj_call(oc, od, g, x, mod, w_out, ln_g, ln_b)


def kernel(x, c, w_ada, b_ada, w_in_even, sink_logits, w_in_odd, kv_norm_g, w_uk, w_uv, w_out, ln_g, ln_b):
    bsz, seq, d = x.shape
    rope_h = _rope_tables(seq, HEAD_DIM, LANES)
    rope_r = _rope_tables(seq, C_ROPE_DIM, LANES)
    rope_i = _rope_tables(seq, IDX_DIM, LANES)
    mods = _ada_call(c, w_ada, b_ada).reshape(DEPTH, bsz, 3, d)
    for layer in range(DEPTH):
        mod = mods[layer]
        j = layer // 2
        if layer % 2 == 0:
            x = _even_layer(x, mod, w_in_even[j], sink_logits[j], w_out[layer], ln_g[layer], ln_b[layer], rope_h)
        else:
            x = _odd_layer(x, mod, w_in_odd[j], kv_norm_g[j], w_uk[j], w_uv[j], w_out[layer],
                           ln_g[layer], ln_b[layer], rope_r, rope_i)
    return x
```

```python
import jax
import jax.numpy as jnp
from jax import lax
from jax.experimental import pallas as pl
from jax.experimental.pallas import tpu as pltpu

D_MODEL = 1024
DEPTH = 2
HEAD_DIM = 64
ROPE_THETA = 10000.0
LN_EPS = 1e-5
A_HEADS = 8
MOBA_BLOCK = 256
MOBA_TOPK = 3
B_HEADS = 8
B_KV_HEADS = 2
SWA_WINDOW = 128
C_HEADS = 8
C_NOPE_DIM = 64
C_ROPE_DIM = 32
C_V_DIM = 64
C_KV_LATENT = 128
IDX_HEADS = 8
IDX_DIM = 32
DSA_TOPK = 256
D_HEADS = 8
MIX_WIDTH = 1024
EVEN_SIZES = (512, 512, 512, 512, 128, 128, 1024)
ODD_SIZES = (512, 256, 128, 32, 256, 32, 8, 512, 512, 512, 1024)
DEEPNORM_ALPHA = (2 * DEPTH) ** 0.25

LANES = 128
NEG_BIG = -1e30
INT_MIN = -2 ** 31
SB_UNDERFLOW = -104.0

PROJ_TM = 512
DSA_TQ = 128
DSA_CK = 512
SB_T = 128
MOBA_PAIRS = 4
VMEM_LIMIT = 48 * 1024 * 1024

_BF16 = jnp.bfloat16
_F32 = jnp.float32


def _cparams(sem):
    return pltpu.CompilerParams(dimension_semantics=sem, vmem_limit_bytes=VMEM_LIMIT)


def _dot_t(a, b):
    return lax.dot_general(a, b, (((1,), (1,)), ((), ())), preferred_element_type=_F32)


def _dot(a, b):
    return jnp.dot(a, b, preferred_element_type=_F32)


def _lane_iota(shape):
    return lax.broadcasted_iota(jnp.int32, shape, len(shape) - 1)


def _row_iota(shape):
    return lax.broadcasted_iota(jnp.int32, shape, len(shape) - 2)


def _head_pair_rhs(q_t, head_dim):
    frow = _row_iota(q_t.shape)
    return jnp.concatenate(
        [jnp.where((frow // head_dim) == hh, q_t, jnp.zeros_like(q_t)) for hh in range(2)], axis=1)


def _ada_body(c_ref, w_ref, b_ref, o_ref):
    c = c_ref[...]
    cond = c * jax.nn.sigmoid(c)
    o_ref[0] = _dot(cond, w_ref[0]) + b_ref[0]


def _ada_call(c, w_ada, b_ada):
    depth, d, n3 = w_ada.shape
    bsz = c.shape[0]
    nb = n3 // d
    return pl.pallas_call(
        _ada_body,
        out_shape=jax.ShapeDtypeStruct((depth, bsz, n3), _F32),
        grid=(depth, nb),
        in_specs=[
            pl.BlockSpec((bsz, d), lambda l, j: (0, 0)),
            pl.BlockSpec((1, d, d), lambda l, j: (l, 0, j)),
            pl.BlockSpec((1, 1, d), lambda l, j: (l, 0, j)),
        ],
        out_specs=pl.BlockSpec((1, bsz, d), lambda l, j: (l, 0, j)),
        compiler_params=_cparams(("arbitrary", "arbitrary")),
        name="ada_mod",
    )(c, w_ada, b_ada.reshape(depth, 1, n3))


def _rope_tables(seq, dim, period_lanes):
    inv = 1.0 / (ROPE_THETA ** (jnp.arange(0, dim, 2, dtype=_F32) / dim))
    ang = jnp.arange(seq, dtype=_F32)[:, None] * inv[None, :]
    cos, sin = jnp.cos(ang), jnp.sin(ang)
    cos_h = jnp.concatenate([cos, cos], axis=1)
    sin_h = jnp.concatenate([-sin, sin], axis=1)
    reps = period_lanes // dim
    return jnp.tile(cos_h, (1, reps)), jnp.tile(sin_h, (1, reps))


def _rope_piece(x, cos, sin_signed, dim):
    half = dim // 2
    first = (_lane_iota(x.shape) % dim) < half
    partner = jnp.where(first, pltpu.roll(x, LANES - half, 1), pltpu.roll(x, half, 1))
    return x * cos + partner * sin_signed


def _modulated(x_ref, mod_ref):
    x = x_ref[0]
    shift = mod_ref[0, 0:1, :]
    scale = mod_ref[0, 1:2, :]
    return (x * (1.0 + scale) + shift).astype(_BF16)


def _silu_gate(h, w_ref, col0, g_ref):
    for p in range(2):
        gate = _dot(h, w_ref[:, col0 + 512 * p:col0 + 512 * (p + 1)])
        g_ref[0, :, 512 * p:512 * (p + 1)] = (gate * jax.nn.sigmoid(gate)).astype(_BF16)


def _store_blocks_t(acc, out_ref, blk):
    tm, n = acc.shape
    for r in range(tm // blk):
        for p in range(n // LANES):
            out_ref[0, r, p * LANES:(p + 1) * LANES, :] = (
                acc[r * blk:(r + 1) * blk, p * LANES:(p + 1) * LANES].T.astype(_BF16))


EVEN_COLS = 3584


def _prep_w_even(w):
    offs = [0]
    for n in EVEN_SIZES:
        offs.append(offs[-1] + n)
    aq, ak, av, bq, bk, bv, gate = [w[:, offs[i]:offs[i + 1]] for i in range(len(EVEN_SIZES))]

    def dup(t):
        parts = []
        for g in range(B_KV_HEADS):
            blk = t[:, g * HEAD_DIM:(g + 1) * HEAD_DIM]
            parts += [blk, blk]
        return jnp.concatenate(parts, axis=1)

    return jnp.concatenate([aq, ak, bq, dup(bk), av, dup(bv), gate], axis=1).astype(_BF16)


def _inproj_even_body(x_ref, mod_ref, w_ref, cos_ref, sin_ref,
                      aqt_ref, ak_ref, bq_ref, bk_ref, avt_ref, bv_ref, g_ref):
    h = _modulated(x_ref, mod_ref)
    cos = cos_ref[...]
    sin = sin_ref[...]
    q_scale = HEAD_DIM ** -0.5

    def roped(col0, ncols, out_ref, scale, transposed=False):
        acc = _dot(h, w_ref[:, col0:col0 + ncols])
        for p in range(ncols // LANES):
            piece = _rope_piece(acc[:, p * LANES:(p + 1) * LANES], cos, sin, HEAD_DIM)
            if scale != 1.0:
                piece = piece * scale
            if transposed:
                out_ref[0, p * LANES:(p + 1) * LANES, :] = piece.T.astype(_BF16)
            else:
                out_ref[0, :, p * LANES:(p + 1) * LANES] = piece.astype(_BF16)

    roped(0, 512, aqt_ref, q_scale, transposed=True)
    roped(512, 512, ak_ref, 1.0)
    roped(1024, 512, bq_ref, q_scale)
    roped(1536, 256, bk_ref, 1.0)
    _store_blocks_t(_dot(h, w_ref[:, 1792:2304]), avt_ref, MOBA_BLOCK)
    bv_ref[0] = _dot(h, w_ref[:, 2304:2560]).astype(_BF16)
    _silu_gate(h, w_ref, 2560, g_ref)


def _inproj_even_call(x, mod, w, cos, sin):
    bsz, seq, d = x.shape
    tm = PROJ_TM
    row = lambda n: pl.BlockSpec((1, tm, n), lambda b, i: (b, i, 0))
    tok = lambda n: jax.ShapeDtypeStruct((bsz, seq, n), _BF16)
    per_tile = tm // MOBA_BLOCK
    return pl.pallas_call(
        _inproj_even_body,
        out_shape=[
            jax.ShapeDtypeStruct((bsz, 512, seq), _BF16), tok(512), tok(512), tok(256),
            jax.ShapeDtypeStruct((bsz, seq // MOBA_BLOCK, 512, MOBA_BLOCK), _BF16), tok(256), tok(1024),
        ],
        grid=(bsz, seq // tm),
        in_specs=[
            row(d),
            pl.BlockSpec((1, 3, d), lambda b, i: (b, 0, 0)),
            pl.BlockSpec((d, EVEN_COLS), lambda b, i: (0, 0)),
            pl.BlockSpec((tm, LANES), lambda b, i: (i, 0)),
            pl.BlockSpec((tm, LANES), lambda b, i: (i, 0)),
        ],
        out_specs=[
            pl.BlockSpec((1, 512, tm), lambda b, i: (b, 0, i)), row(512), row(512), row(256),
            pl.BlockSpec((1, per_tile, 512, MOBA_BLOCK), lambda b, i: (b, i, 0, 0)), row(256), row(1024),
        ],
        compiler_params=_cparams(("arbitrary", "arbitrary")),
        name="inproj_even",
    )(x, mod, w, cos, sin)


def _moba_body(qt_ref, k_ref, vt_ref, o_ref, km_ref, selb_ref, acc_ref):
    qi = pl.program_id(2)
    tq = MOBA_BLOCK
    nblk = km_ref.shape[0]
    npair = qt_ref.shape[1] // LANES
    pw = 2 * tq
    width = npair * pw

    @pl.when(qi == 0)
    def _():
        for r in range(nblk):
            blk_k = k_ref[0, r * tq:(r + 1) * tq, :].astype(_F32)
            km_ref[r:r + 1, :] = jnp.mean(blk_k, axis=0, keepdims=True)

    qcats = [_head_pair_rhs(qt_ref[0, p * LANES:(p + 1) * LANES, :], HEAD_DIM) for p in range(npair)]

    blk = _row_iota((nblk, width))
    blkf = blk.astype(_F32)
    valid = blk < qi
    gate = jnp.concatenate([_dot(km_ref[:, p * LANES:(p + 1) * LANES].astype(_BF16), qcats[p])
                            for p in range(npair)], axis=1)
    gate = jnp.where(valid, gate, -jnp.inf)
    sel = jnp.zeros((nblk, width), _F32)
    for _ in range(MOBA_TOPK):
        top = jnp.max(gate, axis=0, keepdims=True)
        first = jnp.min(jnp.where(gate == top, blkf, float(nblk)), axis=0, keepdims=True)
        pick = blkf == first
        sel = jnp.where(pick, 1.0, sel)
        gate = jnp.where(pick, -jnp.inf, gate)
    selb_ref[...] = jnp.where(valid & (sel > 0.0), 0.0, NEG_BIG)

    acc_ref[...] = jnp.zeros_like(acc_ref)

    def step(carry, j, bias_of_pair):
        m, l = carry
        kj = k_ref[0, pl.ds(pl.multiple_of(j * tq, tq), tq), :]
        ss = [_dot(kj[:, p * LANES:(p + 1) * LANES], qcats[p]) + bias_of_pair(p) for p in range(npair)]
        ms, ls = [], []
        for p in range(npair):
            pcols = slice(p * pw, (p + 1) * pw)
            m_new = jnp.maximum(m[:, pcols], jnp.max(ss[p], axis=0, keepdims=True))
            alpha = jnp.exp(m[:, pcols] - m_new)
            e = jnp.exp(ss[p] - m_new)
            ls.append(alpha * l[:, pcols] + jnp.sum(e, axis=0, keepdims=True))
            ms.append(m_new)
            e = e.astype(_BF16)
            for hh in range(2):
                rows = slice((2 * p + hh) * HEAD_DIM, (2 * p + hh + 1) * HEAD_DIM)
                cols = slice(hh * tq, (hh + 1) * tq)
                acc_ref[rows, :] = alpha[:, cols] * acc_ref[rows, :] + _dot(vt_ref[0, j, rows, :], e[:, cols])
        return jnp.concatenate(ms, axis=1), jnp.concatenate(ls, axis=1)

    def past_body(j, carry):
        row_bias = selb_ref[pl.ds(j, 1), :]
        return step(carry, j, lambda p: row_bias[:, p * pw:(p + 1) * pw])

    carry = (jnp.full((1, width), NEG_BIG, _F32), jnp.zeros((1, width), _F32))
    carry = lax.fori_loop(0, qi, past_body, carry)
    causal = jnp.where(_row_iota((tq, pw)) <= _lane_iota((tq, pw)) % tq, 0.0, NEG_BIG)
    _, l = step(carry, qi, lambda p: causal)
    inv = 1.0 / l
    out_t = jnp.concatenate([acc_ref[h * HEAD_DIM:(h + 1) * HEAD_DIM, :] * inv[:, h * tq:(h + 1) * tq]
                             for h in range(2 * npair)], axis=0)
    o_ref[0] = out_t.T.astype(o_ref.dtype)


def _moba_call(q_t, k, v_t):
    bsz, seq, width = k.shape
    tq = MOBA_BLOCK
    nblk = seq // tq
    gw = MOBA_PAIRS * LANES
    return pl.pallas_call(
        _moba_body,
        out_shape=jax.ShapeDtypeStruct((bsz, seq, width), _BF16),
        grid=(bsz, width // gw, nblk),
        in_specs=[
            pl.BlockSpec((1, gw, tq), lambda b, h, i: (b, h, i)),
            pl.BlockSpec((1, seq, gw), lambda b, h, i: (b, 0, h)),
            pl.BlockSpec((1, nblk, gw, tq), lambda b, h, i: (b, 0, h, 0)),
        ],
        out_specs=pl.BlockSpec((1, tq, gw), lambda b, h, i: (b, i, h)),
        scratch_shapes=[
            pltpu.VMEM((nblk, gw), _F32),
            pltpu.VMEM((nblk, MOBA_PAIRS * 2 * tq), _F32),
            pltpu.VMEM((gw, tq), _F32),
        ],
        compiler_params=_cparams(("arbitrary", "arbitrary", "arbitrary")),
        name="moba",
    )(q_t, k, v_t)


def _swa_body(sink_ref, q_ref, kp_ref, kc_ref, vp_ref, vc_ref, o_ref):
    i = pl.program_id(1)
    w = SWA_WINDOW
    lane = _lane_iota((1, LANES))
    row = _row_iota((w, 2 * w))
    col = _lane_iota((w, 2 * w))
    mask = (col > row) & (col <= row + w) & ((col >= w) | (i > 0))
    pairs_per_kv = (B_HEADS // B_KV_HEADS) // 2
    for p in range(B_HEADS // 2):
        g = p // pairs_per_kv
        qp = q_ref[0, :, p * LANES:(p + 1) * LANES]
        k = jnp.concatenate([kp_ref[0, :, g * LANES:(g + 1) * LANES],
                             kc_ref[0, :, g * LANES:(g + 1) * LANES]], axis=0)
        v = jnp.concatenate([vp_ref[0, :, g * LANES:(g + 1) * LANES],
                             vc_ref[0, :, g * LANES:(g + 1) * LANES]], axis=0)
        outs = []
        for hh in range(2):
            qh = jnp.where((lane // HEAD_DIM) == hh, qp, jnp.zeros_like(qp))
            sink = sink_ref[2 * p + hh]
            s = jnp.where(mask, _dot_t(qh, k), -jnp.inf)
            m = jnp.maximum(jnp.max(s, axis=1, keepdims=True), sink)
            e = jnp.exp(s - m)
            l = jnp.sum(e, axis=1, keepdims=True) + jnp.exp(sink - m)
            outs.append(_dot(e.astype(_BF16), v) / l)
        o_ref[0, :, p * LANES:(p + 1) * LANES] = jnp.where(lane < HEAD_DIM, outs[0], outs[1]).astype(o_ref.dtype)


def _swa_call(q, kdup, vdup, sinks):
    bsz, seq, width = q.shape
    w = SWA_WINDOW
    kvw = kdup.shape[2]
    prev = pl.BlockSpec((1, w, kvw), lambda b, i: (b, jnp.maximum(i - 1, 0), 0))
    cur = pl.BlockSpec((1, w, kvw), lambda b, i: (b, i, 0))
    return pl.pallas_call(
        _swa_body,
        out_shape=jax.ShapeDtypeStruct((bsz, seq, width), _BF16),
        grid=(bsz, seq // w),
        in_specs=[
            pl.BlockSpec(memory_space=pltpu.SMEM),
            pl.BlockSpec((1, w, width), lambda b, i: (b, i, 0)),
            prev, cur, prev, cur,
        ],
        out_specs=pl.BlockSpec((1, w, width), lambda b, i: (b, i, 0)),
        compiler_params=_cparams(("arbitrary", "arbitrary")),
        name="swa_sink",
    )(sinks, q, kdup, kdup, vdup, vdup)


def _outproj_body(o1_ref, o2_ref, g_ref, x_ref, mod_ref, w_ref, lng_ref, lnb_ref, y_ref):
    half = o1_ref.shape[2]
    a1 = o1_ref[0] * g_ref[0, :, :half]
    a2 = o2_ref[0] * g_ref[0, :, half:]
    y = _dot(a1, w_ref[:half, :]) + _dot(a2, w_ref[half:, :])
    gate = mod_ref[0, 2:3, :]
    z = DEEPNORM_ALPHA * x_ref[0] + (1.0 + gate) * y
    mu = jnp.mean(z, axis=1, keepdims=True)
    zc = z - mu
    var = jnp.mean(zc * zc, axis=1, keepdims=True)
    y_ref[0] = zc * lax.rsqrt(var + LN_EPS) * lng_ref[...] + lnb_ref[...]


def _outproj_call(o1, o2, g, x, mod, w_out, ln_g, ln_b):
    bsz, seq, d = x.shape
    tm = PROJ_TM
    half = o1.shape[2]
    row = lambda n: pl.BlockSpec((1, tm, n), lambda b, i: (b, i, 0))
    return pl.pallas_call(
        _outproj_body,
        out_shape=jax.ShapeDtypeStruct((bsz, seq, d), _F32),
        grid=(bsz, seq // tm),
        in_specs=[
            row(half), row(half), row(2 * half), row(d),
            pl.BlockSpec((1, 3, d), lambda b, i: (b, 0, 0)),
            pl.BlockSpec((2 * half, d), lambda b, i: (0, 0)),
            pl.BlockSpec((1, d), lambda b, i: (0, 0)),
            pl.BlockSpec((1, d), lambda b, i: (0, 0)),
        ],
        out_specs=row(d),
        compiler_params=_cparams(("arbitrary", "arbitrary")),
        name="outproj_deepnorm",
    )(o1, o2, g, x, mod, w_out.astype(_BF16), ln_g.reshape(1, d), ln_b.reshape(1, d))


ODD_COLS = 4096


def _prep_w_odd(w):
    offs = [0]
    for n in ODD_SIZES:
        offs.append(offs[-1] + n)
    cqn, cqr, ckv, ckr, iq, ik, iw, dq, dk, dv, gate = [w[:, offs[i]:offs[i + 1]] for i in range(len(ODD_SIZES))]
    iw_blk = jnp.concatenate([iw, jnp.zeros((w.shape[0], LANES - IDX_HEADS), w.dtype)], axis=1)
    return jnp.concatenate([cqn, cqr, ckv, jnp.tile(ckr, (1, 4)), iq, jnp.tile(ik, (1, 4)), iw_blk,
                            dq, dk, dv, gate], axis=1).astype(_BF16)


def _inproj_odd_body(x_ref, mod_ref, w_ref, cosr_ref, sinr_ref, cosi_ref, sini_ref, kvg_ref,
                     cqn_ref, cqr_ref, kvc_ref, ckvt_ref, iq_ref, ik_ref, iw_ref,
                     dqt_ref, dk_ref, dvt_ref, g_ref):
    h = _modulated(x_ref, mod_ref)
    cosr, sinr = cosr_ref[...], sinr_ref[...]
    cosi, sini = cosi_ref[...], sini_ref[...]

    acc = _dot(h, w_ref[:, 0:512])
    for p in range(4):
        cqn_ref[0, p * LANES:(p + 1) * LANES, :] = acc[:, p * LANES:(p + 1) * LANES].T.astype(_BF16)
    acc = _dot(h, w_ref[:, 512:1024])
    for p in range(2):
        piece = _rope_piece(acc[:, p * LANES:(p + 1) * LANES], cosr, sinr, C_ROPE_DIM)
        cqr_ref[0, p * LANES:(p + 1) * LANES, :] = piece.T.astype(_BF16)
    ckv = acc[:, 256:384]
    ckv = ckv * lax.rsqrt(jnp.mean(ckv * ckv, axis=1, keepdims=True) + LN_EPS) * kvg_ref[...]
    kvc_ref[0, :, 0:LANES] = ckv.astype(_BF16)
    ckvt_ref[0, 0] = ckv.T.astype(_BF16)
    kvc_ref[0, :, LANES:2 * LANES] = _rope_piece(acc[:, 384:512], cosr, sinr, C_ROPE_DIM).astype(_BF16)
    acc = _dot(h, w_ref[:, 1024:1536])
    for p in range(2):
        piece = _rope_piece(acc[:, p * LANES:(p + 1) * LANES], cosi, sini, IDX_DIM)
        iq_ref[0, p * LANES:(p + 1) * LANES, :] = piece.T.astype(_BF16)
    ik_ref[0] = _rope_piece(acc[:, 256:384], cosi, sini, IDX_DIM).astype(_BF16)
    iw_ref[0] = acc[:, 384:512].T[0:IDX_HEADS, :]
    acc = _dot(h, w_ref[:, 1536:2048]) * (HEAD_DIM ** -0.5)
    for p in range(4):
        dqt_ref[0, p * LANES:(p + 1) * LANES, :] = acc[:, p * LANES:(p + 1) * LANES].T.astype(_BF16)
    dk_ref[0] = _dot(h, w_ref[:, 2048:2560]).astype(_BF16)
    _store_blocks_t(_dot(h, w_ref[:, 2560:3072]), dvt_ref, SB_T)
    _silu_gate(h, w_ref, 3072, g_ref)


def _inproj_odd_call(x, mod, w, rope_r, rope_i, kv_g):
    bsz, seq, d = x.shape
    tm = DSA_CK
    row = lambda n: pl.BlockSpec((1, tm, n), lambda b, i: (b, i, 0))
    col = lambda n: pl.BlockSpec((1, n, tm), lambda b, i: (b, 0, i))
    tab = pl.BlockSpec((tm, LANES), lambda b, i: (i, 0))
    tok = lambda n, dt: jax.ShapeDtypeStruct((bsz, seq, n), dt)
    feat = lambda n, dt: jax.ShapeDtypeStruct((bsz, n, seq), dt)
    return pl.pallas_call(
        _inproj_odd_body,
        out_shape=[
            feat(512, _BF16), feat(256, _BF16), tok(256, _BF16),
            jax.ShapeDtypeStruct((bsz, seq // tm, C_KV_LATENT, tm), _BF16),
            feat(256, _BF16), tok(128, _BF16), feat(IDX_HEADS, _F32),
            feat(512, _BF16), tok(512, _BF16),
            jax.ShapeDtypeStruct((bsz, seq // SB_T, 512, SB_T), _BF16), tok(1024, _BF16),
        ],
        grid=(bsz, seq // tm),
        in_specs=[
            row(d),
            pl.BlockSpec((1, 3, d), lambda b, i: (b, 0, 0)),
            pl.BlockSpec((d, ODD_COLS), lambda b, i: (0, 0)),
            tab, tab, tab, tab,
            pl.BlockSpec((1, LANES), lambda b, i: (0, 0)),
        ],
        out_specs=[
            col(512), col(256), row(256),
            pl.BlockSpec((1, 1, C_KV_LATENT, tm), lambda b, i: (b, i, 0, 0)),
            col(256), row(128), col(IDX_HEADS),
            col(512), row(512),
            pl.BlockSpec((1, tm // SB_T, 512, SB_T), lambda b, i: (b, i, 0, 0)), row(1024),
        ],
        compiler_params=_cparams(("arbitrary", "arbitrary")),
        name="inproj_odd",
    )(x, mod, w, rope_r[0], rope_r[1], rope_i[0], rope_i[1], kv_g.reshape(1, LANES))


def _dsa_body(cqn_ref, cqr_ref, iq_ref, iw_ref, kvc_ref, ckvt_ref, ik_ref, wuk_ref, wuvt_ref, o_ref,
              qct_ref, iqt_ref, sc_ref, acc_ref):
    i = pl.program_id(1)
    tq, ck = DSA_TQ, DSA_CK
    seq = kvc_ref.shape[1]
    n_sel = float(min(DSA_TOPK, seq // 4))
    nch = i // (ck // tq) + 1
    att_scale = (C_NOPE_DIM + C_ROPE_DIM) ** -0.5
    idx_scale = (IDX_DIM * IDX_HEADS) ** -0.5
    hq = C_HEADS * tq

    row128 = _row_iota((LANES, tq))
    for h in range(C_HEADS):
        cols = slice(h * tq, (h + 1) * tq)
        qn_t = cqn_ref[0, h * C_NOPE_DIM:(h + 1) * C_NOPE_DIM, :]
        qlat_t = _dot(wuk_ref[h], qn_t) * att_scale
        qct_ref[0:C_KV_LATENT, cols] = qlat_t.astype(_BF16)
        qr_t = cqr_ref[0, h * C_ROPE_DIM:(h + 1) * C_ROPE_DIM, :].astype(_F32) * att_scale
        qct_ref[C_KV_LATENT:C_KV_LATENT + C_ROPE_DIM, cols] = qr_t.astype(_BF16)
        qct_ref[C_KV_LATENT + C_ROPE_DIM:, cols] = jnp.zeros((LANES - C_ROPE_DIM, tq), _BF16)
        blk4 = h // 4
        iq_t = iq_ref[0, blk4 * LANES:(blk4 + 1) * LANES, :]
        iqt_ref[:, cols] = jnp.where((row128 // IDX_DIM) == (h % 4), iq_t, jnp.zeros_like(iq_t))

    t_pos = i * tq + _lane_iota((ck, tq))
    key_row = _row_iota((ck, tq))

    def score_body(c, _):
        off = pl.multiple_of(c * ck, ck)
        rel = _dot(ik_ref[0, pl.ds(off, ck), :], iqt_ref[...])
        score = jnp.zeros((ck, tq), _F32)
        for h in range(IDX_HEADS):
            score = score + jnp.maximum(rel[:, h * tq:(h + 1) * tq], 0.0) * iw_ref[0, h:h + 1, :]
        score = score * idx_scale
        sc_ref[c] = jnp.where(off + key_row <= t_pos, score, -jnp.inf)
        return 0

    lax.fori_loop(0, nch, score_body, 0)

    def count(pred):
        rows_acc = 64

        def body(c, acc):
            hit = jnp.where(pred(sc_ref[c], c * ck + key_row), 1.0, 0.0)
            return acc + jnp.sum(hit.reshape(ck // rows_acc, rows_acc, tq), axis=0)

        acc = lax.fori_loop(0, nch, body, jnp.zeros((rows_acc, tq), _F32))
        return jnp.sum(acc, axis=0, keepdims=True)

    def ordered_float(u):
        key = u ^ INT_MIN
        return pltpu.bitcast(key ^ ((key >> 31) & 0x7FFFFFFF), _F32)

    def bit_body(b, carry):
        t, cnt_t = carry
        cand = t | jnp.left_shift(jnp.int32(1), 31 - b)
        cand_f = ordered_float(cand)
        cnt = count(lambda sc, pos: sc >= cand_f)
        ok = cnt >= n_sel
        return jnp.where(ok, cand, t), jnp.where(ok, cnt, cnt_t)

    t0 = jnp.zeros((1, tq), jnp.int32)
    t, cnt_t = lax.fori_loop(0, 32, bit_body, (t0, jnp.full((1, tq), float(seq), _F32)))
    thr = jnp.where(t == 0, -jnp.inf, ordered_float(t))

    excess = jnp.max(jnp.where((cnt_t > n_sel) & (t != 0), 1.0, 0.0)) > 0.0

    def tie_limit():
        room = n_sel - count(lambda sc, pos: sc > thr)

        def lim_body(b, lim):
            cand = lim | jnp.left_shift(jnp.int32(1), 12 - b)
            cnt = count(lambda sc, pos: (sc == thr) & (pos < cand))
            return jnp.where(cnt <= room, cand, lim)

        return lax.fori_loop(0, 13, lim_body, jnp.zeros((1, tq), jnp.int32))

    limit = lax.cond(excess, tie_limit, lambda: jnp.full((1, tq), 2 * seq, jnp.int32))

    def bias_body(c, _):
        sc = sc_ref[c]
        pos = c * ck + key_row
        chosen = ((sc > thr) | ((sc == thr) & (pos < limit))) & (pos <= t_pos)
        sc_ref[c] = jnp.where(chosen, 0.0, NEG_BIG)
        return 0

    lax.fori_loop(0, nch, bias_body, 0)

    acc_ref[...] = jnp.zeros_like(acc_ref)
    grp = 2 * tq
    gcols = [slice(g * grp, (g + 1) * grp) for g in range(hq // grp)]

    def att_body(c, carry):
        off = pl.multiple_of(c * ck, ck)
        kv = kvc_ref[0, pl.ds(off, ck), :]
        bias = sc_ref[c]
        bias2 = jnp.concatenate([bias, bias], axis=1)
        ss = [_dot(kv, qct_ref[:, cols]) + bias2 for cols in gcols]
        ms, ls, es, alphas = [], [], [], []
        for g, cols in enumerate(gcols):
            m, l = carry[0][:, cols], carry[1][:, cols]
            m_new = jnp.maximum(m, jnp.max(ss[g], axis=0, keepdims=True))
            alpha = jnp.exp(m - m_new)
            e = jnp.exp(ss[g] - m_new)
            ls.append(alpha * l + jnp.sum(e, axis=0, keepdims=True))
            ms.append(m_new)
            alphas.append(alpha)
            es.append(e.astype(_BF16))
        kv_t = ckvt_ref[0, c]
        for g, cols in enumerate(gcols):
            acc_ref[:, cols] = alphas[g] * acc_ref[:, cols] + _dot(kv_t, es[g])
        return jnp.concatenate(ms, axis=1), jnp.concatenate(ls, axis=1)

    m0 = jnp.full((1, hq), NEG_BIG, _F32)
    l0 = jnp.zeros((1, hq), _F32)
    _, l = lax.fori_loop(0, nch, att_body, (m0, l0))
    o_lat_t = (acc_ref[...] / l).astype(_BF16)
    outs = [_dot(wuvt_ref[h], o_lat_t[:, h * tq:(h + 1) * tq]) for h in range(C_HEADS)]
    o_ref[0] = jnp.concatenate(outs, axis=0).T.astype(o_ref.dtype)


def _dsa_call(cqn_t, cqr_t, iq_t, iw_t, kvc, ckv_t, ik, w_uk, wuv_t):
    bsz, seq, _ = kvc.shape
    tq = DSA_TQ
    qcol = lambda n: pl.BlockSpec((1, n, tq), lambda b, i: (b, 0, i))
    full = lambda n: pl.BlockSpec((1, seq, n), lambda b, i: (b, 0, 0))
    whole = lambda a: pl.BlockSpec(a.shape, lambda b, i: (0,) * a.ndim)
    width = C_HEADS * C_V_DIM
    return pl.pallas_call(
        _dsa_body,
        out_shape=jax.ShapeDtypeStruct((bsz, seq, width), _BF16),
        grid=(bsz, seq // tq),
        in_specs=[
            qcol(512), qcol(256), qcol(256), qcol(IDX_HEADS), full(256),
            pl.BlockSpec((1,) + ckv_t.shape[1:], lambda b, i: (b, 0, 0, 0)),
            full(128), whole(w_uk), whole(wuv_t),
        ],
        out_specs=pl.BlockSpec((1, tq, width), lambda b, i: (b, i, 0)),
        scratch_shapes=[
            pltpu.VMEM((2 * LANES, C_HEADS * tq), _BF16),
            pltpu.VMEM((LANES, IDX_HEADS * tq), _BF16),
            pltpu.VMEM((seq // DSA_CK, DSA_CK, tq), _F32),
            pltpu.VMEM((C_KV_LATENT, C_HEADS * tq), _F32),
        ],
        compiler_params=_cparams(("arbitrary", "arbitrary")),
        name="dsa",
    )(cqn_t, cqr_t, iq_t, iw_t, kvc, ckv_t, ik, w_uk, wuv_t)


def _sb_body(qt_ref, k_ref, vt_ref, o_ref, acc_ref):
    i = pl.program_id(1)
    t = SB_T
    npair = D_HEADS // 2
    width = 2 * t
    qcats = [_head_pair_rhs(qt_ref[0, p * LANES:(p + 1) * LANES, :], HEAD_DIM) for p in range(npair)]
    key_row = _row_iota((t, width))
    q_lane = _lane_iota((t, width)) % t
    later = jnp.where(_lane_iota((t, t)) > _row_iota((t, t)), 1.0, 0.0).astype(_BF16)
    acc_ref[...] = jnp.zeros_like(acc_ref)

    def blocks(js, rest, masks):
        ks = [k_ref[0, pl.ds(pl.multiple_of(j * t, t), t), :] for j in js]
        zs = [[_dot(kj[:, p * LANES:(p + 1) * LANES], qcats[p]) for p in range(npair)] for kj in ks]
        rests = []
        for p in range(npair):
            run = rest[:, p * width:(p + 1) * width]
            for n, j in enumerate(js):
                z = zs[n][p]
                log_keep = -(jnp.maximum(z, 0.0) + jnp.log(1.0 + jnp.exp(-jnp.abs(z))))
                log_beta = z + log_keep
                if masks[n] is not None:
                    log_keep = jnp.where(masks[n], log_keep, 0.0)
                hi = log_keep.astype(_BF16)
                lo = (log_keep - hi.astype(_F32)).astype(_BF16)
                a = jnp.exp(log_beta + (_dot(later, hi) + _dot(later, lo) + run))
                if masks[n] is not None:
                    a = jnp.where(masks[n], a, 0.0)
                a = a.astype(_BF16)
                for hh in range(2):
                    rows = slice((2 * p + hh) * HEAD_DIM, (2 * p + hh + 1) * HEAD_DIM)
                    acc_ref[rows, :] = acc_ref[rows, :] + _dot(vt_ref[0, j, rows, :], a[:, hh * t:(hh + 1) * t])
                run = run + jnp.sum(log_keep, axis=0, keepdims=True)
            rests.append(run)
        return jnp.concatenate(rests, axis=1)

    rest = blocks([i], jnp.zeros((1, npair * width), _F32), [key_row < q_lane])

    def cond(carry):
        j, _, top = carry
        return (j >= 0) & (top > SB_UNDERFLOW)

    def body(carry):
        j, rest, _ = carry
        second_ok = jnp.broadcast_to(j >= 1, (t, width))
        rest = blocks([j, jnp.maximum(j - 1, 0)], rest, [None, second_ok])
        return j - 2, rest, jnp.max(rest)

    lax.while_loop(cond, body, (i - 1, rest, jnp.max(rest)))
    o_ref[0] = acc_ref[...].T.astype(o_ref.dtype)


def _sb_call(q_t, k, v_t):
    bsz, seq, width = k.shape
    t = SB_T
    return pl.pallas_call(
        _sb_body,
        out_shape=jax.ShapeDtypeStruct((bsz, seq, width), _BF16),
        grid=(bsz, seq // t),
        in_specs=[
            pl.BlockSpec((1, width, t), lambda b, i: (b, 0, i)),
            pl.BlockSpec((1, seq, width), lambda b, i: (b, 0, 0)),
            pl.BlockSpec((1, seq // t, width, t), lambda b, i: (b, 0, 0, 0)),
        ],
        out_specs=pl.BlockSpec((1, t, width), lambda b, i: (b, i, 0)),
        scratch_shapes=[pltpu.VMEM((width, t), _F32)],
        compiler_params=_cparams(("arbitrary", "arbitrary")),
        name="stick_breaking",
    )(q_t, k, v_t)


def _even_layer(x, mod, w_in, sinks, w_out, ln_g, ln_b, rope_h):
    aq_t, ak, bq, bk, av_t, bv, g = _inproj_even_call(x, mod, _prep_w_even(w_in), *rope_h)
    oa = _moba_call(aq_t, ak, av_t)
    ob = _swa_call(bq, bk, bv, sinks)
    return _outproj_call(oa, ob, g, x, mod, w_out, ln_g, ln_b)


def _odd_layer(x, mod, w_in, kv_g, w_uk, w_uv, w_out, ln_g, ln_b, rope_r, rope_i):
    cqn_t, cqr_t, kvc, ckv_t, iq_t, ik, iw_t, dq_t, dk, dv_t, g = _inproj_odd_call(
        x, mod, _prep_w_odd(w_in), rope_r, rope_i, kv_g)
    oc = _dsa_call(cqn_t, cqr_t, iq_t, iw_t, kvc, ckv_t, ik,
                   w_uk.astype(_BF16), w_uv.transpose(0, 2, 1).astype(_BF16))
    od = _sb_call(dq_t, dk, dv_t)
    return _outproj_call(oc, od, g, x, mod, w_out, ln_g, ln_b)


def kernel(x, c, w_ada, b_ada, w_in_even, sink_logits, w_in_odd, kv_norm_g, w_uk, w_uv, w_out, ln_g, ln_b):
    bsz, seq, d = x.shape
    rope_h = _rope_tables(seq, HEAD_DIM, LANES)
    rope_r = _rope_tables(seq, C_ROPE_DIM, LANES)
    rope_i = _rope_tables(seq, IDX_DIM, LANES)
    mods = _ada_call(c, w_ada, b_ada).reshape(DEPTH, bsz, 3, d)
    for layer in range(DEPTH):
        mod = mods[layer]
        j = layer // 2
        if layer % 2 == 0:
            x = _even_layer(x, mod, w_in_even[j], sink_logits[j], w_out[layer], ln_g[layer], ln_b[layer], rope_h)
        else:
            x = _odd_layer(x, mod, w_in_odd[j], kv_norm_g[j], w_uk[j], w_uv[j], w_out[layer],
                           ln_g[layer], ln_b[layer], rope_r, rope_i)
    return x
```

```python
import jax
import jax.numpy as jnp
from jax import lax
from jax.experimental import pallas as pl
from jax.experimental.pallas import tpu as pltpu

D_MODEL = 1024
DEPTH = 2
HEAD_DIM = 64
ROPE_THETA = 10000.0
LN_EPS = 1e-5
A_HEADS = 8
MOBA_BLOCK = 256
MOBA_TOPK = 3
B_HEADS = 8
B_KV_HEADS = 2
SWA_WINDOW = 128
C_HEADS = 8
C_NOPE_DIM = 64
C_ROPE_DIM = 32
C_V_DIM = 64
C_KV_LATENT = 128
IDX_HEADS = 8
IDX_DIM = 32
DSA_TOPK = 256
D_HEADS = 8
MIX_WIDTH = 1024
EVEN_SIZES = (512, 512, 512, 512, 128, 128, 1024)
ODD_SIZES = (512, 256, 128, 32, 256, 32, 8, 512, 512, 512, 1024)
DEEPNORM_ALPHA = (2 * DEPTH) ** 0.25

LANES = 128
NEG_BIG = -1e30
INT_MIN = -2 ** 31
SB_UNDERFLOW = -104.0

PROJ_TM = 512
DSA_TQ = 256
DSA_CK = 512
SB_T = 128
MOBA_PAIRS = 4
VMEM_LIMIT = 48 * 1024 * 1024

_BF16 = jnp.bfloat16
_F32 = jnp.float32


def _cparams(sem):
    return pltpu.CompilerParams(dimension_semantics=sem, vmem_limit_bytes=VMEM_LIMIT)


def _dot_t(a, b):
    return lax.dot_general(a, b, (((1,), (1,)), ((), ())), preferred_element_type=_F32)


def _dot(a, b):
    return jnp.dot(a, b, preferred_element_type=_F32)


def _lane_iota(shape):
    return lax.broadcasted_iota(jnp.int32, shape, len(shape) - 1)


def _row_iota(shape):
    return lax.broadcasted_iota(jnp.int32, shape, len(shape) - 2)


def _head_pair_rhs(q_t, head_dim):
    frow = _row_iota(q_t.shape)
    return jnp.concatenate(
        [jnp.where((frow // head_dim) == hh, q_t, jnp.zeros_like(q_t)) for hh in range(2)], axis=1)


def _ada_body(c_ref, w_ref, b_ref, o_ref):
    c = c_ref[...]
    cond = c * jax.nn.sigmoid(c)
    o_ref[0] = _dot(cond, w_ref[0]) + b_ref[0]


def _ada_call(c, w_ada, b_ada):
    depth, d, n3 = w_ada.shape
    bsz = c.shape[0]
    nb = n3 // d
    return pl.pallas_call(
        _ada_body,
        out_shape=jax.ShapeDtypeStruct((depth, bsz, n3), _F32),
        grid=(depth, nb),
        in_specs=[
            pl.BlockSpec((bsz, d), lambda l, j: (0, 0)),
            pl.BlockSpec((1, d, d), lambda l, j: (l, 0, j)),
            pl.BlockSpec((1, 1, d), lambda l, j: (l, 0, j)),
        ],
        out_specs=pl.BlockSpec((1, bsz, d), lambda l, j: (l, 0, j)),
        compiler_params=_cparams(("arbitrary", "arbitrary")),
        name="ada_mod",
    )(c, w_ada, b_ada.reshape(depth, 1, n3))


def _rope_tables(seq, dim, period_lanes):
    inv = 1.0 / (ROPE_THETA ** (jnp.arange(0, dim, 2, dtype=_F32) / dim))
    ang = jnp.arange(seq, dtype=_F32)[:, None] * inv[None, :]
    cos, sin = jnp.cos(ang), jnp.sin(ang)
    cos_h = jnp.concatenate([cos, cos], axis=1)
    sin_h = jnp.concatenate([-sin, sin], axis=1)
    reps = period_lanes // dim
    return jnp.tile(cos_h, (1, reps)), jnp.tile(sin_h, (1, reps))


def _rope_piece(x, cos, sin_signed, dim):
    half = dim // 2
    first = (_lane_iota(x.shape) % dim) < half
    partner = jnp.where(first, pltpu.roll(x, LANES - half, 1), pltpu.roll(x, half, 1))
    return x * cos + partner * sin_signed


def _modulated(x_ref, mod_ref):
    x = x_ref[0]
    shift = mod_ref[0, 0:1, :]
    scale = mod_ref[0, 1:2, :]
    return (x * (1.0 + scale) + shift).astype(_BF16)


def _silu_gate(h, w_ref, col0, g_ref):
    for p in range(2):
        gate = _dot(h, w_ref[:, col0 + 512 * p:col0 + 512 * (p + 1)])
        g_ref[0, :, 512 * p:512 * (p + 1)] = (gate * jax.nn.sigmoid(gate)).astype(_BF16)


def _store_blocks_t(acc, out_ref, blk):
    tm, n = acc.shape
    for r in range(tm // blk):
        for p in range(n // LANES):
            out_ref[0, r, p * LANES:(p + 1) * LANES, :] = (
                acc[r * blk:(r + 1) * blk, p * LANES:(p + 1) * LANES].T.astype(_BF16))


EVEN_COLS = 3584


def _prep_w_even(w):
    offs = [0]
    for n in EVEN_SIZES:
        offs.append(offs[-1] + n)
    aq, ak, av, bq, bk, bv, gate = [w[:, offs[i]:offs[i + 1]] for i in range(len(EVEN_SIZES))]

    def dup(t):
        parts = []
        for g in range(B_KV_HEADS):
            blk = t[:, g * HEAD_DIM:(g + 1) * HEAD_DIM]
            parts += [blk, blk]
        return jnp.concatenate(parts, axis=1)

    return jnp.concatenate([aq, ak, bq, dup(bk), av, dup(bv), gate], axis=1).astype(_BF16)


def _inproj_even_body(x_ref, mod_ref, w_ref, cos_ref, sin_ref,
                      aqt_ref, ak_ref, bq_ref, bk_ref, avt_ref, bv_ref, g_ref):
    h = _modulated(x_ref, mod_ref)
    cos = cos_ref[...]
    sin = sin_ref[...]
    q_scale = HEAD_DIM ** -0.5

    def roped(col0, ncols, out_ref, scale, transposed=False):
        acc = _dot(h, w_ref[:, col0:col0 + ncols])
        for p in range(ncols // LANES):
            piece = _rope_piece(acc[:, p * LANES:(p + 1) * LANES], cos, sin, HEAD_DIM)
            if scale != 1.0:
                piece = piece * scale
            if transposed:
                out_ref[0, p * LANES:(p + 1) * LANES, :] = piece.T.astype(_BF16)
            else:
                out_ref[0, :, p * LANES:(p + 1) * LANES] = piece.astype(_BF16)

    roped(0, 512, aqt_ref, q_scale, transposed=True)
    roped(512, 512, ak_ref, 1.0)
    roped(1024, 512, bq_ref, q_scale)
    roped(1536, 256, bk_ref, 1.0)
    _store_blocks_t(_dot(h, w_ref[:, 1792:2304]), avt_ref, MOBA_BLOCK)
    bv_ref[0] = _dot(h, w_ref[:, 2304:2560]).astype(_BF16)
    _silu_gate(h, w_ref, 2560, g_ref)


def _inproj_even_call(x, mod, w, cos, sin):
    bsz, seq, d = x.shape
    tm = PROJ_TM
    row = lambda n: pl.BlockSpec((1, tm, n), lambda b, i: (b, i, 0))
    tok = lambda n: jax.ShapeDtypeStruct((bsz, seq, n), _BF16)
    per_tile = tm // MOBA_BLOCK
    return pl.pallas_call(
        _inproj_even_body,
        out_shape=[
            jax.ShapeDtypeStruct((bsz, 512, seq), _BF16), tok(512), tok(512), tok(256),
            jax.ShapeDtypeStruct((bsz, seq // MOBA_BLOCK, 512, MOBA_BLOCK), _BF16), tok(256), tok(1024),
        ],
        grid=(bsz, seq // tm),
        in_specs=[
            row(d),
            pl.BlockSpec((1, 3, d), lambda b, i: (b, 0, 0)),
            pl.BlockSpec((d, EVEN_COLS), lambda b, i: (0, 0)),
            pl.BlockSpec((tm, LANES), lambda b, i: (i, 0)),
            pl.BlockSpec((tm, LANES), lambda b, i: (i, 0)),
        ],
        out_specs=[
            pl.BlockSpec((1, 512, tm), lambda b, i: (b, 0, i)), row(512), row(512), row(256),
            pl.BlockSpec((1, per_tile, 512, MOBA_BLOCK), lambda b, i: (b, i, 0, 0)), row(256), row(1024),
        ],
        compiler_params=_cparams(("arbitrary", "arbitrary")),
        name="inproj_even",
    )(x, mod, w, cos, sin)


def _moba_body(qt_ref, k_ref, vt_ref, o_ref, km_ref, selb_ref, acc_ref):
    qi = pl.program_id(2)
    tq = MOBA_BLOCK
    nblk = km_ref.shape[0]
    npair = qt_ref.shape[1] // LANES
    pw = 2 * tq
    width = npair * pw

    @pl.when(qi == 0)
    def _():
        for r in range(nblk):
            blk_k = k_ref[0, r * tq:(r + 1) * tq, :].astype(_F32)
            km_ref[r:r + 1, :] = jnp.mean(blk_k, axis=0, keepdims=True)

    qcats = [_head_pair_rhs(qt_ref[0, p * LANES:(p + 1) * LANES, :], HEAD_DIM) for p in range(npair)]

    blk = _row_iota((nblk, width))
    blkf = blk.astype(_F32)
    valid = blk < qi
    gate = jnp.concatenate([_dot(km_ref[:, p * LANES:(p + 1) * LANES].astype(_BF16), qcats[p])
                            for p in range(npair)], axis=1)
    gate = jnp.where(valid, gate, -jnp.inf)
    sel = jnp.zeros((nblk, width), _F32)
    for _ in range(MOBA_TOPK):
        top = jnp.max(gate, axis=0, keepdims=True)
        first = jnp.min(jnp.where(gate == top, blkf, float(nblk)), axis=0, keepdims=True)
        pick = blkf == first
        sel = jnp.where(pick, 1.0, sel)
        gate = jnp.where(pick, -jnp.inf, gate)
    selb_ref[...] = jnp.where(valid & (sel > 0.0), 0.0, NEG_BIG)

    acc_ref[...] = jnp.zeros_like(acc_ref)

    def step(carry, j, bias_of_pair):
        m, l = carry
        kj = k_ref[0, pl.ds(pl.multiple_of(j * tq, tq), tq), :]
        ss = [_dot(kj[:, p * LANES:(p + 1) * LANES], qcats[p]) + bias_of_pair(p) for p in range(npair)]
        ms, ls = [], []
        for p in range(npair):
            pcols = slice(p * pw, (p + 1) * pw)
            m_new = jnp.maximum(m[:, pcols], jnp.max(ss[p], axis=0, keepdims=True))
            alpha = jnp.exp(m[:, pcols] - m_new)
            e = jnp.exp(ss[p] - m_new)
            ls.append(alpha * l[:, pcols] + jnp.sum(e, axis=0, keepdims=True))
            ms.append(m_new)
            e = e.astype(_BF16)
            for hh in range(2):
                rows = slice((2 * p + hh) * HEAD_DIM, (2 * p + hh + 1) * HEAD_DIM)
                cols = slice(hh * tq, (hh + 1) * tq)
                acc_ref[rows, :] = alpha[:, cols] * acc_ref[rows, :] + _dot(vt_ref[0, j, rows, :], e[:, cols])
        return jnp.concatenate(ms, axis=1), jnp.concatenate(ls, axis=1)

    def past_body(j, carry):
        row_bias = selb_ref[pl.ds(j, 1), :]
        return step(carry, j, lambda p: row_bias[:, p * pw:(p + 1) * pw])

    carry = (jnp.full((1, width), NEG_BIG, _F32), jnp.zeros((1, width), _F32))
    carry = lax.fori_loop(0, qi, past_body, carry)
    causal = jnp.where(_row_iota((tq, pw)) <= _lane_iota((tq, pw)) % tq, 0.0, NEG_BIG)
    _, l = step(carry, qi, lambda p: causal)
    inv = 1.0 / l
    out_t = jnp.concatenate([acc_ref[h * HEAD_DIM:(h + 1) * HEAD_DIM, :] * inv[:, h * tq:(h + 1) * tq]
                             for h in range(2 * npair)], axis=0)
    o_ref[0] = out_t.T.astype(o_ref.dtype)


def _moba_call(q_t, k, v_t):
    bsz, seq, width = k.shape
    tq = MOBA_BLOCK
    nblk = seq // tq
    gw = MOBA_PAIRS * LANES
    return pl.pallas_call(
        _moba_body,
        out_shape=jax.ShapeDtypeStruct((bsz, seq, width), _BF16),
        grid=(bsz, width // gw, nblk),
        in_specs=[
            pl.BlockSpec((1, gw, tq), lambda b, h, i: (b, h, i)),
            pl.BlockSpec((1, seq, gw), lambda b, h, i: (b, 0, h)),
            pl.BlockSpec((1, nblk, gw, tq), lambda b, h, i: (b, 0, h, 0)),
        ],
        out_specs=pl.BlockSpec((1, tq, gw), lambda b, h, i: (b, i, h)),
        scratch_shapes=[
            pltpu.VMEM((nblk, gw), _F32),
            pltpu.VMEM((nblk, MOBA_PAIRS * 2 * tq), _F32),
            pltpu.VMEM((gw, tq), _F32),
        ],
        compiler_params=_cparams(("arbitrary", "arbitrary", "arbitrary")),
        name="moba",
    )(q_t, k, v_t)


def _swa_body(sink_ref, q_ref, kp_ref, kc_ref, vp_ref, vc_ref, o_ref):
    i = pl.program_id(1)
    w = SWA_WINDOW
    lane = _lane_iota((1, LANES))
    row = _row_iota((w, 2 * w))
    col = _lane_iota((w, 2 * w))
    mask = (col > row) & (col <= row + w) & ((col >= w) | (i > 0))
    pairs_per_kv = (B_HEADS // B_KV_HEADS) // 2
    ks = [jnp.concatenate([kp_ref[0, :, g * LANES:(g + 1) * LANES],
                           kc_ref[0, :, g * LANES:(g + 1) * LANES]], axis=0) for g in range(B_KV_HEADS)]
    vs = [jnp.concatenate([vp_ref[0, :, g * LANES:(g + 1) * LANES],
                           vc_ref[0, :, g * LANES:(g + 1) * LANES]], axis=0) for g in range(B_KV_HEADS)]
    ss = []
    for h in range(B_HEADS):
        qp = q_ref[0, :, (h // 2) * LANES:(h // 2 + 1) * LANES]
        qh = jnp.where((lane // HEAD_DIM) == (h % 2), qp, jnp.zeros_like(qp))
        ss.append(_dot_t(qh, ks[(h // 2) // pairs_per_kv]))
    outs = []
    for h in range(B_HEADS):
        sink = sink_ref[h]
        s = jnp.where(mask, ss[h], -jnp.inf)
        m = jnp.maximum(jnp.max(s, axis=1, keepdims=True), sink)
        e = jnp.exp(s - m)
        l = jnp.sum(e, axis=1, keepdims=True) + jnp.exp(sink - m)
        outs.append(_dot(e.astype(_BF16), vs[(h // 2) // pairs_per_kv]) / l)
    for p in range(B_HEADS // 2):
        o_ref[0, :, p * LANES:(p + 1) * LANES] = jnp.where(
            lane < HEAD_DIM, outs[2 * p], outs[2 * p + 1]).astype(o_ref.dtype)


def _swa_call(q, kdup, vdup, sinks):
    bsz, seq, width = q.shape
    w = SWA_WINDOW
    kvw = kdup.shape[2]
    prev = pl.BlockSpec((1, w, kvw), lambda b, i: (b, jnp.maximum(i - 1, 0), 0))
    cur = pl.BlockSpec((1, w, kvw), lambda b, i: (b, i, 0))
    return pl.pallas_call(
        _swa_body,
        out_shape=jax.ShapeDtypeStruct((bsz, seq, width), _BF16),
        grid=(bsz, seq // w),
        in_specs=[
            pl.BlockSpec(memory_space=pltpu.SMEM),
            pl.BlockSpec((1, w, width), lambda b, i: (b, i, 0)),
            prev, cur, prev, cur,
        ],
        out_specs=pl.BlockSpec((1, w, width), lambda b, i: (b, i, 0)),
        compiler_params=_cparams(("arbitrary", "arbitrary")),
        name="swa_sink",
    )(sinks, q, kdup, kdup, vdup, vdup)


def _outproj_body(o1_ref, o2_ref, g_ref, x_ref, mod_ref, w_ref, lng_ref, lnb_ref, y_ref):
    half = o1_ref.shape[2]
    a1 = o1_ref[0] * g_ref[0, :, :half]
    a2 = o2_ref[0] * g_ref[0, :, half:]
    y = _dot(a1, w_ref[:half, :]) + _dot(a2, w_ref[half:, :])
    gate = mod_ref[0, 2:3, :]
    z = DEEPNORM_ALPHA * x_ref[0] + (1.0 + gate) * y
    mu = jnp.mean(z, axis=1, keepdims=True)
    zc = z - mu
    var = jnp.mean(zc * zc, axis=1, keepdims=True)
    y_ref[0] = zc * lax.rsqrt(var + LN_EPS) * lng_ref[...] + lnb_ref[...]


def _outproj_call(o1, o2, g, x, mod, w_out, ln_g, ln_b):
    bsz, seq, d = x.shape
    tm = PROJ_TM
    half = o1.shape[2]
    row = lambda n: pl.BlockSpec((1, tm, n), lambda b, i: (b, i, 0))
    return pl.pallas_call(
        _outproj_body,
        out_shape=jax.ShapeDtypeStruct((bsz, seq, d), _F32),
        grid=(bsz, seq // tm),
        in_specs=[
            row(half), row(half), row(2 * half), row(d),
            pl.BlockSpec((1, 3, d), lambda b, i: (b, 0, 0)),
            pl.BlockSpec((2 * half, d), lambda b, i: (0, 0)),
            pl.BlockSpec((1, d), lambda b, i: (0, 0)),
            pl.BlockSpec((1, d), lambda b, i: (0, 0)),
        ],
        out_specs=row(d),
        compiler_params=_cparams(("arbitrary", "arbitrary")),
        name="outproj_deepnorm",
    )(o1, o2, g, x, mod, w_out.astype(_BF16), ln_g.reshape(1, d), ln_b.reshape(1, d))


ODD_COLS = 4096


def _prep_w_odd(w):
    offs = [0]
    for n in ODD_SIZES:
        offs.append(offs[-1] + n)
    cqn, cqr, ckv, ckr, iq, ik, iw, dq, dk, dv, gate = [w[:, offs[i]:offs[i + 1]] for i in range(len(ODD_SIZES))]
    iw_blk = jnp.concatenate([iw, jnp.zeros((w.shape[0], LANES - IDX_HEADS), w.dtype)], axis=1)
    return jnp.concatenate([cqn, cqr, ckv, jnp.tile(ckr, (1, 4)), iq, jnp.tile(ik, (1, 4)), iw_blk,
                            dq, dk, dv, gate], axis=1).astype(_BF16)


def _inproj_odd_body(x_ref, mod_ref, w_ref, cosr_ref, sinr_ref, cosi_ref, sini_ref, kvg_ref,
                     cqn_ref, cqr_ref, kvc_ref, ckvt_ref, iq_ref, ik_ref, iw_ref,
                     dqt_ref, dk_ref, dvt_ref, g_ref):
    h = _modulated(x_ref, mod_ref)
    cosr, sinr = cosr_ref[...], sinr_ref[...]
    cosi, sini = cosi_ref[...], sini_ref[...]

    acc = _dot(h, w_ref[:, 0:512])
    for p in range(4):
        cqn_ref[0, p * LANES:(p + 1) * LANES, :] = acc[:, p * LANES:(p + 1) * LANES].T.astype(_BF16)
    acc = _dot(h, w_ref[:, 512:1024])
    for p in range(2):
        piece = _rope_piece(acc[:, p * LANES:(p + 1) * LANES], cosr, sinr, C_ROPE_DIM)
        cqr_ref[0, p * LANES:(p + 1) * LANES, :] = piece.T.astype(_BF16)
    ckv = acc[:, 256:384]
    ckv = ckv * lax.rsqrt(jnp.mean(ckv * ckv, axis=1, keepdims=True) + LN_EPS) * kvg_ref[...]
    kvc_ref[0, :, 0:LANES] = ckv.astype(_BF16)
    ckvt_ref[0, 0] = ckv.T.astype(_BF16)
    kvc_ref[0, :, LANES:2 * LANES] = _rope_piece(acc[:, 384:512], cosr, sinr, C_ROPE_DIM).astype(_BF16)
    acc = _dot(h, w_ref[:, 1024:1536])
    for p in range(2):
        piece = _rope_piece(acc[:, p * LANES:(p + 1) * LANES], cosi, sini, IDX_DIM)
        iq_ref[0, p * LANES:(p + 1) * LANES, :] = piece.T.astype(_BF16)
    ik_ref[0] = _rope_piece(acc[:, 256:384], cosi, sini, IDX_DIM).astype(_BF16)
    iw_ref[0] = acc[:, 384:512].T[0:IDX_HEADS, :]
    acc = _dot(h, w_ref[:, 1536:2048]) * (HEAD_DIM ** -0.5)
    for p in range(4):
        dqt_ref[0, p * LANES:(p + 1) * LANES, :] = acc[:, p * LANES:(p + 1) * LANES].T.astype(_BF16)
    dk_ref[0] = _dot(h, w_ref[:, 2048:2560]).astype(_BF16)
    _store_blocks_t(_dot(h, w_ref[:, 2560:3072]), dvt_ref, SB_T)
    _silu_gate(h, w_ref, 3072, g_ref)


def _inproj_odd_call(x, mod, w, rope_r, rope_i, kv_g):
    bsz, seq, d = x.shape
    tm = DSA_CK
    row = lambda n: pl.BlockSpec((1, tm, n), lambda b, i: (b, i, 0))
    col = lambda n: pl.BlockSpec((1, n, tm), lambda b, i: (b, 0, i))
    tab = pl.BlockSpec((tm, LANES), lambda b, i: (i, 0))
    tok = lambda n, dt: jax.ShapeDtypeStruct((bsz, seq, n), dt)
    feat = lambda n, dt: jax.ShapeDtypeStruct((bsz, n, seq), dt)
    return pl.pallas_call(
        _inproj_odd_body,
        out_shape=[
            feat(512, _BF16), feat(256, _BF16), tok(256, _BF16),
            jax.ShapeDtypeStruct((bsz, seq // tm, C_KV_LATENT, tm), _BF16),
            feat(256, _BF16), tok(128, _BF16), feat(IDX_HEADS, _F32),
            feat(512, _BF16), tok(512, _BF16),
            jax.ShapeDtypeStruct((bsz, seq // SB_T, 512, SB_T), _BF16), tok(1024, _BF16),
        ],
        grid=(bsz, seq // tm),
        in_specs=[
            row(d),
            pl.BlockSpec((1, 3, d), lambda b, i: (b, 0, 0)),
            pl.BlockSpec((d, ODD_COLS), lambda b, i: (0, 0)),
            tab, tab, tab, tab,
            pl.BlockSpec((1, LANES), lambda b, i: (0, 0)),
        ],
        out_specs=[
            col(512), col(256), row(256),
            pl.BlockSpec((1, 1, C_KV_LATENT, tm), lambda b, i: (b, i, 0, 0)),
            col(256), row(128), col(IDX_HEADS),
            col(512), row(512),
            pl.BlockSpec((1, tm // SB_T, 512, SB_T), lambda b, i: (b, i, 0, 0)), row(1024),
        ],
        compiler_params=_cparams(("arbitrary", "arbitrary")),
        name="inproj_odd",
    )(x, mod, w, rope_r[0], rope_r[1], rope_i[0], rope_i[1], kv_g.reshape(1, LANES))


def _dsa_body(cqn_ref, cqr_ref, iq_ref, iw_ref, kvc_ref, ckvt_ref, ik_ref, wuk_ref, wuvt_ref, o_ref,
              qct_ref, iqt_ref, sc_ref, acc_ref):
    i = pl.program_id(1)
    tq, ck = DSA_TQ, DSA_CK
    seq = kvc_ref.shape[1]
    n_sel = float(min(DSA_TOPK, seq // 4))
    nch = i // (ck // tq) + 1
    att_scale = (C_NOPE_DIM + C_ROPE_DIM) ** -0.5
    idx_scale = (IDX_DIM * IDX_HEADS) ** -0.5
    hq = C_HEADS * tq

    row128 = _row_iota((LANES, tq))
    for h in range(C_HEADS):
        cols = slice(h * tq, (h + 1) * tq)
        qn_t = cqn_ref[0, h * C_NOPE_DIM:(h + 1) * C_NOPE_DIM, :]
        qlat_t = _dot(wuk_ref[h], qn_t) * att_scale
        qct_ref[0:C_KV_LATENT, cols] = qlat_t.astype(_BF16)
        qr_t = cqr_ref[0, h * C_ROPE_DIM:(h + 1) * C_ROPE_DIM, :].astype(_F32) * att_scale
        qct_ref[C_KV_LATENT:C_KV_LATENT + C_ROPE_DIM, cols] = qr_t.astype(_BF16)
        qct_ref[C_KV_LATENT + C_ROPE_DIM:, cols] = jnp.zeros((LANES - C_ROPE_DIM, tq), _BF16)
        blk4 = h // 4
        iq_t = iq_ref[0, blk4 * LANES:(blk4 + 1) * LANES, :]
        iqt_ref[:, cols] = jnp.where((row128 // IDX_DIM) == (h % 4), iq_t, jnp.zeros_like(iq_t))

    t_pos = i * tq + _lane_iota((ck, tq))
    key_row = _row_iota((ck, tq))

    def score_body(c, _):
        off = pl.multiple_of(c * ck, ck)
        rel = _dot(ik_ref[0, pl.ds(off, ck), :], iqt_ref[...])
        score = jnp.zeros((ck, tq), _F32)
        for h in range(IDX_HEADS):
            score = score + jnp.maximum(rel[:, h * tq:(h + 1) * tq], 0.0) * iw_ref[0, h:h + 1, :]
        score = score * idx_scale
        sc_ref[c] = jnp.where(off + key_row <= t_pos, score, -jnp.inf)
        return 0

    lax.fori_loop(0, nch, score_body, 0)

    def count(pred):
        rows_acc = 8 * (8 * LANES // tq)

        slab_row = _row_iota((rows_acc, tq))

        def body(c, acc):
            for r in range(ck // rows_acc):
                rows = slice(r * rows_acc, (r + 1) * rows_acc)
                acc = acc + jnp.where(pred(sc_ref[c, rows, :], c * ck + r * rows_acc + slab_row), 1.0, 0.0)
            return acc

        acc = lax.fori_loop(0, nch, body, jnp.zeros((rows_acc, tq), _F32))
        return jnp.sum(acc, axis=0, keepdims=True)

    def ordered_float(u):
        key = u ^ INT_MIN
        return pltpu.bitcast(key ^ ((key >> 31) & 0x7FFFFFFF), _F32)

    def bit_body(b, carry):
        t, cnt_t = carry
        cand = t | jnp.left_shift(jnp.int32(1), 31 - b)
        cand_f = ordered_float(cand)
        cnt = count(lambda sc, pos: sc >= cand_f)
        ok = cnt >= n_sel
        return jnp.where(ok, cand, t), jnp.where(ok, cnt, cnt_t)

    t0 = jnp.zeros((1, tq), jnp.int32)
    t, cnt_t = lax.fori_loop(0, 32, bit_body, (t0, jnp.full((1, tq), float(seq), _F32)))
    thr = jnp.where(t == 0, -jnp.inf, ordered_float(t))

    excess = jnp.max(jnp.where((cnt_t > n_sel) & (t != 0), 1.0, 0.0)) > 0.0

    def tie_limit():
        room = n_sel - count(lambda sc, pos: sc > thr)

        def lim_body(b, lim):
            cand = lim | jnp.left_shift(jnp.int32(1), 12 - b)
            cnt = count(lambda sc, pos: (sc == thr) & (pos < cand))
            return jnp.where(cnt <= room, cand, lim)

        return lax.fori_loop(0, 13, lim_body, jnp.zeros((1, tq), jnp.int32))

    limit = lax.cond(excess, tie_limit, lambda: jnp.full((1, tq), 2 * seq, jnp.int32))

    def bias_body(c, _):
        sc = sc_ref[c]
        pos = c * ck + key_row
        chosen = ((sc > thr) | ((sc == thr) & (pos < limit))) & (pos <= t_pos)
        sc_ref[c] = jnp.where(chosen, 0.0, NEG_BIG)
        return 0

    lax.fori_loop(0, nch, bias_body, 0)

    acc_ref[...] = jnp.zeros_like(acc_ref)
    grp = 2 * tq
    gcols = [slice(g * grp, (g + 1) * grp) for g in range(hq // grp)]

    def att_body(c, carry):
        off = pl.multiple_of(c * ck, ck)
        kv = kvc_ref[0, pl.ds(off, ck), :]
        bias = sc_ref[c]
        bias2 = jnp.concatenate([bias, bias], axis=1)
        ss = [_dot(kv, qct_ref[:, cols]) + bias2 for cols in gcols]
        ms, ls, es, alphas = [], [], [], []
        for g, cols in enumerate(gcols):
            m, l = carry[0][:, cols], carry[1][:, cols]
            m_new = jnp.maximum(m, jnp.max(ss[g], axis=0, keepdims=True))
            alpha = jnp.exp(m - m_new)
            e = jnp.exp(ss[g] - m_new)
            ls.append(alpha * l + jnp.sum(e, axis=0, keepdims=True))
            ms.append(m_new)
            alphas.append(alpha)
            es.append(e.astype(_BF16))
        kv_t = ckvt_ref[0, c]
        for g, cols in enumerate(gcols):
            acc_ref[:, cols] = alphas[g] * acc_ref[:, cols] + _dot(kv_t, es[g])
        return jnp.concatenate(ms, axis=1), jnp.concatenate(ls, axis=1)

    m0 = jnp.full((1, hq), NEG_BIG, _F32)
    l0 = jnp.zeros((1, hq), _F32)
    _, l = lax.fori_loop(0, nch, att_body, (m0, l0))
    o_lat_t = (acc_ref[...] / l).astype(_BF16)
    outs = [_dot(wuvt_ref[h], o_lat_t[:, h * tq:(h + 1) * tq]) for h in range(C_HEADS)]
    o_ref[0] = jnp.concatenate(outs, axis=0).T.astype(o_ref.dtype)


def _dsa_call(cqn_t, cqr_t, iq_t, iw_t, kvc, ckv_t, ik, w_uk, wuv_t):
    bsz, seq, _ = kvc.shape
    tq = DSA_TQ
    qcol = lambda n: pl.BlockSpec((1, n, tq), lambda b, i: (b, 0, i))
    full = lambda n: pl.BlockSpec((1, seq, n), lambda b, i: (b, 0, 0))
    whole = lambda a: pl.BlockSpec(a.shape, lambda b, i: (0,) * a.ndim)
    width = C_HEADS * C_V_DIM
    return pl.pallas_call(
        _dsa_body,
        out_shape=jax.ShapeDtypeStruct((bsz, seq, width), _BF16),
        grid=(bsz, seq // tq),
        in_specs=[
            qcol(512), qcol(256), qcol(256), qcol(IDX_HEADS), full(256),
            pl.BlockSpec((1,) + ckv_t.shape[1:], lambda b, i: (b, 0, 0, 0)),
            full(128), whole(w_uk), whole(wuv_t),
        ],
        out_specs=pl.BlockSpec((1, tq, width), lambda b, i: (b, i, 0)),
        scratch_shapes=[
            pltpu.VMEM((2 * LANES, C_HEADS * tq), _BF16),
            pltpu.VMEM((LANES, IDX_HEADS * tq), _BF16),
            pltpu.VMEM((seq // DSA_CK, DSA_CK, tq), _F32),
            pltpu.VMEM((C_KV_LATENT, C_HEADS * tq), _F32),
        ],
        compiler_params=_cparams(("arbitrary", "arbitrary")),
        name="dsa",
    )(cqn_t, cqr_t, iq_t, iw_t, kvc, ckv_t, ik, w_uk, wuv_t)


def _sb_body(qt_ref, k_ref, vt_ref, o_ref, acc_ref):
    i = pl.program_id(1)
    t = SB_T
    npair = D_HEADS // 2
    width = 2 * t
    qcats = [_head_pair_rhs(qt_ref[0, p * LANES:(p + 1) * LANES, :], HEAD_DIM) for p in range(npair)]
    key_row = _row_iota((t, width))
    q_lane = _lane_iota((t, width)) % t
    later = jnp.where(_lane_iota((t, t)) > _row_iota((t, t)), 1.0, 0.0).astype(_BF16)
    later2 = jnp.concatenate([later, later], axis=1)
    acc_ref[...] = jnp.zeros_like(acc_ref)

    def blocks(js, rest, masks):
        ks = [k_ref[0, pl.ds(pl.multiple_of(j * t, t), t), :] for j in js]
        zs = [[_dot(kj[:, p * LANES:(p + 1) * LANES], qcats[p]) for p in range(npair)] for kj in ks]
        rests = []
        for p in range(npair):
            run = rest[:, p * width:(p + 1) * width]
            for n, j in enumerate(js):
                z = zs[n][p]
                log_keep = -(jnp.maximum(z, 0.0) + jnp.log(1.0 + jnp.exp(-jnp.abs(z))))
                log_beta = z + log_keep
                if masks[n] is not None:
                    log_keep = jnp.where(masks[n], log_keep, 0.0)
                hi = log_keep.astype(_BF16)
                lo = (log_keep - hi.astype(_F32)).astype(_BF16)
                a = jnp.exp(log_beta + (_dot(later2, jnp.concatenate([hi, lo], axis=0)) + run))
                if masks[n] is not None:
                    a = jnp.where(masks[n], a, 0.0)
                prows = slice(p * LANES, (p + 1) * LANES)
                pv = _dot(vt_ref[0, j, prows, :], a.astype(_BF16))
                for hh in range(2):
                    rows = slice((2 * p + hh) * HEAD_DIM, (2 * p + hh + 1) * HEAD_DIM)
                    acc_ref[rows, :] = acc_ref[rows, :] + pv[hh * HEAD_DIM:(hh + 1) * HEAD_DIM, hh * t:(hh + 1) * t]
                run = run + jnp.sum(log_keep, axis=0, keepdims=True)
            rests.append(run)
        return jnp.concatenate(rests, axis=1)

    rest = blocks([i], jnp.zeros((1, npair * width), _F32), [key_row < q_lane])

    def cond(carry):
        j, _, top = carry
        return (j >= 0) & (top > SB_UNDERFLOW)

    def body(carry):
        j, rest, _ = carry
        second_ok = jnp.broadcast_to(j >= 1, (t, width))
        rest = blocks([j, jnp.maximum(j - 1, 0)], rest, [None, second_ok])
        return j - 2, rest, jnp.max(rest)

    lax.while_loop(cond, body, (i - 1, rest, jnp.max(rest)))
    o_ref[0] = acc_ref[...].T.astype(o_ref.dtype)


def _sb_call(q_t, k, v_t):
    bsz, seq, width = k.shape
    t = SB_T
    return pl.pallas_call(
        _sb_body,
        out_shape=jax.ShapeDtypeStruct((bsz, seq, width), _BF16),
        grid=(bsz, seq // t),
        in_specs=[
            pl.BlockSpec((1, width, t), lambda b, i: (b, 0, i)),
            pl.BlockSpec((1, seq, width), lambda b, i: (b, 0, 0)),
            pl.BlockSpec((1, seq // t, width, t), lambda b, i: (b, 0, 0, 0)),
        ],
        out_specs=pl.BlockSpec((1, t, width), lambda b, i: (b, i, 0)),
        scratch_shapes=[pltpu.VMEM((width, t), _F32)],
        compiler_params=_cparams(("arbitrary", "arbitrary")),
        name="stick_breaking",
    )(q_t, k, v_t)


def _even_layer(x, mod, w_in, sinks, w_out, ln_g, ln_b, rope_h):
    aq_t, ak, bq, bk, av_t, bv, g = _inproj_even_call(x, mod, _prep_w_even(w_in), *rope_h)
    oa = _moba_call(aq_t, ak, av_t)
    ob = _swa_call(bq, bk, bv, sinks)
    return _outproj_call(oa, ob, g, x, mod, w_out, ln_g, ln_b)


def _odd_layer(x, mod, w_in, kv_g, w_uk, w_uv, w_out, ln_g, ln_b, rope_r, rope_i):
    cqn_t, cqr_t, kvc, ckv_t, iq_t, ik, iw_t, dq_t, dk, dv_t, g = _inproj_odd_call(
        x, mod, _prep_w_odd(w_in), rope_r, rope_i, kv_g)
    oc = _dsa_call(cqn_t, cqr_t, iq_t, iw_t, kvc, ckv_t, ik,
                   w_uk.astype(_BF16), w_uv.transpose(0, 2, 1).astype(_BF16))
    od = _sb_call(dq_t, dk, dv_t)
    return _outproj_call(oc, od, g, x, mod, w_out, ln_g, ln_b)


def kernel(x, c, w_ada, b_ada, w_in_even, sink_logits, w_in_odd, kv_norm_g, w_uk, w_uv, w_out, ln_g, ln_b):
    bsz, seq, d = x.shape
    rope_h = _rope_tables(seq, HEAD_DIM, LANES)
    rope_r = _rope_tables(seq, C_ROPE_DIM, LANES)
    rope_i = _rope_tables(seq, IDX_DIM, LANES)
    mods = _ada_call(c, w_ada, b_ada).reshape(DEPTH, bsz, 3, d)
    for layer in range(DEPTH):
        mod = mods[layer]
        j = layer // 2
        if layer % 2 == 0:
            x = _even_layer(x, mod, w_in_even[j], sink_logits[j], w_out[layer], ln_g[layer], ln_b[layer], rope_h)
        else:
            x = _odd_layer(x, mod, w_in_odd[j], kv_norm_g[j], w_uk[j], w_uv[j], w_out[layer],
                           ln_g[layer], ln_b[layer], rope_r, rope_i)
    return x
```

```python
import jax
import jax.numpy as jnp
from jax import lax
from jax.experimental import pallas as pl
from jax.experimental.pallas import tpu as pltpu

D_MODEL = 1024
DEPTH = 2
HEAD_DIM = 64
ROPE_THETA = 10000.0
LN_EPS = 1e-5
A_HEADS = 8
MOBA_BLOCK = 256
MOBA_TOPK = 3
B_HEADS = 8
B_KV_HEADS = 2
SWA_WINDOW = 128
C_HEADS = 8
C_NOPE_DIM = 64
C_ROPE_DIM = 32
C_V_DIM = 64
C_KV_LATENT = 128
IDX_HEADS = 8
IDX_DIM = 32
DSA_TOPK = 256
D_HEADS = 8
MIX_WIDTH = 1024
EVEN_SIZES = (512, 512, 512, 512, 128, 128, 1024)
ODD_SIZES = (512, 256, 128, 32, 256, 32, 8, 512, 512, 512, 1024)
DEEPNORM_ALPHA = (2 * DEPTH) ** 0.25

LANES = 128
NEG_BIG = -1e30
INT_MIN = -2 ** 31
SB_UNDERFLOW = -104.0

PROJ_TM = 512
DSA_TQ = 256
DSA_CK = 512
SB_T = 128
MOBA_PAIRS = 4
BISECT_STEPS = 20
BISECT_EXTRA = 4
VMEM_LIMIT = 48 * 1024 * 1024

_BF16 = jnp.bfloat16
_F32 = jnp.float32


def _cparams(sem):
    return pltpu.CompilerParams(dimension_semantics=sem, vmem_limit_bytes=VMEM_LIMIT)


def _dot_t(a, b):
    return lax.dot_general(a, b, (((1,), (1,)), ((), ())), preferred_element_type=_F32)


def _dot(a, b):
    return jnp.dot(a, b, preferred_element_type=_F32)


def _lane_iota(shape):
    return lax.broadcasted_iota(jnp.int32, shape, len(shape) - 1)


def _row_iota(shape):
    return lax.broadcasted_iota(jnp.int32, shape, len(shape) - 2)


def _head_pair_rhs(q_t, head_dim):
    frow = _row_iota(q_t.shape)
    return jnp.concatenate(
        [jnp.where((frow // head_dim) == hh, q_t, jnp.zeros_like(q_t)) for hh in range(2)], axis=1)


def _ada_body(c_ref, w_ref, b_ref, o_ref):
    c = c_ref[...]
    cond = c * jax.nn.sigmoid(c)
    o_ref[0] = _dot(cond, w_ref[0]) + b_ref[0]


def _ada_call(c, w_ada, b_ada):
    depth, d, n3 = w_ada.shape
    bsz = c.shape[0]
    nb = n3 // d
    return pl.pallas_call(
        _ada_body,
        out_shape=jax.ShapeDtypeStruct((depth, bsz, n3), _F32),
        grid=(depth, nb),
        in_specs=[
            pl.BlockSpec((bsz, d), lambda l, j: (0, 0)),
            pl.BlockSpec((1, d, d), lambda l, j: (l, 0, j)),
            pl.BlockSpec((1, 1, d), lambda l, j: (l, 0, j)),
        ],
        out_specs=pl.BlockSpec((1, bsz, d), lambda l, j: (l, 0, j)),
        compiler_params=_cparams(("arbitrary", "arbitrary")),
        name="ada_mod",
    )(c, w_ada, b_ada.reshape(depth, 1, n3))


def _rope_tables(seq, dim, period_lanes):
    inv = 1.0 / (ROPE_THETA ** (jnp.arange(0, dim, 2, dtype=_F32) / dim))
    ang = jnp.arange(seq, dtype=_F32)[:, None] * inv[None, :]
    cos, sin = jnp.cos(ang), jnp.sin(ang)
    cos_h = jnp.concatenate([cos, cos], axis=1)
    sin_h = jnp.concatenate([-sin, sin], axis=1)
    reps = period_lanes // dim
    return jnp.tile(cos_h, (1, reps)), jnp.tile(sin_h, (1, reps))


def _rope_piece(x, cos, sin_signed, dim):
    half = dim // 2
    first = (_lane_iota(x.shape) % dim) < half
    partner = jnp.where(first, pltpu.roll(x, LANES - half, 1), pltpu.roll(x, half, 1))
    return x * cos + partner * sin_signed


def _modulated(x_ref, mod_ref):
    x = x_ref[0]
    shift = mod_ref[0, 0:1, :]
    scale = mod_ref[0, 1:2, :]
    return (x * (1.0 + scale) + shift).astype(_BF16)


def _silu_gate(h, w_ref, col0, g_ref):
    for p in range(2):
        gate = _dot(h, w_ref[:, col0 + 512 * p:col0 + 512 * (p + 1)])
        g_ref[0, :, 512 * p:512 * (p + 1)] = (gate * jax.nn.sigmoid(gate)).astype(_BF16)


def _store_blocks_t(acc, out_ref, blk):
    tm, n = acc.shape
    for r in range(tm // blk):
        for p in range(n // LANES):
            out_ref[0, r, p * LANES:(p + 1) * LANES, :] = (
                acc[r * blk:(r + 1) * blk, p * LANES:(p + 1) * LANES].T.astype(_BF16))


EVEN_COLS = 3584


def _prep_w_even(w):
    offs = [0]
    for n in EVEN_SIZES:
        offs.append(offs[-1] + n)
    w = w.astype(_BF16)
    aq, ak, av, bq, bk, bv, gate = [w[:, offs[i]:offs[i + 1]] for i in range(len(EVEN_SIZES))]

    def dup(t):
        parts = []
        for g in range(B_KV_HEADS):
            blk = t[:, g * HEAD_DIM:(g + 1) * HEAD_DIM]
            parts += [blk, blk]
        return jnp.concatenate(parts, axis=1)

    return jnp.concatenate([aq, ak, bq, dup(bk), av, dup(bv), gate], axis=1)


def _inproj_even_body(x_ref, mod_ref, w_ref, cos_ref, sin_ref,
                      aqt_ref, ak_ref, bq_ref, bk_ref, avt_ref, bv_ref, g_ref):
    h = _modulated(x_ref, mod_ref)
    cos = cos_ref[...]
    sin = sin_ref[...]
    q_scale = HEAD_DIM ** -0.5

    def roped(col0, ncols, out_ref, scale, transposed=False):
        acc = _dot(h, w_ref[:, col0:col0 + ncols])
        for p in range(ncols // LANES):
            piece = _rope_piece(acc[:, p * LANES:(p + 1) * LANES], cos, sin, HEAD_DIM)
            if scale != 1.0:
                piece = piece * scale
            if transposed:
                out_ref[0, p * LANES:(p + 1) * LANES, :] = piece.T.astype(_BF16)
            else:
                out_ref[0, :, p * LANES:(p + 1) * LANES] = piece.astype(_BF16)

    roped(0, 512, aqt_ref, q_scale, transposed=True)
    roped(512, 512, ak_ref, 1.0)
    roped(1024, 512, bq_ref, q_scale)
    roped(1536, 256, bk_ref, 1.0)
    _store_blocks_t(_dot(h, w_ref[:, 1792:2304]), avt_ref, MOBA_BLOCK)
    bv_ref[0] = _dot(h, w_ref[:, 2304:2560]).astype(_BF16)
    _silu_gate(h, w_ref, 2560, g_ref)


def _inproj_even_call(x, mod, w, cos, sin):
    bsz, seq, d = x.shape
    tm = PROJ_TM
    row = lambda n: pl.BlockSpec((1, tm, n), lambda b, i: (b, i, 0))
    tok = lambda n: jax.ShapeDtypeStruct((bsz, seq, n), _BF16)
    per_tile = tm // MOBA_BLOCK
    return pl.pallas_call(
        _inproj_even_body,
        out_shape=[
            jax.ShapeDtypeStruct((bsz, 512, seq), _BF16), tok(512), tok(512), tok(256),
            jax.ShapeDtypeStruct((bsz, seq // MOBA_BLOCK, 512, MOBA_BLOCK), _BF16), tok(256), tok(1024),
        ],
        grid=(bsz, seq // tm),
        in_specs=[
            row(d),
            pl.BlockSpec((1, 3, d), lambda b, i: (b, 0, 0)),
            pl.BlockSpec((d, EVEN_COLS), lambda b, i: (0, 0)),
            pl.BlockSpec((tm, LANES), lambda b, i: (i, 0)),
            pl.BlockSpec((tm, LANES), lambda b, i: (i, 0)),
        ],
        out_specs=[
            pl.BlockSpec((1, 512, tm), lambda b, i: (b, 0, i)), row(512), row(512), row(256),
            pl.BlockSpec((1, per_tile, 512, MOBA_BLOCK), lambda b, i: (b, i, 0, 0)), row(256), row(1024),
        ],
        compiler_params=_cparams(("arbitrary", "arbitrary")),
        name="inproj_even",
    )(x, mod, w, cos, sin)


def _moba_body(qt_ref, k_ref, vt_ref, o_ref, km_ref, selb_ref, acc_ref):
    qi = pl.program_id(2)
    tq = MOBA_BLOCK
    nblk = km_ref.shape[0]
    npair = qt_ref.shape[1] // LANES
    pw = 2 * tq
    width = npair * pw

    @pl.when(qi == 0)
    def _():
        for r in range(nblk):
            blk_k = k_ref[0, r * tq:(r + 1) * tq, :].astype(_F32)
            km_ref[r:r + 1, :] = jnp.mean(blk_k, axis=0, keepdims=True)

    qcats = [_head_pair_rhs(qt_ref[0, p * LANES:(p + 1) * LANES, :], HEAD_DIM) for p in range(npair)]

    blk = _row_iota((nblk, width))
    blkf = blk.astype(_F32)
    valid = blk < qi
    gate = jnp.concatenate([_dot(km_ref[:, p * LANES:(p + 1) * LANES].astype(_BF16), qcats[p])
                            for p in range(npair)], axis=1)
    gate = jnp.where(valid, gate, -jnp.inf)
    sel = jnp.zeros((nblk, width), _F32)
    for _ in range(MOBA_TOPK):
        top = jnp.max(gate, axis=0, keepdims=True)
        first = jnp.min(jnp.where(gate == top, blkf, float(nblk)), axis=0, keepdims=True)
        pick = blkf == first
        sel = jnp.where(pick, 1.0, sel)
        gate = jnp.where(pick, -jnp.inf, gate)
    selb_ref[...] = jnp.where(valid & (sel > 0.0), 0.0, NEG_BIG)

    acc_ref[...] = jnp.zeros_like(acc_ref)

    def step(carry, j, bias_of_pair):
        m, l = carry
        kj = k_ref[0, pl.ds(pl.multiple_of(j * tq, tq), tq), :]
        ss = [_dot(kj[:, p * LANES:(p + 1) * LANES], qcats[p]) + bias_of_pair(p) for p in range(npair)]
        ms, ls = [], []
        for p in range(npair):
            pcols = slice(p * pw, (p + 1) * pw)
            m_new = jnp.maximum(m[:, pcols], jnp.max(ss[p], axis=0, keepdims=True))
            alpha = jnp.exp(m[:, pcols] - m_new)
            e = jnp.exp(ss[p] - m_new)
            ls.append(alpha * l[:, pcols] + jnp.sum(e, axis=0, keepdims=True))
            ms.append(m_new)
            e = e.astype(_BF16)
            for hh in range(2):
                rows = slice((2 * p + hh) * HEAD_DIM, (2 * p + hh + 1) * HEAD_DIM)
                cols = slice(hh * tq, (hh + 1) * tq)
                acc_ref[rows, :] = alpha[:, cols] * acc_ref[rows, :] + _dot(vt_ref[0, j, rows, :], e[:, cols])
        return jnp.concatenate(ms, axis=1), jnp.concatenate(ls, axis=1)

    def past_body(j, carry):
        row_bias = selb_ref[pl.ds(j, 1), :]
        return step(carry, j, lambda p: row_bias[:, p * pw:(p + 1) * pw])

    carry = (jnp.full((1, width), NEG_BIG, _F32), jnp.zeros((1, width), _F32))
    carry = lax.fori_loop(0, qi, past_body, carry)
    causal = jnp.where(_row_iota((tq, pw)) <= _lane_iota((tq, pw)) % tq, 0.0, NEG_BIG)
    _, l = step(carry, qi, lambda p: causal)
    inv = 1.0 / l
    out_t = jnp.concatenate([acc_ref[h * HEAD_DIM:(h + 1) * HEAD_DIM, :] * inv[:, h * tq:(h + 1) * tq]
                             for h in range(2 * npair)], axis=0)
    o_ref[0] = out_t.T.astype(o_ref.dtype)


def _moba_call(q_t, k, v_t):
    bsz, seq, width = k.shape
    tq = MOBA_BLOCK
    nblk = seq // tq
    gw = MOBA_PAIRS * LANES
    return pl.pallas_call(
        _moba_body,
        out_shape=jax.ShapeDtypeStruct((bsz, seq, width), _BF16),
        grid=(bsz, width // gw, nblk),
        in_specs=[
            pl.BlockSpec((1, gw, tq), lambda b, h, i: (b, h, i)),
            pl.BlockSpec((1, seq, gw), lambda b, h, i: (b, 0, h)),
            pl.BlockSpec((1, nblk, gw, tq), lambda b, h, i: (b, 0, h, 0)),
        ],
        out_specs=pl.BlockSpec((1, tq, gw), lambda b, h, i: (b, i, h)),
        scratch_shapes=[
            pltpu.VMEM((nblk, gw), _F32),
            pltpu.VMEM((nblk, MOBA_PAIRS * 2 * tq), _F32),
            pltpu.VMEM((gw, tq), _F32),
        ],
        compiler_params=_cparams(("arbitrary", "arbitrary", "arbitrary")),
        name="moba",
    )(q_t, k, v_t)


def _swa_body(sink_ref, q_ref, kp_ref, kc_ref, vp_ref, vc_ref, o_ref):
    i = pl.program_id(1)
    w = SWA_WINDOW
    lane = _lane_iota((1, LANES))
    row = _row_iota((w, 2 * w))
    col = _lane_iota((w, 2 * w))
    mask = (col > row) & (col <= row + w) & ((col >= w) | (i > 0))
    pairs_per_kv = (B_HEADS // B_KV_HEADS) // 2
    ks = [jnp.concatenate([kp_ref[0, :, g * LANES:(g + 1) * LANES],
                           kc_ref[0, :, g * LANES:(g + 1) * LANES]], axis=0) for g in range(B_KV_HEADS)]
    vs = [jnp.concatenate([vp_ref[0, :, g * LANES:(g + 1) * LANES],
                           vc_ref[0, :, g * LANES:(g + 1) * LANES]], axis=0) for g in range(B_KV_HEADS)]
    ss = []
    for h in range(B_HEADS):
        qp = q_ref[0, :, (h // 2) * LANES:(h // 2 + 1) * LANES]
        qh = jnp.where((lane // HEAD_DIM) == (h % 2), qp, jnp.zeros_like(qp))
        ss.append(_dot_t(qh, ks[(h // 2) // pairs_per_kv]))
    outs = []
    for h in range(B_HEADS):
        sink = sink_ref[h]
        s = jnp.where(mask, ss[h], -jnp.inf)
        m = jnp.maximum(jnp.max(s, axis=1, keepdims=True), sink)
        e = jnp.exp(s - m)
        l = jnp.sum(e, axis=1, keepdims=True) + jnp.exp(sink - m)
        outs.append(_dot(e.astype(_BF16), vs[(h // 2) // pairs_per_kv]) / l)
    for p in range(B_HEADS // 2):
        o_ref[0, :, p * LANES:(p + 1) * LANES] = jnp.where(
            lane < HEAD_DIM, outs[2 * p], outs[2 * p + 1]).astype(o_ref.dtype)


def _swa_call(q, kdup, vdup, sinks):
    bsz, seq, width = q.shape
    w = SWA_WINDOW
    kvw = kdup.shape[2]
    prev = pl.BlockSpec((1, w, kvw), lambda b, i: (b, jnp.maximum(i - 1, 0), 0))
    cur = pl.BlockSpec((1, w, kvw), lambda b, i: (b, i, 0))
    return pl.pallas_call(
        _swa_body,
        out_shape=jax.ShapeDtypeStruct((bsz, seq, width), _BF16),
        grid=(bsz, seq // w),
        in_specs=[
            pl.BlockSpec(memory_space=pltpu.SMEM),
            pl.BlockSpec((1, w, width), lambda b, i: (b, i, 0)),
            prev, cur, prev, cur,
        ],
        out_specs=pl.BlockSpec((1, w, width), lambda b, i: (b, i, 0)),
        compiler_params=_cparams(("arbitrary", "arbitrary")),
        name="swa_sink",
    )(sinks, q, kdup, kdup, vdup, vdup)


def _outproj_body(o1_ref, o2_ref, g_ref, x_ref, mod_ref, w_ref, lng_ref, lnb_ref, y_ref):
    half = o1_ref.shape[2]
    a1 = o1_ref[0] * g_ref[0, :, :half]
    a2 = o2_ref[0] * g_ref[0, :, half:]
    y = _dot(a1, w_ref[:half, :]) + _dot(a2, w_ref[half:, :])
    gate = mod_ref[0, 2:3, :]
    z = DEEPNORM_ALPHA * x_ref[0] + (1.0 + gate) * y
    mu = jnp.mean(z, axis=1, keepdims=True)
    zc = z - mu
    var = jnp.mean(zc * zc, axis=1, keepdims=True)
    y_ref[0] = zc * lax.rsqrt(var + LN_EPS) * lng_ref[...] + lnb_ref[...]


def _outproj_call(o1, o2, g, x, mod, w_out, ln_g, ln_b):
    bsz, seq, d = x.shape
    tm = PROJ_TM
    half = o1.shape[2]
    row = lambda n: pl.BlockSpec((1, tm, n), lambda b, i: (b, i, 0))
    return pl.pallas_call(
        _outproj_body,
        out_shape=jax.ShapeDtypeStruct((bsz, seq, d), _F32),
        grid=(bsz, seq // tm),
        in_specs=[
            row(half), row(half), row(2 * half), row(d),
            pl.BlockSpec((1, 3, d), lambda b, i: (b, 0, 0)),
            pl.BlockSpec((2 * half, d), lambda b, i: (0, 0)),
            pl.BlockSpec((1, d), lambda b, i: (0, 0)),
            pl.BlockSpec((1, d), lambda b, i: (0, 0)),
        ],
        out_specs=row(d),
        compiler_params=_cparams(("arbitrary", "arbitrary")),
        name="outproj_deepnorm",
    )(o1, o2, g, x, mod, w_out.astype(_BF16), ln_g.reshape(1, d), ln_b.reshape(1, d))


ODD_COLS = 4096


def _prep_w_odd(w):
    offs = [0]
    for n in ODD_SIZES:
        offs.append(offs[-1] + n)
    w = w.astype(_BF16)
    cqn, cqr, ckv, ckr, iq, ik, iw, dq, dk, dv, gate = [w[:, offs[i]:offs[i + 1]] for i in range(len(ODD_SIZES))]
    iw_blk = jnp.concatenate([iw, jnp.zeros((w.shape[0], LANES - IDX_HEADS), w.dtype)], axis=1)
    return jnp.concatenate([cqn, cqr, ckv, jnp.tile(ckr, (1, 4)), iq, jnp.tile(ik, (1, 4)), iw_blk,
                            dq, dk, dv, gate], axis=1)


def _inproj_odd_body(x_ref, mod_ref, w_ref, cosr_ref, sinr_ref, cosi_ref, sini_ref, kvg_ref,
                     cqn_ref, cqr_ref, kvc_ref, ckvt_ref, iq_ref, ik_ref, iw_ref,
                     dqt_ref, dk_ref, dvt_ref, g_ref):
    h = _modulated(x_ref, mod_ref)
    cosr, sinr = cosr_ref[...], sinr_ref[...]
    cosi, sini = cosi_ref[...], sini_ref[...]

    acc = _dot(h, w_ref[:, 0:512])
    for p in range(4):
        cqn_ref[0, p * LANES:(p + 1) * LANES, :] = acc[:, p * LANES:(p + 1) * LANES].T.astype(_BF16)
    acc = _dot(h, w_ref[:, 512:1024])
    for p in range(2):
        piece = _rope_piece(acc[:, p * LANES:(p + 1) * LANES], cosr, sinr, C_ROPE_DIM)
        cqr_ref[0, p * LANES:(p + 1) * LANES, :] = piece.T.astype(_BF16)
    ckv = acc[:, 256:384]
    ckv = ckv * lax.rsqrt(jnp.mean(ckv * ckv, axis=1, keepdims=True) + LN_EPS) * kvg_ref[...]
    kvc_ref[0, :, 0:LANES] = ckv.astype(_BF16)
    ckvt_ref[0, 0] = ckv.T.astype(_BF16)
    kvc_ref[0, :, LANES:2 * LANES] = _rope_piece(acc[:, 384:512], cosr, sinr, C_ROPE_DIM).astype(_BF16)
    acc = _dot(h, w_ref[:, 1024:1536])
    for p in range(2):
        piece = _rope_piece(acc[:, p * LANES:(p + 1) * LANES], cosi, sini, IDX_DIM)
        iq_ref[0, p * LANES:(p + 1) * LANES, :] = piece.T.astype(_BF16)
    ik_ref[0] = _rope_piece(acc[:, 256:384], cosi, sini, IDX_DIM).astype(_BF16)
    iw_ref[0] = acc[:, 384:512].T[0:IDX_HEADS, :]
    acc = _dot(h, w_ref[:, 1536:2048]) * (HEAD_DIM ** -0.5)
    for p in range(4):
        dqt_ref[0, p * LANES:(p + 1) * LANES, :] = acc[:, p * LANES:(p + 1) * LANES].T.astype(_BF16)
    dk_ref[0] = _dot(h, w_ref[:, 2048:2560]).astype(_BF16)
    _store_blocks_t(_dot(h, w_ref[:, 2560:3072]), dvt_ref, SB_T)
    _silu_gate(h, w_ref, 3072, g_ref)


def _inproj_odd_call(x, mod, w, rope_r, rope_i, kv_g):
    bsz, seq, d = x.shape
    tm = DSA_CK
    row = lambda n: pl.BlockSpec((1, tm, n), lambda b, i: (b, i, 0))
    col = lambda n: pl.BlockSpec((1, n, tm), lambda b, i: (b, 0, i))
    tab = pl.BlockSpec((tm, LANES), lambda b, i: (i, 0))
    tok = lambda n, dt: jax.ShapeDtypeStruct((bsz, seq, n), dt)
    feat = lambda n, dt: jax.ShapeDtypeStruct((bsz, n, seq), dt)
    return pl.pallas_call(
        _inproj_odd_body,
        out_shape=[
            feat(512, _BF16), feat(256, _BF16), tok(256, _BF16),
            jax.ShapeDtypeStruct((bsz, seq // tm, C_KV_LATENT, tm), _BF16),
            feat(256, _BF16), tok(128, _BF16), feat(IDX_HEADS, _F32),
            feat(512, _BF16), tok(512, _BF16),
            jax.ShapeDtypeStruct((bsz, seq // SB_T, 512, SB_T), _BF16), tok(1024, _BF16),
        ],
        grid=(bsz, seq // tm),
        in_specs=[
            row(d),
            pl.BlockSpec((1, 3, d), lambda b, i: (b, 0, 0)),
            pl.BlockSpec((d, ODD_COLS), lambda b, i: (0, 0)),
            tab, tab, tab, tab,
            pl.BlockSpec((1, LANES), lambda b, i: (0, 0)),
        ],
        out_specs=[
            col(512), col(256), row(256),
            pl.BlockSpec((1, 1, C_KV_LATENT, tm), lambda b, i: (b, i, 0, 0)),
            col(256), row(128), col(IDX_HEADS),
            col(512), row(512),
            pl.BlockSpec((1, tm // SB_T, 512, SB_T), lambda b, i: (b, i, 0, 0)), row(1024),
        ],
        compiler_params=_cparams(("arbitrary", "arbitrary")),
        name="inproj_odd",
    )(x, mod, w, rope_r[0], rope_r[1], rope_i[0], rope_i[1], kv_g.reshape(1, LANES))


def _dsa_body(cqn_ref, cqr_ref, iq_ref, iw_ref, kvc_ref, ckvt_ref, ik_ref, wuk_ref, wuvt_ref, o_ref,
              qct_ref, iqt_ref, sc_ref, acc_ref):
    i = pl.program_id(1)
    tq, ck = DSA_TQ, DSA_CK
    seq = kvc_ref.shape[1]
    n_sel = float(min(DSA_TOPK, seq // 4))
    nch = i // (ck // tq) + 1
    att_scale = (C_NOPE_DIM + C_ROPE_DIM) ** -0.5
    idx_scale = (IDX_DIM * IDX_HEADS) ** -0.5
    hq = C_HEADS * tq

    row128 = _row_iota((LANES, tq))
    for h in range(C_HEADS):
        cols = slice(h * tq, (h + 1) * tq)
        qn_t = cqn_ref[0, h * C_NOPE_DIM:(h + 1) * C_NOPE_DIM, :]
        qlat_t = _dot(wuk_ref[h], qn_t) * att_scale
        qct_ref[0:C_KV_LATENT, cols] = qlat_t.astype(_BF16)
        qr_t = cqr_ref[0, h * C_ROPE_DIM:(h + 1) * C_ROPE_DIM, :].astype(_F32) * att_scale
        qct_ref[C_KV_LATENT:C_KV_LATENT + C_ROPE_DIM, cols] = qr_t.astype(_BF16)
        qct_ref[C_KV_LATENT + C_ROPE_DIM:, cols] = jnp.zeros((LANES - C_ROPE_DIM, tq), _BF16)
        blk4 = h // 4
        iq_t = iq_ref[0, blk4 * LANES:(blk4 + 1) * LANES, :]
        iqt_ref[:, cols] = jnp.where((row128 // IDX_DIM) == (h % 4), iq_t, jnp.zeros_like(iq_t))

    t_pos = i * tq + _lane_iota((ck, tq))
    key_row = _row_iota((ck, tq))

    def score_body(c, carry):
        vmax, vmin = carry
        off = pl.multiple_of(c * ck, ck)
        rel = _dot(ik_ref[0, pl.ds(off, ck), :], iqt_ref[...])
        score = jnp.zeros((ck, tq), _F32)
        for h in range(IDX_HEADS):
            score = score + jnp.maximum(rel[:, h * tq:(h + 1) * tq], 0.0) * iw_ref[0, h:h + 1, :]
        score = score * idx_scale
        low = jnp.where(off + key_row <= t_pos, score, -jnp.inf)
        sc_ref[c] = low
        vmax = jnp.maximum(vmax, jnp.max(low, axis=0, keepdims=True))
        vmin = jnp.minimum(vmin, jnp.min(score, axis=0, keepdims=True))
        return vmax, vmin

    vmax, vmin = lax.fori_loop(0, nch, score_body,
                               (jnp.full((1, tq), -jnp.inf, _F32), jnp.full((1, tq), jnp.inf, _F32)))

    def count(pred):
        rows_acc = 8 * (8 * LANES // tq)

        slab_row = _row_iota((rows_acc, tq))

        def body(c, acc):
            for r in range(ck // rows_acc):
                rows = slice(r * rows_acc, (r + 1) * rows_acc)
                acc = acc + jnp.where(pred(sc_ref[c, rows, :], c * ck + r * rows_acc + slab_row), 1.0, 0.0)
            return acc

        acc = lax.fori_loop(0, nch, body, jnp.zeros((rows_acc, tq), _F32))
        return jnp.sum(acc, axis=0, keepdims=True)

    def ordered_float(u):
        key = u ^ INT_MIN
        return pltpu.bitcast(key ^ ((key >> 31) & 0x7FFFFFFF), _F32)

    n_valid = (i * tq + _lane_iota((1, tq)) + 1).astype(_F32)
    few = n_valid < n_sel
    c_ge0 = count(lambda sc, pos: sc >= 0.0)
    c_gt0 = count(lambda sc, pos: sc > 0.0)
    zero_tie = (c_gt0 < n_sel) & (c_ge0 >= n_sel)
    above = c_gt0 >= n_sel

    def bisect_body(_, carry):
        lo, hi, cnt_lo = carry
        mid = lo + 0.5 * (hi - lo)
        cnt = count(lambda sc, pos: sc >= mid)
        ok = cnt >= n_sel
        return jnp.where(ok, mid, lo), jnp.where(ok, hi, mid), jnp.where(ok, cnt, cnt_lo)

    def all_resolved(cnt_lo):
        done = few | zero_tie | (cnt_lo == n_sel)
        return jnp.min(jnp.where(done, 1.0, 0.0)) > 0.0

    state = (jnp.where(above, 0.0, vmin), jnp.where(above, vmax, 0.0), jnp.where(above, c_ge0, n_valid))
    state = lax.fori_loop(0, BISECT_STEPS, bisect_body, state)
    state = lax.cond(all_resolved(state[2]), lambda s: s,
                     lambda s: lax.fori_loop(0, BISECT_EXTRA, bisect_body, s), state)
    lo, _, cnt_lo = state

    def exact_search():
        def bit_body(b, carry):
            t, cnt_t = carry
            cand = t | jnp.left_shift(jnp.int32(1), 31 - b)
            cand_f = ordered_float(cand)
            cnt = count(lambda sc, pos: sc >= cand_f)
            ok = cnt >= n_sel
            return jnp.where(ok, cand, t), jnp.where(ok, cnt, cnt_t)

        t0 = jnp.zeros((1, tq), jnp.int32)
        t, cnt_t = lax.fori_loop(0, 32, bit_body, (t0, jnp.full((1, tq), float(seq), _F32)))
        return jnp.where(t == 0, -jnp.inf, ordered_float(t)), cnt_t

    thr, cnt_thr = lax.cond(
        all_resolved(cnt_lo),
        lambda: (jnp.where(few, -jnp.inf, jnp.where(zero_tie, 0.0, lo)), jnp.where(zero_tie, c_ge0, cnt_lo)),
        exact_search)

    excess = jnp.max(jnp.where((cnt_thr > n_sel) & (thr > -jnp.inf), 1.0, 0.0)) > 0.0

    def tie_limit():
        room = n_sel - count(lambda sc, pos: sc > thr)

        def lim_body(b, lim):
            cand = lim | jnp.left_shift(jnp.int32(1), 12 - b)
            cnt = count(lambda sc, pos: (sc == thr) & (pos < cand))
            return jnp.where(cnt <= room, cand, lim)

        return lax.fori_loop(0, 13, lim_body, jnp.zeros((1, tq), jnp.int32))

    limit = lax.cond(excess, tie_limit, lambda: jnp.full((1, tq), 2 * seq, jnp.int32))

    def bias_body(c, _):
        sc = sc_ref[c]
        pos = c * ck + key_row
        chosen = ((sc > thr) | ((sc == thr) & (pos < limit))) & (pos <= t_pos)
        sc_ref[c] = jnp.where(chosen, 0.0, NEG_BIG)
        return 0

    lax.fori_loop(0, nch, bias_body, 0)

    acc_ref[...] = jnp.zeros_like(acc_ref)
    grp = 2 * tq
    gcols = [slice(g * grp, (g + 1) * grp) for g in range(hq // grp)]

    def att_body(c, carry):
        off = pl.multiple_of(c * ck, ck)
        kv = kvc_ref[0, pl.ds(off, ck), :]
        bias = sc_ref[c]
        bias2 = jnp.concatenate([bias, bias], axis=1)
        ss = [_dot(kv, qct_ref[:, cols]) + bias2 for cols in gcols]
        ms, ls, es, alphas = [], [], [], []
        for g, cols in enumerate(gcols):
            m, l = carry[0][:, cols], carry[1][:, cols]
            m_new = jnp.maximum(m, jnp.max(ss[g], axis=0, keepdims=True))
            alpha = jnp.exp(m - m_new)
            e = jnp.exp(ss[g] - m_new)
            ls.append(alpha * l + jnp.sum(e, axis=0, keepdims=True))
            ms.append(m_new)
            alphas.append(alpha)
            es.append(e.astype(_BF16))
        kv_t = ckvt_ref[0, c]
        for g, cols in enumerate(gcols):
            acc_ref[:, cols] = alphas[g] * acc_ref[:, cols] + _dot(kv_t, es[g])
        return jnp.concatenate(ms, axis=1), jnp.concatenate(ls, axis=1)

    m0 = jnp.full((1, hq), NEG_BIG, _F32)
    l0 = jnp.zeros((1, hq), _F32)
    _, l = lax.fori_loop(0, nch, att_body, (m0, l0))
    o_lat_t = (acc_ref[...] / l).astype(_BF16)
    outs = [_dot(wuvt_ref[h], o_lat_t[:, h * tq:(h + 1) * tq]) for h in range(C_HEADS)]
    o_ref[0] = jnp.concatenate(outs, axis=0).T.astype(o_ref.dtype)


def _dsa_call(cqn_t, cqr_t, iq_t, iw_t, kvc, ckv_t, ik, w_uk, wuv_t):
    bsz, seq, _ = kvc.shape
    tq = DSA_TQ
    qcol = lambda n: pl.BlockSpec((1, n, tq), lambda b, i: (b, 0, i))
    full = lambda n: pl.BlockSpec((1, seq, n), lambda b, i: (b, 0, 0))
    whole = lambda a: pl.BlockSpec(a.shape, lambda b, i: (0,) * a.ndim)
    width = C_HEADS * C_V_DIM
    return pl.pallas_call(
        _dsa_body,
        out_shape=jax.ShapeDtypeStruct((bsz, seq, width), _BF16),
        grid=(bsz, seq // tq),
        in_specs=[
            qcol(512), qcol(256), qcol(256), qcol(IDX_HEADS), full(256),
            pl.BlockSpec((1,) + ckv_t.shape[1:], lambda b, i: (b, 0, 0, 0)),
            full(128), whole(w_uk), whole(wuv_t),
        ],
        out_specs=pl.BlockSpec((1, tq, width), lambda b, i: (b, i, 0)),
        scratch_shapes=[
            pltpu.VMEM((2 * LANES, C_HEADS * tq), _BF16),
            pltpu.VMEM((LANES, IDX_HEADS * tq), _BF16),
            pltpu.VMEM((seq // DSA_CK, DSA_CK, tq), _F32),
            pltpu.VMEM((C_KV_LATENT, C_HEADS * tq), _F32),
        ],
        compiler_params=_cparams(("arbitrary", "arbitrary")),
        name="dsa",
    )(cqn_t, cqr_t, iq_t, iw_t, kvc, ckv_t, ik, w_uk, wuv_t)


def _sb_body(qt_ref, k_ref, vt_ref, o_ref, acc_ref):
    i = pl.program_id(1)
    t = SB_T
    npair = D_HEADS // 2
    width = 2 * t
    qcats = [_head_pair_rhs(qt_ref[0, p * LANES:(p + 1) * LANES, :], HEAD_DIM) for p in range(npair)]
    key_row = _row_iota((t, width))
    q_lane = _lane_iota((t, width)) % t
    later = jnp.where(_lane_iota((t, t)) > _row_iota((t, t)), 1.0, 0.0).astype(_BF16)
    later2 = jnp.concatenate([later, later], axis=1)
    acc_ref[...] = jnp.zeros_like(acc_ref)

    def blocks(js, rest, masks):
        ks = [k_ref[0, pl.ds(pl.multiple_of(j * t, t), t), :] for j in js]
        zs = [[_dot(kj[:, p * LANES:(p + 1) * LANES], qcats[p]) for p in range(npair)] for kj in ks]
        rests = []
        for p in range(npair):
            run = rest[:, p * width:(p + 1) * width]
            for n, j in enumerate(js):
                z = zs[n][p]
                log_keep = -(jnp.maximum(z, 0.0) + jnp.log(1.0 + jnp.exp(-jnp.abs(z))))
                log_beta = z + log_keep
                if masks[n] is not None:
                    log_keep = jnp.where(masks[n], log_keep, 0.0)
                hi = log_keep.astype(_BF16)
                lo = (log_keep - hi.astype(_F32)).astype(_BF16)
                a = jnp.exp(log_beta + (_dot(later2, jnp.concatenate([hi, lo], axis=0)) + run))
                if masks[n] is not None:
                    a = jnp.where(masks[n], a, 0.0)
                prows = slice(p * LANES, (p + 1) * LANES)
                pv = _dot(vt_ref[0, j, prows, :], a.astype(_BF16))
                for hh in range(2):
                    rows = slice((2 * p + hh) * HEAD_DIM, (2 * p + hh + 1) * HEAD_DIM)
                    acc_ref[rows, :] = acc_ref[rows, :] + pv[hh * HEAD_DIM:(hh + 1) * HEAD_DIM, hh * t:(hh + 1) * t]
                run = run + jnp.sum(log_keep, axis=0, keepdims=True)
            rests.append(run)
        return jnp.concatenate(rests, axis=1)

    rest = blocks([i], jnp.zeros((1, npair * width), _F32), [key_row < q_lane])

    def cond(carry):
        j, _, top = carry
        return (j >= 0) & (top > SB_UNDERFLOW)

    def body(carry):
        j, rest, _ = carry
        second_ok = jnp.broadcast_to(j >= 1, (t, width))
        rest = blocks([j, jnp.maximum(j - 1, 0)], rest, [None, second_ok])
        return j - 2, rest, jnp.max(rest)

    lax.while_loop(cond, body, (i - 1, rest, jnp.max(rest)))
    o_ref[0] = acc_ref[...].T.astype(o_ref.dtype)


def _sb_call(q_t, k, v_t):
    bsz, seq, width = k.shape
    t = SB_T
    return pl.pallas_call(
        _sb_body,
        out_shape=jax.ShapeDtypeStruct((bsz, seq, width), _BF16),
        grid=(bsz, seq // t),
        in_specs=[
            pl.BlockSpec((1, width, t), lambda b, i: (b, 0, i)),
            pl.BlockSpec((1, seq, width), lambda b, i: (b, 0, 0)),
            pl.BlockSpec((1, seq // t, width, t), lambda b, i: (b, 0, 0, 0)),
        ],
        out_specs=pl.BlockSpec((1, t, width), lambda b, i: (b, i, 0)),
        scratch_shapes=[pltpu.VMEM((width, t), _F32)],
        compiler_params=_cparams(("arbitrary", "arbitrary")),
        name="stick_breaking",
    )(q_t, k, v_t)


def _even_layer(x, mod, w_in, sinks, w_out, ln_g, ln_b, rope_h):
    aq_t, ak, bq, bk, av_t, bv, g = _inproj_even_call(x, mod, _prep_w_even(w_in), *rope_h)
    oa = _moba_call(aq_t, ak, av_t)
    ob = _swa_call(bq, bk, bv, sinks)
    return _outproj_call(oa, ob, g, x, mod, w_out, ln_g, ln_b)


def _odd_layer(x, mod, w_in, kv_g, w_uk, w_uv, w_out, ln_g, ln_b, rope_r, rope_i):
    cqn_t, cqr_t, kvc, ckv_t, iq_t, ik, iw_t, dq_t, dk, dv_t, g = _inproj_odd_call(
        x, mod, _prep_w_odd(w_in), rope_r, rope_i, kv_g)
    oc = _dsa_call(cqn_t, cqr_t, iq_t, iw_t, kvc, ckv_t, ik,
                   w_uk.astype(_BF16), w_uv.transpose(0, 2, 1).astype(_BF16))
    od = _sb_call(dq_t, dk, dv_t)
    return _outproj_call(oc, od, g, x, mod, w_out, ln_g, ln_b)


def kernel(x, c, w_ada, b_ada, w_in_even, sink_logits, w_in_odd, kv_norm_g, w_uk, w_uv, w_out, ln_g, ln_b):
    bsz, seq, d = x.shape
    rope_h = _rope_tables(seq, HEAD_DIM, LANES)
    rope_r = _rope_tables(seq, C_ROPE_DIM, LANES)
    rope_i = _rope_tables(seq, IDX_DIM, LANES)
    mods = _ada_call(c, w_ada, b_ada).reshape(DEPTH, bsz, 3, d)
    for layer in range(DEPTH):
        mod = mods[layer]
        j = layer // 2
        if layer % 2 == 0:
            x = _even_layer(x, mod, w_in_even[j], sink_logits[j], w_out[layer], ln_g[layer], ln_b[layer], rope_h)
        else:
            x = _odd_layer(x, mod, w_in_odd[j], kv_norm_g[j], w_uk[j], w_uv[j], w_out[layer],
                           ln_g[layer], ln_b[layer], rope_r, rope_i)
    return x
```

```python
import jax
import jax.numpy as jnp
from jax import lax
from jax.experimental import pallas as pl
from jax.experimental.pallas import tpu as pltpu

D_MODEL = 1024
DEPTH = 2
HEAD_DIM = 64
ROPE_THETA = 10000.0
LN_EPS = 1e-5
A_HEADS = 8
MOBA_BLOCK = 256
MOBA_TOPK = 3
B_HEADS = 8
B_KV_HEADS = 2
SWA_WINDOW = 128
C_HEADS = 8
C_NOPE_DIM = 64
C_ROPE_DIM = 32
C_V_DIM = 64
C_KV_LATENT = 128
IDX_HEADS = 8
IDX_DIM = 32
DSA_TOPK = 256
D_HEADS = 8
MIX_WIDTH = 1024
EVEN_SIZES = (512, 512, 512, 512, 128, 128, 1024)
ODD_SIZES = (512, 256, 128, 32, 256, 32, 8, 512, 512, 512, 1024)
DEEPNORM_ALPHA = (2 * DEPTH) ** 0.25

LANES = 128
NEG_BIG = -1e30
INT_MIN = -2 ** 31
SB_UNDERFLOW = -104.0

PROJ_TM = 512
DSA_TQ = 512
DSA_CK = 512
SB_T = 128
MOBA_PAIRS = 4
SWA_WINDOWS_PER_STEP = 2
BISECT_STEPS = 20
BISECT_EXTRA = 4
VMEM_LIMIT = 48 * 1024 * 1024

_BF16 = jnp.bfloat16
_F32 = jnp.float32


def _cparams(sem):
    return pltpu.CompilerParams(dimension_semantics=sem, vmem_limit_bytes=VMEM_LIMIT)


def _dot_t(a, b):
    return lax.dot_general(a, b, (((1,), (1,)), ((), ())), preferred_element_type=_F32)


def _dot(a, b):
    return jnp.dot(a, b, preferred_element_type=_F32)


def _lane_iota(shape):
    return lax.broadcasted_iota(jnp.int32, shape, len(shape) - 1)


def _row_iota(shape):
    return lax.broadcasted_iota(jnp.int32, shape, len(shape) - 2)


def _head_pair_rhs(q_t, head_dim):
    frow = _row_iota(q_t.shape)
    return jnp.concatenate(
        [jnp.where((frow // head_dim) == hh, q_t, jnp.zeros_like(q_t)) for hh in range(2)], axis=1)


def _ada_body(c_ref, w_ref, b_ref, o_ref):
    c = c_ref[...]
    cond = c * jax.nn.sigmoid(c)
    o_ref[0] = _dot(cond, w_ref[0]) + b_ref[0]


def _ada_call(c, w_ada, b_ada):
    depth, d, n3 = w_ada.shape
    bsz = c.shape[0]
    nb = n3 // d
    return pl.pallas_call(
        _ada_body,
        out_shape=jax.ShapeDtypeStruct((depth, bsz, n3), _F32),
        grid=(depth, nb),
        in_specs=[
            pl.BlockSpec((bsz, d), lambda l, j: (0, 0)),
            pl.BlockSpec((1, d, d), lambda l, j: (l, 0, j)),
            pl.BlockSpec((1, 1, d), lambda l, j: (l, 0, j)),
        ],
        out_specs=pl.BlockSpec((1, bsz, d), lambda l, j: (l, 0, j)),
        compiler_params=_cparams(("arbitrary", "arbitrary")),
        name="ada_mod",
    )(c, w_ada, b_ada.reshape(depth, 1, n3))


def _rope_tables(seq, dim, period_lanes):
    inv = 1.0 / (ROPE_THETA ** (jnp.arange(0, dim, 2, dtype=_F32) / dim))
    ang = jnp.arange(seq, dtype=_F32)[:, None] * inv[None, :]
    cos, sin = jnp.cos(ang), jnp.sin(ang)
    cos_h = jnp.concatenate([cos, cos], axis=1)
    sin_h = jnp.concatenate([-sin, sin], axis=1)
    reps = period_lanes // dim
    return jnp.tile(cos_h, (1, reps)), jnp.tile(sin_h, (1, reps))


def _rope_piece(x, cos, sin_signed, dim):
    half = dim // 2
    first = (_lane_iota(x.shape) % dim) < half
    partner = jnp.where(first, pltpu.roll(x, LANES - half, 1), pltpu.roll(x, half, 1))
    return x * cos + partner * sin_signed


def _modulated(x_ref, mod_ref):
    x = x_ref[0]
    shift = mod_ref[0, 0:1, :]
    scale = mod_ref[0, 1:2, :]
    return (x * (1.0 + scale) + shift).astype(_BF16)


def _silu_gate(h, w_ref, col0, g_ref):
    for p in range(2):
        gate = _dot(h, w_ref[:, col0 + 512 * p:col0 + 512 * (p + 1)])
        g_ref[0, :, 512 * p:512 * (p + 1)] = (gate * jax.nn.sigmoid(gate)).astype(_BF16)


def _store_blocks_t(acc, out_ref, blk):
    tm, n = acc.shape
    for r in range(tm // blk):
        for p in range(n // LANES):
            out_ref[0, r, p * LANES:(p + 1) * LANES, :] = (
                acc[r * blk:(r + 1) * blk, p * LANES:(p + 1) * LANES].T.astype(_BF16))


EVEN_COLS = 3584


def _prep_w_even(w):
    offs = [0]
    for n in EVEN_SIZES:
        offs.append(offs[-1] + n)
    w = w.astype(_BF16)
    aq, ak, av, bq, bk, bv, gate = [w[:, offs[i]:offs[i + 1]] for i in range(len(EVEN_SIZES))]

    def dup(t):
        parts = []
        for g in range(B_KV_HEADS):
            blk = t[:, g * HEAD_DIM:(g + 1) * HEAD_DIM]
            parts += [blk, blk]
        return jnp.concatenate(parts, axis=1)

    return jnp.concatenate([aq, ak, bq, dup(bk), av, dup(bv), gate], axis=1)


def _inproj_even_body(x_ref, mod_ref, w_ref, cos_ref, sin_ref,
                      aqt_ref, ak_ref, bq_ref, bk_ref, avt_ref, bv_ref, g_ref):
    h = _modulated(x_ref, mod_ref)
    cos = cos_ref[...]
    sin = sin_ref[...]
    q_scale = HEAD_DIM ** -0.5

    def roped(col0, ncols, out_ref, scale, transposed=False):
        acc = _dot(h, w_ref[:, col0:col0 + ncols])
        for p in range(ncols // LANES):
            piece = _rope_piece(acc[:, p * LANES:(p + 1) * LANES], cos, sin, HEAD_DIM)
            if scale != 1.0:
                piece = piece * scale
            if transposed:
                out_ref[0, p * LANES:(p + 1) * LANES, :] = piece.T.astype(_BF16)
            else:
                out_ref[0, :, p * LANES:(p + 1) * LANES] = piece.astype(_BF16)

    roped(0, 512, aqt_ref, q_scale, transposed=True)
    roped(512, 512, ak_ref, 1.0)
    roped(1024, 512, bq_ref, q_scale)
    roped(1536, 256, bk_ref, 1.0)
    _store_blocks_t(_dot(h, w_ref[:, 1792:2304]), avt_ref, MOBA_BLOCK)
    bv_ref[0] = _dot(h, w_ref[:, 2304:2560]).astype(_BF16)
    _silu_gate(h, w_ref, 2560, g_ref)


def _inproj_even_call(x, mod, w, cos, sin):
    bsz, seq, d = x.shape
    tm = PROJ_TM
    row = lambda n: pl.BlockSpec((1, tm, n), lambda b, i: (b, i, 0))
    tok = lambda n: jax.ShapeDtypeStruct((bsz, seq, n), _BF16)
    per_tile = tm // MOBA_BLOCK
    return pl.pallas_call(
        _inproj_even_body,
        out_shape=[
            jax.ShapeDtypeStruct((bsz, 512, seq), _BF16), tok(512), tok(512), tok(256),
            jax.ShapeDtypeStruct((bsz, seq // MOBA_BLOCK, 512, MOBA_BLOCK), _BF16), tok(256), tok(1024),
        ],
        grid=(bsz, seq // tm),
        in_specs=[
            row(d),
            pl.BlockSpec((1, 3, d), lambda b, i: (b, 0, 0)),
            pl.BlockSpec((d, EVEN_COLS), lambda b, i: (0, 0)),
            pl.BlockSpec((tm, LANES), lambda b, i: (i, 0)),
            pl.BlockSpec((tm, LANES), lambda b, i: (i, 0)),
        ],
        out_specs=[
            pl.BlockSpec((1, 512, tm), lambda b, i: (b, 0, i)), row(512), row(512), row(256),
            pl.BlockSpec((1, per_tile, 512, MOBA_BLOCK), lambda b, i: (b, i, 0, 0)), row(256), row(1024),
        ],
        compiler_params=_cparams(("arbitrary", "arbitrary")),
        name="inproj_even",
    )(x, mod, w, cos, sin)


def _moba_body(qt_ref, k_ref, vt_ref, o_ref, km_ref, selb_ref, acc_ref):
    qi = pl.program_id(2)
    tq = MOBA_BLOCK
    nblk = km_ref.shape[0]
    npair = qt_ref.shape[1] // LANES
    pw = 2 * tq
    width = npair * pw

    @pl.when(qi == 0)
    def _():
        for r in range(nblk):
            blk_k = k_ref[0, r * tq:(r + 1) * tq, :].astype(_F32)
            km_ref[r:r + 1, :] = jnp.mean(blk_k, axis=0, keepdims=True)

    qcats = [_head_pair_rhs(qt_ref[0, p * LANES:(p + 1) * LANES, :], HEAD_DIM) for p in range(npair)]

    blk = _row_iota((nblk, width))
    blkf = blk.astype(_F32)
    valid = blk < qi
    gate = jnp.concatenate([_dot(km_ref[:, p * LANES:(p + 1) * LANES].astype(_BF16), qcats[p])
                            for p in range(npair)], axis=1)
    gate = jnp.where(valid, gate, -jnp.inf)
    sel = jnp.zeros((nblk, width), _F32)
    for _ in range(MOBA_TOPK):
        top = jnp.max(gate, axis=0, keepdims=True)
        first = jnp.min(jnp.where(gate == top, blkf, float(nblk)), axis=0, keepdims=True)
        pick = blkf == first
        sel = jnp.where(pick, 1.0, sel)
        gate = jnp.where(pick, -jnp.inf, gate)
    selb_ref[...] = jnp.where(valid & (sel > 0.0), 0.0, NEG_BIG)

    acc_ref[...] = jnp.zeros_like(acc_ref)

    def step(carry, blocks):
        m, l = carry
        ks = [k_ref[0, pl.ds(pl.multiple_of(j * tq, tq), tq), :] for j, _ in blocks]
        ss = [[_dot(kj[:, p * LANES:(p + 1) * LANES], qcats[p]) + bias(p) for kj, (_, bias) in zip(ks, blocks)]
              for p in range(npair)]
        ms, ls = [], []
        for p in range(npair):
            pcols = slice(p * pw, (p + 1) * pw)
            s0, s1 = ss[p]
            m_new = jnp.maximum(m[:, pcols], jnp.maximum(jnp.max(s0, axis=0, keepdims=True),
                                                         jnp.max(s1, axis=0, keepdims=True)))
            alpha = jnp.exp(m[:, pcols] - m_new)
            e0 = jnp.exp(s0 - m_new)
            e1 = jnp.exp(s1 - m_new)
            ls.append(alpha * l[:, pcols] + jnp.sum(e0, axis=0, keepdims=True) + jnp.sum(e1, axis=0, keepdims=True))
            ms.append(m_new)
            e = jnp.concatenate([e0.astype(_BF16), e1.astype(_BF16)], axis=0)
            for hh in range(2):
                rows = slice((2 * p + hh) * HEAD_DIM, (2 * p + hh + 1) * HEAD_DIM)
                cols = slice(hh * tq, (hh + 1) * tq)
                v_t = jnp.concatenate([vt_ref[0, j, rows, :] for j, _ in blocks], axis=1)
                acc_ref[rows, :] = alpha[:, cols] * acc_ref[rows, :] + _dot(v_t, e[:, cols])
        return jnp.concatenate(ms, axis=1), jnp.concatenate(ls, axis=1)

    def row_bias(j, live):
        row = jnp.where(live, selb_ref[pl.ds(j, 1), :], NEG_BIG)
        return lambda p: row[:, p * pw:(p + 1) * pw]

    def pair_body(t, carry):
        return step(carry, [(2 * t, row_bias(2 * t, True)), (2 * t + 1, row_bias(2 * t + 1, True))])

    carry = (jnp.full((1, width), NEG_BIG, _F32), jnp.zeros((1, width), _F32))
    carry = lax.fori_loop(0, qi // 2, pair_body, carry)
    causal = jnp.where(_row_iota((tq, pw)) <= _lane_iota((tq, pw)) % tq, 0.0, NEG_BIG)
    last = jnp.maximum(qi - 1, 0)
    _, l = step(carry, [(qi, lambda p: causal), (last, row_bias(last, qi % 2 == 1))])
    inv = 1.0 / l
    out_t = jnp.concatenate([acc_ref[h * HEAD_DIM:(h + 1) * HEAD_DIM, :] * inv[:, h * tq:(h + 1) * tq]
                             for h in range(2 * npair)], axis=0)
    o_ref[0] = out_t.T.astype(o_ref.dtype)


def _moba_call(q_t, k, v_t):
    bsz, seq, width = k.shape
    tq = MOBA_BLOCK
    nblk = seq // tq
    gw = MOBA_PAIRS * LANES
    return pl.pallas_call(
        _moba_body,
        out_shape=jax.ShapeDtypeStruct((bsz, seq, width), _BF16),
        grid=(bsz, width // gw, nblk),
        in_specs=[
            pl.BlockSpec((1, gw, tq), lambda b, h, i: (b, h, i)),
            pl.BlockSpec((1, seq, gw), lambda b, h, i: (b, 0, h)),
            pl.BlockSpec((1, nblk, gw, tq), lambda b, h, i: (b, 0, h, 0)),
        ],
        out_specs=pl.BlockSpec((1, tq, gw), lambda b, h, i: (b, i, h)),
        scratch_shapes=[
            pltpu.VMEM((nblk, gw), _F32),
            pltpu.VMEM((nblk, MOBA_PAIRS * 2 * tq), _F32),
            pltpu.VMEM((gw, tq), _F32),
        ],
        compiler_params=_cparams(("arbitrary", "arbitrary", "arbitrary")),
        name="moba",
    )(q_t, k, v_t)


def _swa_body(sink_ref, q_ref, kp_ref, kc_ref, vp_ref, vc_ref, o_ref):
    i = pl.program_id(1)
    w = SWA_WINDOW
    lane = _lane_iota((1, LANES))
    row = _row_iota((w, 2 * w))
    col = _lane_iota((w, 2 * w))
    in_window = (col > row) & (col <= row + w)
    pairs_per_kv = (B_HEADS // B_KV_HEADS) // 2
    nwin = q_ref.shape[1] // w

    def band(prev_ref, cur_ref, n, g):
        glanes = slice(g * LANES, (g + 1) * LANES)
        before = prev_ref[0, :, glanes] if n == 0 else cur_ref[0, (n - 1) * w:n * w, glanes]
        return jnp.concatenate([before, cur_ref[0, n * w:(n + 1) * w, glanes]], axis=0)

    ss, vs = [], []
    for n in range(nwin):
        ks = [band(kp_ref, kc_ref, n, g) for g in range(B_KV_HEADS)]
        vs.append([band(vp_ref, vc_ref, n, g) for g in range(B_KV_HEADS)])
        for h in range(B_HEADS):
            qp = q_ref[0, n * w:(n + 1) * w, (h // 2) * LANES:(h // 2 + 1) * LANES]
            qh = jnp.where((lane // HEAD_DIM) == (h % 2), qp, jnp.zeros_like(qp))
            ss.append(_dot_t(qh, ks[(h // 2) // pairs_per_kv]))
    for n in range(nwin):
        mask = in_window & ((col >= w) | (i > 0)) if n == 0 else in_window
        outs = []
        for h in range(B_HEADS):
            sink = sink_ref[h]
            s = jnp.where(mask, ss[n * B_HEADS + h], -jnp.inf)
            m = jnp.maximum(jnp.max(s, axis=1, keepdims=True), sink)
            e = jnp.exp(s - m)
            l = jnp.sum(e, axis=1, keepdims=True) + jnp.exp(sink - m)
            outs.append(_dot(e.astype(_BF16), vs[n][(h // 2) // pairs_per_kv]) / l)
        for p in range(B_HEADS // 2):
            o_ref[0, n * w:(n + 1) * w, p * LANES:(p + 1) * LANES] = jnp.where(
                lane < HEAD_DIM, outs[2 * p], outs[2 * p + 1]).astype(o_ref.dtype)


def _swa_call(q, kdup, vdup, sinks):
    bsz, seq, width = q.shape
    w = SWA_WINDOW
    kvw = kdup.shape[2]
    nwin = SWA_WINDOWS_PER_STEP
    tq = nwin * w
    prev = pl.BlockSpec((1, w, kvw), lambda b, i: (b, jnp.maximum(i * nwin - 1, 0), 0))
    cur = pl.BlockSpec((1, tq, kvw), lambda b, i: (b, i, 0))
    return pl.pallas_call(
        _swa_body,
        out_shape=jax.ShapeDtypeStruct((bsz, seq, width), _BF16),
        grid=(bsz, seq // tq),
        in_specs=[
            pl.BlockSpec(memory_space=pltpu.SMEM),
            pl.BlockSpec((1, tq, width), lambda b, i: (b, i, 0)),
            prev, cur, prev, cur,
        ],
        out_specs=pl.BlockSpec((1, tq, width), lambda b, i: (b, i, 0)),
        compiler_params=_cparams(("arbitrary", "arbitrary")),
        name="swa_sink",
    )(sinks, q, kdup, kdup, vdup, vdup)


def _outproj_body(o1_ref, o2_ref, g_ref, x_ref, mod_ref, w_ref, lng_ref, lnb_ref, y_ref):
    half = o1_ref.shape[2]
    a1 = o1_ref[0] * g_ref[0, :, :half]
    a2 = o2_ref[0] * g_ref[0, :, half:]
    y = _dot(a1, w_ref[:half, :]) + _dot(a2, w_ref[half:, :])
    gate = mod_ref[0, 2:3, :]
    z = DEEPNORM_ALPHA * x_ref[0] + (1.0 + gate) * y
    mu = jnp.mean(z, axis=1, keepdims=True)
    zc = z - mu
    var = jnp.mean(zc * zc, axis=1, keepdims=True)
    y_ref[0] = zc * lax.rsqrt(var + LN_EPS) * lng_ref[...] + lnb_ref[...]


def _outproj_call(o1, o2, g, x, mod, w_out, ln_g, ln_b):
    bsz, seq, d = x.shape
    tm = PROJ_TM
    half = o1.shape[2]
    row = lambda n: pl.BlockSpec((1, tm, n), lambda b, i: (b, i, 0))
    return pl.pallas_call(
        _outproj_body,
        out_shape=jax.ShapeDtypeStruct((bsz, seq, d), _F32),
        grid=(bsz, seq // tm),
        in_specs=[
            row(half), row(half), row(2 * half), row(d),
            pl.BlockSpec((1, 3, d), lambda b, i: (b, 0, 0)),
            pl.BlockSpec((2 * half, d), lambda b, i: (0, 0)),
            pl.BlockSpec((1, d), lambda b, i: (0, 0)),
            pl.BlockSpec((1, d), lambda b, i: (0, 0)),
        ],
        out_specs=row(d),
        compiler_params=_cparams(("arbitrary", "arbitrary")),
        name="outproj_deepnorm",
    )(o1, o2, g, x, mod, w_out.astype(_BF16), ln_g.reshape(1, d), ln_b.reshape(1, d))


ODD_COLS = 4096


def _prep_w_odd(w):
    offs = [0]
    for n in ODD_SIZES:
        offs.append(offs[-1] + n)
    w = w.astype(_BF16)
    cqn, cqr, ckv, ckr, iq, ik, iw, dq, dk, dv, gate = [w[:, offs[i]:offs[i + 1]] for i in range(len(ODD_SIZES))]
    iw_blk = jnp.concatenate([iw, jnp.zeros((w.shape[0], LANES - IDX_HEADS), w.dtype)], axis=1)
    return jnp.concatenate([cqn, cqr, ckv, jnp.tile(ckr, (1, 4)), iq, jnp.tile(ik, (1, 4)), iw_blk,
                            dq, dk, dv, gate], axis=1)


def _inproj_odd_body(x_ref, mod_ref, w_ref, cosr_ref, sinr_ref, cosi_ref, sini_ref, kvg_ref,
                     cqn_ref, cqr_ref, kvc_ref, ckvt_ref, iq_ref, ik_ref, iw_ref,
                     dqt_ref, dk_ref, dvt_ref, g_ref):
    h = _modulated(x_ref, mod_ref)
    cosr, sinr = cosr_ref[...], sinr_ref[...]
    cosi, sini = cosi_ref[...], sini_ref[...]

    acc = _dot(h, w_ref[:, 0:512])
    for p in range(4):
        cqn_ref[0, p * LANES:(p + 1) * LANES, :] = acc[:, p * LANES:(p + 1) * LANES].T.astype(_BF16)
    acc = _dot(h, w_ref[:, 512:1024])
    for p in range(2):
        piece = _rope_piece(acc[:, p * LANES:(p + 1) * LANES], cosr, sinr, C_ROPE_DIM)
        cqr_ref[0, p * LANES:(p + 1) * LANES, :] = piece.T.astype(_BF16)
    ckv = acc[:, 256:384]
    ckv = ckv * lax.rsqrt(jnp.mean(ckv * ckv, axis=1, keepdims=True) + LN_EPS) * kvg_ref[...]
    kvc_ref[0, :, 0:LANES] = ckv.astype(_BF16)
    ckvt_ref[0, 0] = ckv.T.astype(_BF16)
    kvc_ref[0, :, LANES:2 * LANES] = _rope_piece(acc[:, 384:512], cosr, sinr, C_ROPE_DIM).astype(_BF16)
    acc = _dot(h, w_ref[:, 1024:1536])
    for p in range(2):
        piece = _rope_piece(acc[:, p * LANES:(p + 1) * LANES], cosi, sini, IDX_DIM)
        iq_ref[0, p * LANES:(p + 1) * LANES, :] = piece.T.astype(_BF16)
    ik_ref[0] = _rope_piece(acc[:, 256:384], cosi, sini, IDX_DIM).astype(_BF16)
    iw_ref[0] = acc[:, 384:512].T[0:IDX_HEADS, :]
    acc = _dot(h, w_ref[:, 1536:2048]) * (HEAD_DIM ** -0.5)
    for p in range(4):
        dqt_ref[0, p * LANES:(p + 1) * LANES, :] = acc[:, p * LANES:(p + 1) * LANES].T.astype(_BF16)
    dk_ref[0] = _dot(h, w_ref[:, 2048:2560]).astype(_BF16)
    _store_blocks_t(_dot(h, w_ref[:, 2560:3072]), dvt_ref, SB_T)
    _silu_gate(h, w_ref, 3072, g_ref)


def _inproj_odd_call(x, mod, w, rope_r, rope_i, kv_g):
    bsz, seq, d = x.shape
    tm = DSA_CK
    row = lambda n: pl.BlockSpec((1, tm, n), lambda b, i: (b, i, 0))
    col = lambda n: pl.BlockSpec((1, n, tm), lambda b, i: (b, 0, i))
    tab = pl.BlockSpec((tm, LANES), lambda b, i: (i, 0))
    tok = lambda n, dt: jax.ShapeDtypeStruct((bsz, seq, n), dt)
    feat = lambda n, dt: jax.ShapeDtypeStruct((bsz, n, seq), dt)
    return pl.pallas_call(
        _inproj_odd_body,
        out_shape=[
            feat(512, _BF16), feat(256, _BF16), tok(256, _BF16),
            jax.ShapeDtypeStruct((bsz, seq // tm, C_KV_LATENT, tm), _BF16),
            feat(256, _BF16), tok(128, _BF16), feat(IDX_HEADS, _F32),
            feat(512, _BF16), tok(512, _BF16),
            jax.ShapeDtypeStruct((bsz, seq // SB_T, 512, SB_T), _BF16), tok(1024, _BF16),
        ],
        grid=(bsz, seq // tm),
        in_specs=[
            row(d),
            pl.BlockSpec((1, 3, d), lambda b, i: (b, 0, 0)),
            pl.BlockSpec((d, ODD_COLS), lambda b, i: (0, 0)),
            tab, tab, tab, tab,
            pl.BlockSpec((1, LANES), lambda b, i: (0, 0)),
        ],
        out_specs=[
            col(512), col(256), row(256),
            pl.BlockSpec((1, 1, C_KV_LATENT, tm), lambda b, i: (b, i, 0, 0)),
            col(256), row(128), col(IDX_HEADS),
            col(512), row(512),
            pl.BlockSpec((1, tm // SB_T, 512, SB_T), lambda b, i: (b, i, 0, 0)), row(1024),
        ],
        compiler_params=_cparams(("arbitrary", "arbitrary")),
        name="inproj_odd",
    )(x, mod, w, rope_r[0], rope_r[1], rope_i[0], rope_i[1], kv_g.reshape(1, LANES))


def _dsa_body(cqn_ref, cqr_ref, iq_ref, iw_ref, kvc_ref, ckvt_ref, ik_ref, wuk_ref, wuvt_ref, o_ref,
              qct_ref, iqt_ref, sc_ref, acc_ref):
    i = pl.program_id(1)
    tq, ck = DSA_TQ, DSA_CK
    seq = kvc_ref.shape[1]
    n_sel = float(min(DSA_TOPK, seq // 4))
    nch = i // (ck // tq) + 1
    att_scale = (C_NOPE_DIM + C_ROPE_DIM) ** -0.5
    idx_scale = (IDX_DIM * IDX_HEADS) ** -0.5
    hq = C_HEADS * tq

    row128 = _row_iota((LANES, tq))
    for h in range(C_HEADS):
        cols = slice(h * tq, (h + 1) * tq)
        qn_t = cqn_ref[0, h * C_NOPE_DIM:(h + 1) * C_NOPE_DIM, :]
        qlat_t = _dot(wuk_ref[h], qn_t) * att_scale
        qct_ref[0:C_KV_LATENT, cols] = qlat_t.astype(_BF16)
        qr_t = cqr_ref[0, h * C_ROPE_DIM:(h + 1) * C_ROPE_DIM, :].astype(_F32) * att_scale
        qct_ref[C_KV_LATENT:C_KV_LATENT + C_ROPE_DIM, cols] = qr_t.astype(_BF16)
        qct_ref[C_KV_LATENT + C_ROPE_DIM:, cols] = jnp.zeros((LANES - C_ROPE_DIM, tq), _BF16)
        blk4 = h // 4
        iq_t = iq_ref[0, blk4 * LANES:(blk4 + 1) * LANES, :]
        iqt_ref[:, cols] = jnp.where((row128 // IDX_DIM) == (h % 4), iq_t, jnp.zeros_like(iq_t))

    t_pos = i * tq + _lane_iota((ck, tq))
    key_row = _row_iota((ck, tq))

    def score_body(c, carry):
        vmax, vmin = carry
        off = pl.multiple_of(c * ck, ck)
        rel = _dot(ik_ref[0, pl.ds(off, ck), :], iqt_ref[...])
        score = jnp.zeros((ck, tq), _F32)
        for h in range(IDX_HEADS):
            score = score + jnp.maximum(rel[:, h * tq:(h + 1) * tq], 0.0) * iw_ref[0, h:h + 1, :]
        score = score * idx_scale
        low = jnp.where(off + key_row <= t_pos, score, -jnp.inf)
        sc_ref[c] = low
        vmax = jnp.maximum(vmax, jnp.max(low, axis=0, keepdims=True))
        vmin = jnp.minimum(vmin, jnp.min(score, axis=0, keepdims=True))
        return vmax, vmin

    vmax, vmin = lax.fori_loop(0, nch, score_body,
                               (jnp.full((1, tq), -jnp.inf, _F32), jnp.full((1, tq), jnp.inf, _F32)))

    def count(pred):
        rows_acc = 8 * (8 * LANES // tq)

        slab_row = _row_iota((rows_acc, tq))

        def body(c, acc):
            for r in range(ck // rows_acc):
                rows = slice(r * rows_acc, (r + 1) * rows_acc)
                acc = acc + jnp.where(pred(sc_ref[c, rows, :], c * ck + r * rows_acc + slab_row), 1.0, 0.0)
            return acc

        acc = lax.fori_loop(0, nch, body, jnp.zeros((rows_acc, tq), _F32))
        return jnp.sum(acc, axis=0, keepdims=True)

    def ordered_float(u):
        key = u ^ INT_MIN
        return pltpu.bitcast(key ^ ((key >> 31) & 0x7FFFFFFF), _F32)

    n_valid = (i * tq + _lane_iota((1, tq)) + 1).astype(_F32)
    few = n_valid < n_sel
    c_ge0 = count(lambda sc, pos: sc >= 0.0)
    c_gt0 = count(lambda sc, pos: sc > 0.0)
    zero_tie = (c_gt0 < n_sel) & (c_ge0 >= n_sel)
    above = c_gt0 >= n_sel

    def bisect_body(_, carry):
        lo, hi, cnt_lo = carry
        mid = lo + 0.5 * (hi - lo)
        cnt = count(lambda sc, pos: sc >= mid)
        ok = cnt >= n_sel
        return jnp.where(ok, mid, lo), jnp.where(ok, hi, mid), jnp.where(ok, cnt, cnt_lo)

    def all_resolved(cnt_lo):
        done = few | zero_tie | (cnt_lo == n_sel)
        return jnp.min(jnp.where(done, 1.0, 0.0)) > 0.0

    state = (jnp.where(above, 0.0, vmin), jnp.where(above, vmax, 0.0), jnp.where(above, c_ge0, n_valid))
    state = lax.fori_loop(0, BISECT_STEPS, bisect_body, state)
    state = lax.cond(all_resolved(state[2]), lambda s: s,
                     lambda s: lax.fori_loop(0, BISECT_EXTRA, bisect_body, s), state)
    lo, _, cnt_lo = state

    def exact_search():
        def bit_body(b, carry):
            t, cnt_t = carry
            cand = t | jnp.left_shift(jnp.int32(1), 31 - b)
            cand_f = ordered_float(cand)
            cnt = count(lambda sc, pos: sc >= cand_f)
            ok = cnt >= n_sel
            return jnp.where(ok, cand, t), jnp.where(ok, cnt, cnt_t)

        t0 = jnp.zeros((1, tq), jnp.int32)
        t, cnt_t = lax.fori_loop(0, 32, bit_body, (t0, jnp.full((1, tq), float(seq), _F32)))
        return jnp.where(t == 0, -jnp.inf, ordered_float(t)), cnt_t

    thr, cnt_thr = lax.cond(
        all_resolved(cnt_lo),
        lambda: (jnp.where(few, -jnp.inf, jnp.where(zero_tie, 0.0, lo)), jnp.where(zero_tie, c_ge0, cnt_lo)),
        exact_search)

    excess = jnp.max(jnp.where((cnt_thr > n_sel) & (thr > -jnp.inf), 1.0, 0.0)) > 0.0

    def tie_limit():
        room = n_sel - count(lambda sc, pos: sc > thr)

        def lim_body(b, lim):
            cand = lim | jnp.left_shift(jnp.int32(1), 12 - b)
            cnt = count(lambda sc, pos: (sc == thr) & (pos < cand))
            return jnp.where(cnt <= room, cand, lim)

        return lax.fori_loop(0, 13, lim_body, jnp.zeros((1, tq), jnp.int32))

    limit = lax.cond(excess, tie_limit, lambda: jnp.full((1, tq), 2 * seq, jnp.int32))

    def bias_body(c, _):
        sc = sc_ref[c]
        pos = c * ck + key_row
        chosen = ((sc > thr) | ((sc == thr) & (pos < limit))) & (pos <= t_pos)
        sc_ref[c] = jnp.where(chosen, 0.0, NEG_BIG)
        return 0

    lax.fori_loop(0, nch, bias_body, 0)

    acc_ref[...] = jnp.zeros_like(acc_ref)
    grp = 2 * tq
    gcols = [slice(g * grp, (g + 1) * grp) for g in range(hq // grp)]

    def att_body(c, carry):
        off = pl.multiple_of(c * ck, ck)
        kv = kvc_ref[0, pl.ds(off, ck), :]
        bias = sc_ref[c]
        bias2 = jnp.concatenate([bias, bias], axis=1)
        ss = [_dot(kv, qct_ref[:, cols]) + bias2 for cols in gcols]
        ms, ls, es, alphas = [], [], [], []
        for g, cols in enumerate(gcols):
            m, l = carry[0][:, cols], carry[1][:, cols]
            m_new = jnp.maximum(m, jnp.max(ss[g], axis=0, keepdims=True))
            alpha = jnp.exp(m - m_new)
            e = jnp.exp(ss[g] - m_new)
            ls.append(alpha * l + jnp.sum(e, axis=0, keepdims=True))
            ms.append(m_new)
            alphas.append(alpha)
            es.append(e.astype(_BF16))
        kv_t = ckvt_ref[0, c]
        for g, cols in enumerate(gcols):
            acc_ref[:, cols] = alphas[g] * acc_ref[:, cols] + _dot(kv_t, es[g])
        return jnp.concatenate(ms, axis=1), jnp.concatenate(ls, axis=1)

    m0 = jnp.full((1, hq), NEG_BIG, _F32)
    l0 = jnp.zeros((1, hq), _F32)
    _, l = lax.fori_loop(0, nch, att_body, (m0, l0))
    o_lat_t = (acc_ref[...] / l).astype(_BF16)
    outs = [_dot(wuvt_ref[h], o_lat_t[:, h * tq:(h + 1) * tq]) for h in range(C_HEADS)]
    o_ref[0] = jnp.concatenate(outs, axis=0).T.astype(o_ref.dtype)


def _dsa_call(cqn_t, cqr_t, iq_t, iw_t, kvc, ckv_t, ik, w_uk, wuv_t):
    bsz, seq, _ = kvc.shape
    tq = DSA_TQ
    qcol = lambda n: pl.BlockSpec((1, n, tq), lambda b, i: (b, 0, i))
    full = lambda n: pl.BlockSpec((1, seq, n), lambda b, i: (b, 0, 0))
    whole = lambda a: pl.BlockSpec(a.shape, lambda b, i: (0,) * a.ndim)
    width = C_HEADS * C_V_DIM
    return pl.pallas_call(
        _dsa_body,
        out_shape=jax.ShapeDtypeStruct((bsz, seq, width), _BF16),
        grid=(bsz, seq // tq),
        in_specs=[
            qcol(512), qcol(256), qcol(256), qcol(IDX_HEADS), full(256),
            pl.BlockSpec((1,) + ckv_t.shape[1:], lambda b, i: (b, 0, 0, 0)),
            full(128), whole(w_uk), whole(wuv_t),
        ],
        out_specs=pl.BlockSpec((1, tq, width), lambda b, i: (b, i, 0)),
        scratch_shapes=[
            pltpu.VMEM((2 * LANES, C_HEADS * tq), _BF16),
            pltpu.VMEM((LANES, IDX_HEADS * tq), _BF16),
            pltpu.VMEM((seq // DSA_CK, DSA_CK, tq), _F32),
            pltpu.VMEM((C_KV_LATENT, C_HEADS * tq), _F32),
        ],
        compiler_params=_cparams(("arbitrary", "arbitrary")),
        name="dsa",
    )(cqn_t, cqr_t, iq_t, iw_t, kvc, ckv_t, ik, w_uk, wuv_t)


def _sb_body(qt_ref, k_ref, vt_ref, o_ref, acc_ref):
    i = pl.program_id(1)
    t = SB_T
    npair = D_HEADS // 2
    width = 2 * t
    qcats = [_head_pair_rhs(qt_ref[0, p * LANES:(p + 1) * LANES, :], HEAD_DIM) for p in range(npair)]
    key_row = _row_iota((t, width))
    q_lane = _lane_iota((t, width)) % t
    later = jnp.where(_lane_iota((t, t)) > _row_iota((t, t)), 1.0, 0.0).astype(_BF16)
    later2 = jnp.concatenate([later, later], axis=1)
    acc_ref[...] = jnp.zeros_like(acc_ref)

    def blocks(js, rest, masks):
        ks = [k_ref[0, pl.ds(pl.multiple_of(j * t, t), t), :] for j in js]
        zs = [[_dot(kj[:, p * LANES:(p + 1) * LANES], qcats[p]) for p in range(npair)] for kj in ks]
        rests = []
        for p in range(npair):
            run = rest[:, p * width:(p + 1) * width]
            for n, j in enumerate(js):
                z = zs[n][p]
                log_keep = -(jnp.maximum(z, 0.0) + jnp.log(1.0 + jnp.exp(-jnp.abs(z))))
                log_beta = z + log_keep
                if masks[n] is not None:
                    log_keep = jnp.where(masks[n], log_keep, 0.0)
                hi = log_keep.astype(_BF16)
                lo = (log_keep - hi.astype(_F32)).astype(_BF16)
                a = jnp.exp(log_beta + (_dot(later2, jnp.concatenate([hi, lo], axis=0)) + run))
                if masks[n] is not None:
                    a = jnp.where(masks[n], a, 0.0)
                prows = slice(p * LANES, (p + 1) * LANES)
                pv = _dot(vt_ref[0, j, prows, :], a.astype(_BF16))
                for hh in range(2):
                    rows = slice((2 * p + hh) * HEAD_DIM, (2 * p + hh + 1) * HEAD_DIM)
                    acc_ref[rows, :] = acc_ref[rows, :] + pv[hh * HEAD_DIM:(hh + 1) * HEAD_DIM, hh * t:(hh + 1) * t]
                run = run + jnp.sum(log_keep, axis=0, keepdims=True)
            rests.append(run)
        return jnp.concatenate(rests, axis=1)

    rest = blocks([i], jnp.zeros((1, npair * width), _F32), [key_row < q_lane])

    def cond(carry):
        j, _, top = carry
        return (j >= 0) & (top > SB_UNDERFLOW)

    def body(carry):
        j, rest, _ = carry
        second_ok = jnp.broadcast_to(j >= 1, (t, width))
        rest = blocks([j, jnp.maximum(j - 1, 0)], rest, [None, second_ok])
        return j - 2, rest, jnp.max(rest)

    lax.while_loop(cond, body, (i - 1, rest, jnp.max(rest)))
    o_ref[0] = acc_ref[...].T.astype(o_ref.dtype)


def _sb_call(q_t, k, v_t):
    bsz, seq, width = k.shape
    t = SB_T
    return pl.pallas_call(
        _sb_body,
        out_shape=jax.ShapeDtypeStruct((bsz, seq, width), _BF16),
        grid=(bsz, seq // t),
        in_specs=[
            pl.BlockSpec((1, width, t), lambda b, i: (b, 0, i)),
            pl.BlockSpec((1, seq, width), lambda b, i: (b, 0, 0)),
            pl.BlockSpec((1, seq // t, width, t), lambda b, i: (b, 0, 0, 0)),
        ],
        out_specs=pl.BlockSpec((1, t, width), lambda b, i: (b, i, 0)),
        scratch_shapes=[pltpu.VMEM((width, t), _F32)],
        compiler_params=_cparams(("arbitrary", "arbitrary")),
        name="stick_breaking",
    )(q_t, k, v_t)


def _even_layer(x, mod, w_in, sinks, w_out, ln_g, ln_b, rope_h):
    aq_t, ak, bq, bk, av_t, bv, g = _inproj_even_call(x, mod, _prep_w_even(w_in), *rope_h)
    oa = _moba_call(aq_t, ak, av_t)
    ob = _swa_call(bq, bk, bv, sinks)
    return _outproj_call(oa, ob, g, x, mod, w_out, ln_g, ln_b)


def _odd_layer(x, mod, w_in, kv_g, w_uk, w_uv, w_out, ln_g, ln_b, rope_r, rope_i):
    cqn_t, cqr_t, kvc, ckv_t, iq_t, ik, iw_t, dq_t, dk, dv_t, g = _inproj_odd_call(
        x, mod, _prep_w_odd(w_in), rope_r, rope_i, kv_g)
    oc = _dsa_call(cqn_t, cqr_t, iq_t, iw_t, kvc, ckv_t, ik,
                   w_uk.astype(_BF16), w_uv.transpose(0, 2, 1).astype(_BF16))
    od = _sb_call(dq_t, dk, dv_t)
    return _outproj_call(oc, od, g, x, mod, w_out, ln_g, ln_b)


def kernel(x, c, w_ada, b_ada, w_in_even, sink_logits, w_in_odd, kv_norm_g, w_uk, w_uv, w_out, ln_g, ln_b):
    bsz, seq, d = x.shape
    rope_h = _rope_tables(seq, HEAD_DIM, LANES)
    rope_r = _rope_tables(seq, C_ROPE_DIM, LANES)
    rope_i = _rope_tables(seq, IDX_DIM, LANES)
    mods = _ada_call(c, w_ada, b_ada).reshape(DEPTH, bsz, 3, d)
    for layer in range(DEPTH):
        mod = mods[layer]
        j = layer // 2
        if layer % 2 == 0:
            x = _even_layer(x, mod, w_in_even[j], sink_logits[j], w_out[layer], ln_g[layer], ln_b[layer], rope_h)
        else:
            x = _odd_layer(x, mod, w_in_odd[j], kv_norm_g[j], w_uk[j], w_uv[j], w_out[layer],
                           ln_g[layer], ln_b[layer], rope_r, rope_i)
    return x
```

```python
import jax
import jax.numpy as jnp
from jax import lax
from jax.experimental import pallas as pl
from jax.experimental.pallas import tpu as pltpu

D_MODEL = 1024
DEPTH = 2
HEAD_DIM = 64
ROPE_THETA = 10000.0
LN_EPS = 1e-5
A_HEADS = 8
MOBA_BLOCK = 256
MOBA_TOPK = 3
B_HEADS = 8
B_KV_HEADS = 2
SWA_WINDOW = 128
C_HEADS = 8
C_NOPE_DIM = 64
C_ROPE_DIM = 32
C_V_DIM = 64
C_KV_LATENT = 128
IDX_HEADS = 8
IDX_DIM = 32
DSA_TOPK = 256
D_HEADS = 8
MIX_WIDTH = 1024
EVEN_SIZES = (512, 512, 512, 512, 128, 128, 1024)
ODD_SIZES = (512, 256, 128, 32, 256, 32, 8, 512, 512, 512, 1024)
DEEPNORM_ALPHA = (2 * DEPTH) ** 0.25

LANES = 128
NEG_BIG = -1e30
INT_MIN = -2 ** 31
SB_UNDERFLOW = -104.0

PROJ_TM = 512
DSA_TQ = 512
DSA_CK = 512
SB_T = 128
MOBA_PAIRS = 4
SWA_WINDOWS_PER_STEP = 2
BISECT_STEPS = 20
BISECT_EXTRA = 4
VMEM_LIMIT = 48 * 1024 * 1024

_BF16 = jnp.bfloat16
_F32 = jnp.float32


def _cparams(sem):
    return pltpu.CompilerParams(dimension_semantics=sem, vmem_limit_bytes=VMEM_LIMIT)


def _dot_t(a, b):
    return lax.dot_general(a, b, (((1,), (1,)), ((), ())), preferred_element_type=_F32)


def _dot(a, b):
    return jnp.dot(a, b, preferred_element_type=_F32)


def _lane_iota(shape):
    return lax.broadcasted_iota(jnp.int32, shape, len(shape) - 1)


def _row_iota(shape):
    return lax.broadcasted_iota(jnp.int32, shape, len(shape) - 2)


def _head_pair_rhs(q_t, head_dim):
    frow = _row_iota(q_t.shape)
    return jnp.concatenate(
        [jnp.where((frow // head_dim) == hh, q_t, jnp.zeros_like(q_t)) for hh in range(2)], axis=1)


def _ada_body(c_ref, w_ref, b_ref, o_ref):
    c = c_ref[...]
    cond = c * jax.nn.sigmoid(c)
    o_ref[0] = _dot(cond, w_ref[0]) + b_ref[0]


def _ada_call(c, w_ada, b_ada):
    depth, d, n3 = w_ada.shape
    bsz = c.shape[0]
    nb = n3 // d
    return pl.pallas_call(
        _ada_body,
        out_shape=jax.ShapeDtypeStruct((depth, bsz, n3), _F32),
        grid=(depth, nb),
        in_specs=[
            pl.BlockSpec((bsz, d), lambda l, j: (0, 0)),
            pl.BlockSpec((1, d, d), lambda l, j: (l, 0, j)),
            pl.BlockSpec((1, 1, d), lambda l, j: (l, 0, j)),
        ],
        out_specs=pl.BlockSpec((1, bsz, d), lambda l, j: (l, 0, j)),
        compiler_params=_cparams(("arbitrary", "arbitrary")),
        name="ada_mod",
    )(c, w_ada, b_ada.reshape(depth, 1, n3))


def _rope_tables(seq, dim, period_lanes):
    inv = 1.0 / (ROPE_THETA ** (jnp.arange(0, dim, 2, dtype=_F32) / dim))
    ang = jnp.arange(seq, dtype=_F32)[:, None] * inv[None, :]
    cos, sin = jnp.cos(ang), jnp.sin(ang)
    cos_h = jnp.concatenate([cos, cos], axis=1)
    sin_h = jnp.concatenate([-sin, sin], axis=1)
    reps = period_lanes // dim
    return jnp.tile(cos_h, (1, reps)), jnp.tile(sin_h, (1, reps))


def _rope_piece(x, cos, sin_signed, dim):
    half = dim // 2
    first = (_lane_iota(x.shape) % dim) < half
    partner = jnp.where(first, pltpu.roll(x, LANES - half, 1), pltpu.roll(x, half, 1))
    return x * cos + partner * sin_signed


def _modulated(x_ref, mod_ref):
    x = x_ref[0]
    shift = mod_ref[0, 0:1, :]
    scale = mod_ref[0, 1:2, :]
    return (x * (1.0 + scale) + shift).astype(_BF16)


def _silu_gate(h, w_ref, col0, g_ref):
    for p in range(2):
        gate = _dot(h, w_ref[:, col0 + 512 * p:col0 + 512 * (p + 1)])
        g_ref[0, :, 512 * p:512 * (p + 1)] = (gate * jax.nn.sigmoid(gate)).astype(_BF16)


def _store_blocks_t(acc, out_ref, blk):
    tm, n = acc.shape
    for r in range(tm // blk):
        for p in range(n // LANES):
            out_ref[0, r, p * LANES:(p + 1) * LANES, :] = (
                acc[r * blk:(r + 1) * blk, p * LANES:(p + 1) * LANES].T.astype(_BF16))


EVEN_COLS = 3584


def _prep_w_even(w):
    offs = [0]
    for n in EVEN_SIZES:
        offs.append(offs[-1] + n)
    w = w.astype(_BF16)
    aq, ak, av, bq, bk, bv, gate = [w[:, offs[i]:offs[i + 1]] for i in range(len(EVEN_SIZES))]

    def dup(t):
        parts = []
        for g in range(B_KV_HEADS):
            blk = t[:, g * HEAD_DIM:(g + 1) * HEAD_DIM]
            parts += [blk, blk]
        return jnp.concatenate(parts, axis=1)

    return jnp.concatenate([aq, ak, bq, dup(bk), av, dup(bv), gate], axis=1)


def _inproj_even_body(x_ref, mod_ref, w_ref, cos_ref, sin_ref,
                      aqt_ref, ak_ref, bq_ref, bk_ref, avt_ref, bv_ref, g_ref):
    h = _modulated(x_ref, mod_ref)
    cos = cos_ref[...]
    sin = sin_ref[...]
    q_scale = HEAD_DIM ** -0.5

    def roped(col0, ncols, out_ref, scale, transposed=False):
        acc = _dot(h, w_ref[:, col0:col0 + ncols])
        for p in range(ncols // LANES):
            piece = _rope_piece(acc[:, p * LANES:(p + 1) * LANES], cos, sin, HEAD_DIM)
            if scale != 1.0:
                piece = piece * scale
            if transposed:
                out_ref[0, p * LANES:(p + 1) * LANES, :] = piece.T.astype(_BF16)
            else:
                out_ref[0, :, p * LANES:(p + 1) * LANES] = piece.astype(_BF16)

    roped(0, 512, aqt_ref, q_scale, transposed=True)
    roped(512, 512, ak_ref, 1.0)
    roped(1024, 512, bq_ref, q_scale)
    roped(1536, 256, bk_ref, 1.0)
    _store_blocks_t(_dot(h, w_ref[:, 1792:2304]), avt_ref, MOBA_BLOCK)
    bv_ref[0] = _dot(h, w_ref[:, 2304:2560]).astype(_BF16)
    _silu_gate(h, w_ref, 2560, g_ref)


def _inproj_even_call(x, mod, w, cos, sin):
    bsz, seq, d = x.shape
    tm = PROJ_TM
    row = lambda n: pl.BlockSpec((1, tm, n), lambda b, i: (b, i, 0))
    tok = lambda n: jax.ShapeDtypeStruct((bsz, seq, n), _BF16)
    per_tile = tm // MOBA_BLOCK
    return pl.pallas_call(
        _inproj_even_body,
        out_shape=[
            jax.ShapeDtypeStruct((bsz, 512, seq), _BF16), tok(512), tok(512), tok(256),
            jax.ShapeDtypeStruct((bsz, seq // MOBA_BLOCK, 512, MOBA_BLOCK), _BF16), tok(256), tok(1024),
        ],
        grid=(bsz, seq // tm),
        in_specs=[
            row(d),
            pl.BlockSpec((1, 3, d), lambda b, i: (b, 0, 0)),
            pl.BlockSpec((d, EVEN_COLS), lambda b, i: (0, 0)),
            pl.BlockSpec((tm, LANES), lambda b, i: (i, 0)),
            pl.BlockSpec((tm, LANES), lambda b, i: (i, 0)),
        ],
        out_specs=[
            pl.BlockSpec((1, 512, tm), lambda b, i: (b, 0, i)), row(512), row(512), row(256),
            pl.BlockSpec((1, per_tile, 512, MOBA_BLOCK), lambda b, i: (b, i, 0, 0)), row(256), row(1024),
        ],
        compiler_params=_cparams(("arbitrary", "arbitrary")),
        name="inproj_even",
    )(x, mod, w, cos, sin)


def _moba_body(qt_ref, k_ref, vt_ref, o_ref, km_ref, selb_ref, acc_ref):
    qi = pl.program_id(2)
    tq = MOBA_BLOCK
    nblk = km_ref.shape[0]
    npair = qt_ref.shape[1] // LANES
    pw = 2 * tq
    width = npair * pw

    @pl.when(qi == 0)
    def _():
        for r in range(nblk):
            blk_k = k_ref[0, r * tq:(r + 1) * tq, :].astype(_F32)
            km_ref[r:r + 1, :] = jnp.mean(blk_k, axis=0, keepdims=True)

    qcats = [_head_pair_rhs(qt_ref[0, p * LANES:(p + 1) * LANES, :], HEAD_DIM) for p in range(npair)]

    blk = _row_iota((nblk, width))
    blkf = blk.astype(_F32)
    valid = blk < qi
    gate = jnp.concatenate([_dot(km_ref[:, p * LANES:(p + 1) * LANES].astype(_BF16), qcats[p])
                            for p in range(npair)], axis=1)
    gate = jnp.where(valid, gate, -jnp.inf)
    sel = jnp.zeros((nblk, width), _F32)
    for _ in range(MOBA_TOPK):
        top = jnp.max(gate, axis=0, keepdims=True)
        first = jnp.min(jnp.where(gate == top, blkf, float(nblk)), axis=0, keepdims=True)
        pick = blkf == first
        sel = jnp.where(pick, 1.0, sel)
        gate = jnp.where(pick, -jnp.inf, gate)
    selb_ref[...] = jnp.where(valid & (sel > 0.0), 0.0, NEG_BIG)

    acc_ref[...] = jnp.zeros_like(acc_ref)

    def step(carry, blocks):
        m, l = carry
        ks = [k_ref[0, pl.ds(pl.multiple_of(j * tq, tq), tq), :] for j, _ in blocks]
        ss = [[_dot(kj[:, p * LANES:(p + 1) * LANES], qcats[p]) + bias(p) for kj, (_, bias) in zip(ks, blocks)]
              for p in range(npair)]
        ms, ls = [], []
        for p in range(npair):
            pcols = slice(p * pw, (p + 1) * pw)
            s0, s1 = ss[p]
            m_new = jnp.maximum(m[:, pcols], jnp.maximum(jnp.max(s0, axis=0, keepdims=True),
                                                         jnp.max(s1, axis=0, keepdims=True)))
            alpha = jnp.exp(m[:, pcols] - m_new)
            e0 = jnp.exp(s0 - m_new)
            e1 = jnp.exp(s1 - m_new)
            ls.append(alpha * l[:, pcols] + jnp.sum(e0, axis=0, keepdims=True) + jnp.sum(e1, axis=0, keepdims=True))
            ms.append(m_new)
            e = jnp.concatenate([e0.astype(_BF16), e1.astype(_BF16)], axis=0)
            for hh in range(2):
                rows = slice((2 * p + hh) * HEAD_DIM, (2 * p + hh + 1) * HEAD_DIM)
                cols = slice(hh * tq, (hh + 1) * tq)
                v_t = jnp.concatenate([vt_ref[0, j, rows, :] for j, _ in blocks], axis=1)
                acc_ref[rows, :] = alpha[:, cols] * acc_ref[rows, :] + _dot(v_t, e[:, cols])
        return jnp.concatenate(ms, axis=1), jnp.concatenate(ls, axis=1)

    def row_bias(j, live):
        row = jnp.where(live, selb_ref[pl.ds(j, 1), :], NEG_BIG)
        return lambda p: row[:, p * pw:(p + 1) * pw]

    def pair_body(t, carry):
        return step(carry, [(2 * t, row_bias(2 * t, True)), (2 * t + 1, row_bias(2 * t + 1, True))])

    carry = (jnp.full((1, width), NEG_BIG, _F32), jnp.zeros((1, width), _F32))
    carry = lax.fori_loop(0, qi // 2, pair_body, carry)
    causal = jnp.where(_row_iota((tq, pw)) <= _lane_iota((tq, pw)) % tq, 0.0, NEG_BIG)
    last = jnp.maximum(qi - 1, 0)
    _, l = step(carry, [(qi, lambda p: causal), (last, row_bias(last, qi % 2 == 1))])
    inv = 1.0 / l
    out_t = jnp.concatenate([acc_ref[h * HEAD_DIM:(h + 1) * HEAD_DIM, :] * inv[:, h * tq:(h + 1) * tq]
                             for h in range(2 * npair)], axis=0)
    o_ref[0] = out_t.T.astype(o_ref.dtype)


def _moba_call(q_t, k, v_t):
    bsz, seq, width = k.shape
    tq = MOBA_BLOCK
    nblk = seq // tq
    gw = MOBA_PAIRS * LANES
    return pl.pallas_call(
        _moba_body,
        out_shape=jax.ShapeDtypeStruct((bsz, seq, width), _BF16),
        grid=(bsz, width // gw, nblk),
        in_specs=[
            pl.BlockSpec((1, gw, tq), lambda b, h, i: (b, h, i)),
            pl.BlockSpec((1, seq, gw), lambda b, h, i: (b, 0, h)),
            pl.BlockSpec((1, nblk, gw, tq), lambda b, h, i: (b, 0, h, 0)),
        ],
        out_specs=pl.BlockSpec((1, tq, gw), lambda b, h, i: (b, i, h)),
        scratch_shapes=[
            pltpu.VMEM((nblk, gw), _F32),
            pltpu.VMEM((nblk, MOBA_PAIRS * 2 * tq), _F32),
            pltpu.VMEM((gw, tq), _F32),
        ],
        compiler_params=_cparams(("arbitrary", "arbitrary", "arbitrary")),
        name="moba",
    )(q_t, k, v_t)


def _swa_body(sink_ref, q_ref, kp_ref, kc_ref, vp_ref, vc_ref, o_ref):
    i = pl.program_id(1)
    w = SWA_WINDOW
    lane = _lane_iota((1, LANES))
    row = _row_iota((w, 2 * w))
    col = _lane_iota((w, 2 * w))
    in_window = (col > row) & (col <= row + w)
    pairs_per_kv = (B_HEADS // B_KV_HEADS) // 2
    nwin = q_ref.shape[1] // w

    def band(prev_ref, cur_ref, n, g):
        glanes = slice(g * LANES, (g + 1) * LANES)
        before = prev_ref[0, :, glanes] if n == 0 else cur_ref[0, (n - 1) * w:n * w, glanes]
        return jnp.concatenate([before, cur_ref[0, n * w:(n + 1) * w, glanes]], axis=0)

    ss, vs = [], []
    for n in range(nwin):
        ks = [band(kp_ref, kc_ref, n, g) for g in range(B_KV_HEADS)]
        vs.append([band(vp_ref, vc_ref, n, g) for g in range(B_KV_HEADS)])
        for h in range(B_HEADS):
            qp = q_ref[0, n * w:(n + 1) * w, (h // 2) * LANES:(h // 2 + 1) * LANES]
            qh = jnp.where((lane // HEAD_DIM) == (h % 2), qp, jnp.zeros_like(qp))
            ss.append(_dot_t(qh, ks[(h // 2) // pairs_per_kv]))
    for n in range(nwin):
        mask = in_window & ((col >= w) | (i > 0)) if n == 0 else in_window
        outs = []
        for h in range(B_HEADS):
            sink = sink_ref[h]
            s = jnp.where(mask, ss[n * B_HEADS + h], -jnp.inf)
            m = jnp.maximum(jnp.max(s, axis=1, keepdims=True), sink)
            e = jnp.exp(s - m)
            l = jnp.sum(e, axis=1, keepdims=True) + jnp.exp(sink - m)
            outs.append(_dot(e.astype(_BF16), vs[n][(h // 2) // pairs_per_kv]) / l)
        for p in range(B_HEADS // 2):
            o_ref[0, n * w:(n + 1) * w, p * LANES:(p + 1) * LANES] = jnp.where(
                lane < HEAD_DIM, outs[2 * p], outs[2 * p + 1]).astype(o_ref.dtype)


def _swa_call(q, kdup, vdup, sinks):
    bsz, seq, width = q.shape
    w = SWA_WINDOW
    kvw = kdup.shape[2]
    nwin = SWA_WINDOWS_PER_STEP
    tq = nwin * w
    prev = pl.BlockSpec((1, w, kvw), lambda b, i: (b, jnp.maximum(i * nwin - 1, 0), 0))
    cur = pl.BlockSpec((1, tq, kvw), lambda b, i: (b, i, 0))
    return pl.pallas_call(
        _swa_body,
        out_shape=jax.ShapeDtypeStruct((bsz, seq, width), _BF16),
        grid=(bsz, seq // tq),
        in_specs=[
            pl.BlockSpec(memory_space=pltpu.SMEM),
            pl.BlockSpec((1, tq, width), lambda b, i: (b, i, 0)),
            prev, cur, prev, cur,
        ],
        out_specs=pl.BlockSpec((1, tq, width), lambda b, i: (b, i, 0)),
        compiler_params=_cparams(("arbitrary", "arbitrary")),
        name="swa_sink",
    )(sinks, q, kdup, kdup, vdup, vdup)


def _outproj_body(o1_ref, o2_ref, g_ref, x_ref, mod_ref, w_ref, lng_ref, lnb_ref, y_ref):
    half = o1_ref.shape[2]
    a1 = o1_ref[0] * g_ref[0, :, :half]
    a2 = o2_ref[0] * g_ref[0, :, half:]
    y = _dot(a1, w_ref[:half, :]) + _dot(a2, w_ref[half:, :])
    gate = mod_ref[0, 2:3, :]
    z = DEEPNORM_ALPHA * x_ref[0] + (1.0 + gate) * y
    mu = jnp.mean(z, axis=1, keepdims=True)
    zc = z - mu
    var = jnp.mean(zc * zc, axis=1, keepdims=True)
    y_ref[0] = zc * lax.rsqrt(var + LN_EPS) * lng_ref[...] + lnb_ref[...]


def _outproj_call(o1, o2, g, x, mod, w_out, ln_g, ln_b):
    bsz, seq, d = x.shape
    tm = PROJ_TM
    half = o1.shape[2]
    row = lambda n: pl.BlockSpec((1, tm, n), lambda b, i: (b, i, 0))
    return pl.pallas_call(
        _outproj_body,
        out_shape=jax.ShapeDtypeStruct((bsz, seq, d), _F32),
        grid=(bsz, seq // tm),
        in_specs=[
            row(half), row(half), row(2 * half), row(d),
            pl.BlockSpec((1, 3, d), lambda b, i: (b, 0, 0)),
            pl.BlockSpec((2 * half, d), lambda b, i: (0, 0)),
            pl.BlockSpec((1, d), lambda b, i: (0, 0)),
            pl.BlockSpec((1, d), lambda b, i: (0, 0)),
        ],
        out_specs=row(d),
        compiler_params=_cparams(("arbitrary", "arbitrary")),
        name="outproj_deepnorm",
    )(o1, o2, g, x, mod, w_out.astype(_BF16), ln_g.reshape(1, d), ln_b.reshape(1, d))


ODD_COLS = 4096


def _prep_w_odd(w):
    offs = [0]
    for n in ODD_SIZES:
        offs.append(offs[-1] + n)
    w = w.astype(_BF16)
    cqn, cqr, ckv, ckr, iq, ik, iw, dq, dk, dv, gate = [w[:, offs[i]:offs[i + 1]] for i in range(len(ODD_SIZES))]
    iw_blk = jnp.concatenate([iw, jnp.zeros((w.shape[0], LANES - IDX_HEADS), w.dtype)], axis=1)
    return jnp.concatenate([cqn, cqr, ckv, jnp.tile(ckr, (1, 4)), iq, jnp.tile(ik, (1, 4)), iw_blk,
                            dq, dk, dv, gate], axis=1)


def _inproj_odd_body(x_ref, mod_ref, w_ref, cosr_ref, sinr_ref, cosi_ref, sini_ref, kvg_ref,
                     cqn_ref, cqr_ref, kvc_ref, ckvt_ref, iq_ref, ik_ref, iw_ref,
                     dqt_ref, dk_ref, dvt_ref, g_ref):
    h = _modulated(x_ref, mod_ref)
    cosr, sinr = cosr_ref[...], sinr_ref[...]
    cosi, sini = cosi_ref[...], sini_ref[...]

    acc = _dot(h, w_ref[:, 0:512])
    for p in range(4):
        cqn_ref[0, p * LANES:(p + 1) * LANES, :] = acc[:, p * LANES:(p + 1) * LANES].T.astype(_BF16)
    acc = _dot(h, w_ref[:, 512:1024])
    for p in range(2):
        piece = _rope_piece(acc[:, p * LANES:(p + 1) * LANES], cosr, sinr, C_ROPE_DIM)
        cqr_ref[0, p * LANES:(p + 1) * LANES, :] = piece.T.astype(_BF16)
    ckv = acc[:, 256:384]
    ckv = ckv * lax.rsqrt(jnp.mean(ckv * ckv, axis=1, keepdims=True) + LN_EPS) * kvg_ref[...]
    kvc_ref[0, :, 0:LANES] = ckv.astype(_BF16)
    ckvt_ref[0, 0] = ckv.T.astype(_BF16)
    kvc_ref[0, :, LANES:2 * LANES] = _rope_piece(acc[:, 384:512], cosr, sinr, C_ROPE_DIM).astype(_BF16)
    acc = _dot(h, w_ref[:, 1024:1536])
    for p in range(2):
        piece = _rope_piece(acc[:, p * LANES:(p + 1) * LANES], cosi, sini, IDX_DIM)
        iq_ref[0, p * LANES:(p + 1) * LANES, :] = piece.T.astype(_BF16)
    ik_ref[0] = _rope_piece(acc[:, 256:384], cosi, sini, IDX_DIM).astype(_BF16)
    iw_ref[0] = acc[:, 384:512].T[0:IDX_HEADS, :]
    acc = _dot(h, w_ref[:, 1536:2048]) * (HEAD_DIM ** -0.5)
    for p in range(4):
        dqt_ref[0, p * LANES:(p + 1) * LANES, :] = acc[:, p * LANES:(p + 1) * LANES].T.astype(_BF16)
    dk_ref[0] = _dot(h, w_ref[:, 2048:2560]).astype(_BF16)
    _store_blocks_t(_dot(h, w_ref[:, 2560:3072]), dvt_ref, SB_T)
    _silu_gate(h, w_ref, 3072, g_ref)


def _inproj_odd_call(x, mod, w, rope_r, rope_i, kv_g):
    bsz, seq, d = x.shape
    tm = DSA_CK
    row = lambda n: pl.BlockSpec((1, tm, n), lambda b, i: (b, i, 0))
    col = lambda n: pl.BlockSpec((1, n, tm), lambda b, i: (b, 0, i))
    tab = pl.BlockSpec((tm, LANES), lambda b, i: (i, 0))
    tok = lambda n, dt: jax.ShapeDtypeStruct((bsz, seq, n), dt)
    feat = lambda n, dt: jax.ShapeDtypeStruct((bsz, n, seq), dt)
    return pl.pallas_call(
        _inproj_odd_body,
        out_shape=[
            feat(512, _BF16), feat(256, _BF16), tok(256, _BF16),
            jax.ShapeDtypeStruct((bsz, seq // tm, C_KV_LATENT, tm), _BF16),
            feat(256, _BF16), tok(128, _BF16), feat(IDX_HEADS, _F32),
            feat(512, _BF16), tok(512, _BF16),
            jax.ShapeDtypeStruct((bsz, seq // SB_T, 512, SB_T), _BF16), tok(1024, _BF16),
        ],
        grid=(bsz, seq // tm),
        in_specs=[
            row(d),
            pl.BlockSpec((1, 3, d), lambda b, i: (b, 0, 0)),
            pl.BlockSpec((d, ODD_COLS), lambda b, i: (0, 0)),
            tab, tab, tab, tab,
            pl.BlockSpec((1, LANES), lambda b, i: (0, 0)),
        ],
        out_specs=[
            col(512), col(256), row(256),
            pl.BlockSpec((1, 1, C_KV_LATENT, tm), lambda b, i: (b, i, 0, 0)),
            col(256), row(128), col(IDX_HEADS),
            col(512), row(512),
            pl.BlockSpec((1, tm // SB_T, 512, SB_T), lambda b, i: (b, i, 0, 0)), row(1024),
        ],
        compiler_params=_cparams(("arbitrary", "arbitrary")),
        name="inproj_odd",
    )(x, mod, w, rope_r[0], rope_r[1], rope_i[0], rope_i[1], kv_g.reshape(1, LANES))


def _dsa_body(cqn_ref, cqr_ref, iq_ref, iw_ref, kvc_ref, ckvt_ref, ik_ref, wuk_ref, wuvt_ref, o_ref,
              qct_ref, iqt_ref, sc_ref, acc_ref):
    i = pl.program_id(1)
    tq, ck = DSA_TQ, DSA_CK
    seq = kvc_ref.shape[1]
    n_sel = float(min(DSA_TOPK, seq // 4))
    nch = i // (ck // tq) + 1
    att_scale = (C_NOPE_DIM + C_ROPE_DIM) ** -0.5
    idx_scale = (IDX_DIM * IDX_HEADS) ** -0.5
    hq = C_HEADS * tq

    row128 = _row_iota((LANES, tq))
    for h in range(C_HEADS):
        cols = slice(h * tq, (h + 1) * tq)
        qn_t = cqn_ref[0, h * C_NOPE_DIM:(h + 1) * C_NOPE_DIM, :]
        qlat_t = _dot(wuk_ref[h], qn_t) * att_scale
        qct_ref[0:C_KV_LATENT, cols] = qlat_t.astype(_BF16)
        qr_t = cqr_ref[0, h * C_ROPE_DIM:(h + 1) * C_ROPE_DIM, :].astype(_F32) * att_scale
        qct_ref[C_KV_LATENT:C_KV_LATENT + C_ROPE_DIM, cols] = qr_t.astype(_BF16)
        qct_ref[C_KV_LATENT + C_ROPE_DIM:, cols] = jnp.zeros((LANES - C_ROPE_DIM, tq), _BF16)
        blk4 = h // 4
        iq_t = iq_ref[0, blk4 * LANES:(blk4 + 1) * LANES, :]
        iqt_ref[:, cols] = jnp.where((row128 // IDX_DIM) == (h % 4), iq_t, jnp.zeros_like(iq_t))

    t_pos = i * tq + _lane_iota((ck, tq))
    key_row = _row_iota((ck, tq))

    def score_body(c, carry):
        vmax, vmin = carry
        off = pl.multiple_of(c * ck, ck)
        rel = _dot(ik_ref[0, pl.ds(off, ck), :], iqt_ref[...])
        score = jnp.zeros((ck, tq), _F32)
        for h in range(IDX_HEADS):
            score = score + jnp.maximum(rel[:, h * tq:(h + 1) * tq], 0.0) * iw_ref[0, h:h + 1, :]
        score = score * idx_scale
        low = jnp.where(off + key_row <= t_pos, score, -jnp.inf)
        sc_ref[c] = low
        vmax = jnp.maximum(vmax, jnp.max(low, axis=0, keepdims=True))
        vmin = jnp.minimum(vmin, jnp.min(score, axis=0, keepdims=True))
        return vmax, vmin

    vmax, vmin = lax.fori_loop(0, nch, score_body,
                               (jnp.full((1, tq), -jnp.inf, _F32), jnp.full((1, tq), jnp.inf, _F32)))

    def count(pred):
        rows_acc = 8 * (8 * LANES // tq)

        slab_row = _row_iota((rows_acc, tq))

        def body(c, acc):
            for r in range(ck // rows_acc):
                rows = slice(r * rows_acc, (r + 1) * rows_acc)
                acc = acc + jnp.where(pred(sc_ref[c, rows, :], c * ck + r * rows_acc + slab_row), 1.0, 0.0)
            return acc

        acc = lax.fori_loop(0, nch, body, jnp.zeros((rows_acc, tq), _F32))
        return jnp.sum(acc, axis=0, keepdims=True)

    def ordered_float(u):
        key = u ^ INT_MIN
        return pltpu.bitcast(key ^ ((key >> 31) & 0x7FFFFFFF), _F32)

    n_valid = (i * tq + _lane_iota((1, tq)) + 1).astype(_F32)
    few = n_valid < n_sel
    c_ge0 = count(lambda sc, pos: sc >= 0.0)
    c_gt0 = count(lambda sc, pos: sc > 0.0)
    zero_tie = (c_gt0 < n_sel) & (c_ge0 >= n_sel)
    above = c_gt0 >= n_sel

    def bisect_body(_, carry):
        lo, hi, cnt_lo = carry
        mid = lo + 0.5 * (hi - lo)
        cnt = count(lambda sc, pos: sc >= mid)
        ok = cnt >= n_sel
        return jnp.where(ok, mid, lo), jnp.where(ok, hi, mid), jnp.where(ok, cnt, cnt_lo)

    def all_resolved(cnt_lo):
        done = few | zero_tie | (cnt_lo == n_sel)
        return jnp.min(jnp.where(done, 1.0, 0.0)) > 0.0

    state = (jnp.where(above, 0.0, vmin), jnp.where(above, vmax, 0.0), jnp.where(above, c_ge0, n_valid))
    state = lax.fori_loop(0, BISECT_STEPS, bisect_body, state)
    state = lax.cond(all_resolved(state[2]), lambda s: s,
                     lambda s: lax.fori_loop(0, BISECT_EXTRA, bisect_body, s), state)
    lo, _, cnt_lo = state

    def exact_search():
        def bit_body(b, carry):
            t, cnt_t = carry
            cand = t | jnp.left_shift(jnp.int32(1), 31 - b)
            cand_f = ordered_float(cand)
            cnt = count(lambda sc, pos: sc >= cand_f)
            ok = cnt >= n_sel
            return jnp.where(ok, cand, t), jnp.where(ok, cnt, cnt_t)

        t0 = jnp.zeros((1, tq), jnp.int32)
        t, cnt_t = lax.fori_loop(0, 32, bit_body, (t0, jnp.full((1, tq), float(seq), _F32)))
        return jnp.where(t == 0, -jnp.inf, ordered_float(t)), cnt_t

    thr, cnt_thr = lax.cond(
        all_resolved(cnt_lo),
        lambda: (jnp.where(few, -jnp.inf, jnp.where(zero_tie, 0.0, lo)), jnp.where(zero_tie, c_ge0, cnt_lo)),
        exact_search)

    excess = jnp.max(jnp.where((cnt_thr > n_sel) & (thr > -jnp.inf), 1.0, 0.0)) > 0.0

    def plain_bias():
        def body(c, _):
            chosen = (sc_ref[c] >= thr) & (c * ck + key_row <= t_pos)
            sc_ref[c] = jnp.where(chosen, 0.0, NEG_BIG)
            return 0

        lax.fori_loop(0, nch, body, 0)

    def tie_bias():
        room = n_sel - count(lambda sc, pos: sc > thr)
        upto = jnp.where(_lane_iota((ck, ck)) <= _row_iota((ck, ck)), 1.0, 0.0).astype(_BF16)

        def body(c, seen):
            sc = sc_ref[c]
            causal = c * ck + key_row <= t_pos
            tied = (sc == thr) & causal
            one = jnp.where(tied, 1.0, 0.0)
            rank = _dot(upto, one.astype(_BF16)) + seen
            chosen = ((sc > thr) & causal) | (tied & (rank <= room))
            sc_ref[c] = jnp.where(chosen, 0.0, NEG_BIG)
            return seen + jnp.sum(one, axis=0, keepdims=True)

        lax.fori_loop(0, nch, body, jnp.zeros((1, tq), _F32))

    lax.cond(excess, tie_bias, plain_bias)

    acc_ref[...] = jnp.zeros_like(acc_ref)
    grp = 2 * tq
    gcols = [slice(g * grp, (g + 1) * grp) for g in range(hq // grp)]

    def att_body(c, carry):
        off = pl.multiple_of(c * ck, ck)
        kv = kvc_ref[0, pl.ds(off, ck), :]
        bias = sc_ref[c]
        bias2 = jnp.concatenate([bias, bias], axis=1)
        ss = [_dot(kv, qct_ref[:, cols]) + bias2 for cols in gcols]
        ms, ls, es, alphas = [], [], [], []
        for g, cols in enumerate(gcols):
            m, l = carry[0][:, cols], carry[1][:, cols]
            m_new = jnp.maximum(m, jnp.max(ss[g], axis=0, keepdims=True))
            alpha = jnp.exp(m - m_new)
            e = jnp.exp(ss[g] - m_new)
            ls.append(alpha * l + jnp.sum(e, axis=0, keepdims=True))
            ms.append(m_new)
            alphas.append(alpha)
            es.append(e.astype(_BF16))
        kv_t = ckvt_ref[0, c]
        for g, cols in enumerate(gcols):
            acc_ref[:, cols] = alphas[g] * acc_ref[:, cols] + _dot(kv_t, es[g])
        return jnp.concatenate(ms, axis=1), jnp.concatenate(ls, axis=1)

    m0 = jnp.full((1, hq), NEG_BIG, _F32)
    l0 = jnp.zeros((1, hq), _F32)
    _, l = lax.fori_loop(0, nch, att_body, (m0, l0))
    o_lat_t = (acc_ref[...] / l).astype(_BF16)
    outs = [_dot(wuvt_ref[h], o_lat_t[:, h * tq:(h + 1) * tq]) for h in range(C_HEADS)]
    o_ref[0] = jnp.concatenate(outs, axis=0).T.astype(o_ref.dtype)


def _dsa_call(cqn_t, cqr_t, iq_t, iw_t, kvc, ckv_t, ik, w_uk, wuv_t):
    bsz, seq, _ = kvc.shape
    tq = DSA_TQ
    qcol = lambda n: pl.BlockSpec((1, n, tq), lambda b, i: (b, 0, i))
    full = lambda n: pl.BlockSpec((1, seq, n), lambda b, i: (b, 0, 0))
    whole = lambda a: pl.BlockSpec(a.shape, lambda b, i: (0,) * a.ndim)
    width = C_HEADS * C_V_DIM
    return pl.pallas_call(
        _dsa_body,
        out_shape=jax.ShapeDtypeStruct((bsz, seq, width), _BF16),
        grid=(bsz, seq // tq),
        in_specs=[
            qcol(512), qcol(256), qcol(256), qcol(IDX_HEADS), full(256),
            pl.BlockSpec((1,) + ckv_t.shape[1:], lambda b, i: (b, 0, 0, 0)),
            full(128), whole(w_uk), whole(wuv_t),
        ],
        out_specs=pl.BlockSpec((1, tq, width), lambda b, i: (b, i, 0)),
        scratch_shapes=[
            pltpu.VMEM((2 * LANES, C_HEADS * tq), _BF16),
            pltpu.VMEM((LANES, IDX_HEADS * tq), _BF16),
            pltpu.VMEM((seq // DSA_CK, DSA_CK, tq), _F32),
            pltpu.VMEM((C_KV_LATENT, C_HEADS * tq), _F32),
        ],
        compiler_params=_cparams(("arbitrary", "arbitrary")),
        name="dsa",
    )(cqn_t, cqr_t, iq_t, iw_t, kvc, ckv_t, ik, w_uk, wuv_t)


def _sb_body(qt_ref, k_ref, vt_ref, o_ref, acc_ref):
    i = pl.program_id(1)
    t = SB_T
    npair = D_HEADS // 2
    width = 2 * t
    qcats = [_head_pair_rhs(qt_ref[0, p * LANES:(p + 1) * LANES, :], HEAD_DIM) for p in range(npair)]
    key_row = _row_iota((t, width))
    q_lane = _lane_iota((t, width)) % t
    later = jnp.where(_lane_iota((t, t)) > _row_iota((t, t)), 1.0, 0.0).astype(_BF16)
    later2 = jnp.concatenate([later, later], axis=1)
    acc_ref[...] = jnp.zeros_like(acc_ref)

    def blocks(js, rest, masks):
        ks = [k_ref[0, pl.ds(pl.multiple_of(j * t, t), t), :] for j in js]
        zs = [[_dot(kj[:, p * LANES:(p + 1) * LANES], qcats[p]) for p in range(npair)] for kj in ks]
        rests = []
        for p in range(npair):
            run = rest[:, p * width:(p + 1) * width]
            for n, j in enumerate(js):
                z = zs[n][p]
                log_keep = -(jnp.maximum(z, 0.0) + jnp.log(1.0 + jnp.exp(-jnp.abs(z))))
                log_beta = z + log_keep
                if masks[n] is not None:
                    log_keep = jnp.where(masks[n], log_keep, 0.0)
                hi = log_keep.astype(_BF16)
                lo = (log_keep - hi.astype(_F32)).astype(_BF16)
                a = jnp.exp(log_beta + (_dot(later2, jnp.concatenate([hi, lo], axis=0)) + run))
                if masks[n] is not None:
                    a = jnp.where(masks[n], a, 0.0)
                prows = slice(p * LANES, (p + 1) * LANES)
                pv = _dot(vt_ref[0, j, prows, :], a.astype(_BF16))
                for hh in range(2):
                    rows = slice((2 * p + hh) * HEAD_DIM, (2 * p + hh + 1) * HEAD_DIM)
                    acc_ref[rows, :] = acc_ref[rows, :] + pv[hh * HEAD_DIM:(hh + 1) * HEAD_DIM, hh * t:(hh + 1) * t]
                run = run + jnp.sum(log_keep, axis=0, keepdims=True)
            rests.append(run)
        return jnp.concatenate(rests, axis=1)

    rest = blocks([i], jnp.zeros((1, npair * width), _F32), [key_row < q_lane])

    def cond(carry):
        j, _, top = carry
        return (j >= 0) & (top > SB_UNDERFLOW)

    def body(carry):
        j, rest, _ = carry
        second_ok = jnp.broadcast_to(j >= 1, (t, width))
        rest = blocks([j, jnp.maximum(j - 1, 0)], rest, [None, second_ok])
        return j - 2, rest, jnp.max(rest)

    lax.while_loop(cond, body, (i - 1, rest, jnp.max(rest)))
    o_ref[0] = acc_ref[...].T.astype(o_ref.dtype)


def _sb_call(q_t, k, v_t):
    bsz, seq, width = k.shape
    t = SB_T
    return pl.pallas_call(
        _sb_body,
        out_shape=jax.ShapeDtypeStruct((bsz, seq, width), _BF16),
        grid=(bsz, seq // t),
        in_specs=[
            pl.BlockSpec((1, width, t), lambda b, i: (b, 0, i)),
            pl.BlockSpec((1, seq, width), lambda b, i: (b, 0, 0)),
            pl.BlockSpec((1, seq // t, width, t), lambda b, i: (b, 0, 0, 0)),
        ],
        out_specs=pl.BlockSpec((1, t, width), lambda b, i: (b, i, 0)),
        scratch_shapes=[pltpu.VMEM((width, t), _F32)],
        compiler_params=_cparams(("arbitrary", "arbitrary")),
        name="stick_breaking",
    )(q_t, k, v_t)


def _even_layer(x, mod, w_in, sinks, w_out, ln_g, ln_b, rope_h):
    aq_t, ak, bq, bk, av_t, bv, g = _inproj_even_call(x, mod, _prep_w_even(w_in), *rope_h)
    oa = _moba_call(aq_t, ak, av_t)
    ob = _swa_call(bq, bk, bv, sinks)
    return _outproj_call(oa, ob, g, x, mod, w_out, ln_g, ln_b)


def _odd_layer(x, mod, w_in, kv_g, w_uk, w_uv, w_out, ln_g, ln_b, rope_r, rope_i):
    cqn_t, cqr_t, kvc, ckv_t, iq_t, ik, iw_t, dq_t, dk, dv_t, g = _inproj_odd_call(
        x, mod, _prep_w_odd(w_in), rope_r, rope_i, kv_g)
    oc = _dsa_call(cqn_t, cqr_t, iq_t, iw_t, kvc, ckv_t, ik,
                   w_uk.astype(_BF16), w_uv.transpose(0, 2, 1).astype(_BF16))
    od = _sb_call(dq_t, dk, dv_t)
    return _outproj_call(oc, od, g, x, mod, w_out, ln_g, ln_b)


def kernel(x, c, w_ada, b_ada, w_in_even, sink_logits, w_in_odd, kv_norm_g, w_uk, w_uv, w_out, ln_g, ln_b):
    bsz, seq, d = x.shape
    rope_h = _rope_tables(seq, HEAD_DIM, LANES)
    rope_r = _rope_tables(seq, C_ROPE_DIM, LANES)
    rope_i = _rope_tables(seq, IDX_DIM, LANES)
    mods = _ada_call(c, w_ada, b_ada).reshape(DEPTH, bsz, 3, d)
    for layer in range(DEPTH):
        mod = mods[layer]
        j = layer // 2
        if layer % 2 == 0:
            x = _even_layer(x, mod, w_in_even[j], sink_logits[j], w_out[layer], ln_g[layer], ln_b[layer], rope_h)
        else:
            x = _odd_layer(x, mod, w_in_odd[j], kv_norm_g[j], w_uk[j], w_uv[j], w_out[layer],
                           ln_g[layer], ln_b[layer], rope_r, rope_i)
    return x
```

```python
import jax
import jax.numpy as jnp
from jax import lax
from jax.experimental import pallas as pl
from jax.experimental.pallas import tpu as pltpu

D_MODEL = 1024
DEPTH = 2
HEAD_DIM = 64
ROPE_THETA = 10000.0
LN_EPS = 1e-5
A_HEADS = 8
MOBA_BLOCK = 256
MOBA_TOPK = 3
B_HEADS = 8
B_KV_HEADS = 2
SWA_WINDOW = 128
C_HEADS = 8
C_NOPE_DIM = 64
C_ROPE_DIM = 32
C_V_DIM = 64
C_KV_LATENT = 128
IDX_HEADS = 8
IDX_DIM = 32
DSA_TOPK = 256
D_HEADS = 8
MIX_WIDTH = 1024
EVEN_SIZES = (512, 512, 512, 512, 128, 128, 1024)
ODD_SIZES = (512, 256, 128, 32, 256, 32, 8, 512, 512, 512, 1024)
DEEPNORM_ALPHA = (2 * DEPTH) ** 0.25

LANES = 128
NEG_BIG = -1e30
INT_MIN = -2 ** 31
SB_UNDERFLOW = -104.0

PROJ_TM = 512
DSA_TQ = 512
DSA_CK = 512
SB_T = 128
MOBA_PAIRS = 4
SWA_WINDOWS_PER_STEP = 2
BISECT_STEPS = 20
BISECT_EXTRA = 4
VMEM_LIMIT = 48 * 1024 * 1024

_BF16 = jnp.bfloat16
_F32 = jnp.float32


def _cparams(sem):
    return pltpu.CompilerParams(dimension_semantics=sem, vmem_limit_bytes=VMEM_LIMIT)


def _dot_t(a, b):
    return lax.dot_general(a, b, (((1,), (1,)), ((), ())), preferred_element_type=_F32)


def _dot(a, b):
    return jnp.dot(a, b, preferred_element_type=_F32)


def _lane_iota(shape):
    return lax.broadcasted_iota(jnp.int32, shape, len(shape) - 1)


def _row_iota(shape):
    return lax.broadcasted_iota(jnp.int32, shape, len(shape) - 2)


def _head_pair_rhs(q_t, head_dim):
    frow = _row_iota(q_t.shape)
    return jnp.concatenate(
        [jnp.where((frow // head_dim) == hh, q_t, jnp.zeros_like(q_t)) for hh in range(2)], axis=1)


def _ada_body(c_ref, w_ref, b_ref, o_ref):
    c = c_ref[...]
    cond = c * jax.nn.sigmoid(c)
    o_ref[0] = _dot(cond, w_ref[0]) + b_ref[0]


def _ada_call(c, w_ada, b_ada):
    depth, d, n3 = w_ada.shape
    bsz = c.shape[0]
    nb = n3 // d
    return pl.pallas_call(
        _ada_body,
        out_shape=jax.ShapeDtypeStruct((depth, bsz, n3), _F32),
        grid=(depth, nb),
        in_specs=[
            pl.BlockSpec((bsz, d), lambda l, j: (0, 0)),
            pl.BlockSpec((1, d, d), lambda l, j: (l, 0, j)),
            pl.BlockSpec((1, 1, d), lambda l, j: (l, 0, j)),
        ],
        out_specs=pl.BlockSpec((1, bsz, d), lambda l, j: (l, 0, j)),
        compiler_params=_cparams(("arbitrary", "arbitrary")),
        name="ada_mod",
    )(c, w_ada, b_ada.reshape(depth, 1, n3))


def _rope_tables(seq, dim, period_lanes):
    inv = 1.0 / (ROPE_THETA ** (jnp.arange(0, dim, 2, dtype=_F32) / dim))
    ang = jnp.arange(seq, dtype=_F32)[:, None] * inv[None, :]
    cos, sin = jnp.cos(ang), jnp.sin(ang)
    cos_h = jnp.concatenate([cos, cos], axis=1)
    sin_h = jnp.concatenate([-sin, sin], axis=1)
    reps = period_lanes // dim
    return jnp.tile(cos_h, (1, reps)), jnp.tile(sin_h, (1, reps))


def _rope_piece(x, cos, sin_signed, dim):
    half = dim // 2
    first = (_lane_iota(x.shape) % dim) < half
    partner = jnp.where(first, pltpu.roll(x, LANES - half, 1), pltpu.roll(x, half, 1))
    return x * cos + partner * sin_signed


def _modulated(x_ref, mod_ref):
    x = x_ref[0]
    shift = mod_ref[0, 0:1, :]
    scale = mod_ref[0, 1:2, :]
    return (x * (1.0 + scale) + shift).astype(_BF16)


def _silu_gate(h, w_ref, col0, g_ref):
    for p in range(2):
        gate = _dot(h, w_ref[:, col0 + 512 * p:col0 + 512 * (p + 1)])
        g_ref[0, :, 512 * p:512 * (p + 1)] = (gate * jax.nn.sigmoid(gate)).astype(_BF16)


def _store_blocks_t(acc, out_ref, blk):
    tm, n = acc.shape
    for r in range(tm // blk):
        for p in range(n // LANES):
            out_ref[0, r, p * LANES:(p + 1) * LANES, :] = (
                acc[r * blk:(r + 1) * blk, p * LANES:(p + 1) * LANES].T.astype(_BF16))


EVEN_COLS = 3584


def _prep_w_even(w):
    offs = [0]
    for n in EVEN_SIZES:
        offs.append(offs[-1] + n)
    w = w.astype(_BF16)
    aq, ak, av, bq, bk, bv, gate = [w[:, offs[i]:offs[i + 1]] for i in range(len(EVEN_SIZES))]

    def dup(t):
        parts = []
        for g in range(B_KV_HEADS):
            blk = t[:, g * HEAD_DIM:(g + 1) * HEAD_DIM]
            parts += [blk, blk]
        return jnp.concatenate(parts, axis=1)

    return jnp.concatenate([aq, ak, bq, dup(bk), av, dup(bv), gate], axis=1)


def _inproj_even_body(x_ref, mod_ref, w_ref, cos_ref, sin_ref,
                      aqt_ref, ak_ref, bq_ref, bk_ref, avt_ref, bv_ref, g_ref):
    h = _modulated(x_ref, mod_ref)
    cos = cos_ref[...]
    sin = sin_ref[...]
    q_scale = HEAD_DIM ** -0.5

    def roped(col0, ncols, out_ref, scale, transposed=False):
        acc = _dot(h, w_ref[:, col0:col0 + ncols])
        for p in range(ncols // LANES):
            piece = _rope_piece(acc[:, p * LANES:(p + 1) * LANES], cos, sin, HEAD_DIM)
            if scale != 1.0:
                piece = piece * scale
            if transposed:
                out_ref[0, p * LANES:(p + 1) * LANES, :] = piece.T.astype(_BF16)
            else:
                out_ref[0, :, p * LANES:(p + 1) * LANES] = piece.astype(_BF16)

    roped(0, 512, aqt_ref, q_scale, transposed=True)
    roped(512, 512, ak_ref, 1.0)
    roped(1024, 512, bq_ref, q_scale)
    roped(1536, 256, bk_ref, 1.0)
    _store_blocks_t(_dot(h, w_ref[:, 1792:2304]), avt_ref, MOBA_BLOCK)
    bv_ref[0] = _dot(h, w_ref[:, 2304:2560]).astype(_BF16)
    _silu_gate(h, w_ref, 2560, g_ref)


def _inproj_even_call(x, mod, w, cos, sin):
    bsz, seq, d = x.shape
    tm = PROJ_TM
    row = lambda n: pl.BlockSpec((1, tm, n), lambda b, i: (b, i, 0))
    tok = lambda n: jax.ShapeDtypeStruct((bsz, seq, n), _BF16)
    per_tile = tm // MOBA_BLOCK
    return pl.pallas_call(
        _inproj_even_body,
        out_shape=[
            jax.ShapeDtypeStruct((bsz, 512, seq), _BF16), tok(512), tok(512), tok(256),
            jax.ShapeDtypeStruct((bsz, seq // MOBA_BLOCK, 512, MOBA_BLOCK), _BF16), tok(256), tok(1024),
        ],
        grid=(bsz, seq // tm),
        in_specs=[
            row(d),
            pl.BlockSpec((1, 3, d), lambda b, i: (b, 0, 0)),
            pl.BlockSpec((d, EVEN_COLS), lambda b, i: (0, 0)),
            pl.BlockSpec((tm, LANES), lambda b, i: (i, 0)),
            pl.BlockSpec((tm, LANES), lambda b, i: (i, 0)),
        ],
        out_specs=[
            pl.BlockSpec((1, 512, tm), lambda b, i: (b, 0, i)), row(512), row(512), row(256),
            pl.BlockSpec((1, per_tile, 512, MOBA_BLOCK), lambda b, i: (b, i, 0, 0)), row(256), row(1024),
        ],
        compiler_params=_cparams(("arbitrary", "arbitrary")),
        name="inproj_even",
    )(x, mod, w, cos, sin)


def _moba_body(qt_ref, k_ref, vt_ref, o_ref, km_ref, selb_ref, acc_ref):
    qi = pl.program_id(2)
    tq = MOBA_BLOCK
    nblk = km_ref.shape[0]
    npair = qt_ref.shape[1] // LANES
    pw = 2 * tq
    width = npair * pw

    @pl.when(qi == 0)
    def _():
        for r in range(nblk):
            blk_k = k_ref[0, r * tq:(r + 1) * tq, :].astype(_F32)
            km_ref[r:r + 1, :] = jnp.mean(blk_k, axis=0, keepdims=True)

    qcats = [_head_pair_rhs(qt_ref[0, p * LANES:(p + 1) * LANES, :], HEAD_DIM) for p in range(npair)]

    blk = _row_iota((nblk, width))
    blkf = blk.astype(_F32)
    valid = blk < qi
    gate = jnp.concatenate([_dot(km_ref[:, p * LANES:(p + 1) * LANES].astype(_BF16), qcats[p])
                            for p in range(npair)], axis=1)
    gate = jnp.where(valid, gate, -jnp.inf)
    sel = jnp.zeros((nblk, width), _F32)
    for _ in range(MOBA_TOPK):
        top = jnp.max(gate, axis=0, keepdims=True)
        first = jnp.min(jnp.where(gate == top, blkf, float(nblk)), axis=0, keepdims=True)
        pick = blkf == first
        sel = jnp.where(pick, 1.0, sel)
        gate = jnp.where(pick, -jnp.inf, gate)
    selb_ref[...] = jnp.where(valid & (sel > 0.0), 0.0, NEG_BIG)

    acc_ref[...] = jnp.zeros_like(acc_ref)

    def step(carry, blocks):
        m, l = carry
        ks = [k_ref[0, pl.ds(pl.multiple_of(j * tq, tq), tq), :] for j, _ in blocks]
        ss = [[_dot(kj[:, p * LANES:(p + 1) * LANES], qcats[p]) + bias(p) for kj, (_, bias) in zip(ks, blocks)]
              for p in range(npair)]
        ms, ls = [], []
        for p in range(npair):
            pcols = slice(p * pw, (p + 1) * pw)
            m_new = m[:, pcols]
            for s in ss[p]:
                m_new = jnp.maximum(m_new, jnp.max(s, axis=0, keepdims=True))
            alpha = jnp.exp(m[:, pcols] - m_new)
            es = [jnp.exp(s - m_new) for s in ss[p]]
            l_new = alpha * l[:, pcols]
            for e in es:
                l_new = l_new + jnp.sum(e, axis=0, keepdims=True)
            ls.append(l_new)
            ms.append(m_new)
            e = jnp.concatenate([e.astype(_BF16) for e in es], axis=0)
            for hh in range(2):
                rows = slice((2 * p + hh) * HEAD_DIM, (2 * p + hh + 1) * HEAD_DIM)
                cols = slice(hh * tq, (hh + 1) * tq)
                v_t = jnp.concatenate([vt_ref[0, j, rows, :] for j, _ in blocks], axis=1)
                acc_ref[rows, :] = alpha[:, cols] * acc_ref[rows, :] + _dot(v_t, e[:, cols])
        return jnp.concatenate(ms, axis=1), jnp.concatenate(ls, axis=1)

    def row_bias(j):
        row = selb_ref[pl.ds(j, 1), :]
        return lambda p: row[:, p * pw:(p + 1) * pw]

    def pair_body(t, carry):
        return step(carry, [(2 * t, row_bias(2 * t)), (2 * t + 1, row_bias(2 * t + 1))])

    carry = (jnp.full((1, width), NEG_BIG, _F32), jnp.zeros((1, width), _F32))
    carry = lax.fori_loop(0, qi // 2, pair_body, carry)
    causal = jnp.where(_row_iota((tq, pw)) <= _lane_iota((tq, pw)) % tq, 0.0, NEG_BIG)
    own = (qi, lambda p: causal)
    _, l = lax.cond(qi % 2 == 1,
                    lambda c: step(c, [own, (qi - 1, row_bias(qi - 1))]),
                    lambda c: step(c, [own]), carry)
    inv = 1.0 / l
    out_t = jnp.concatenate([acc_ref[h * HEAD_DIM:(h + 1) * HEAD_DIM, :] * inv[:, h * tq:(h + 1) * tq]
                             for h in range(2 * npair)], axis=0)
    o_ref[0] = out_t.T.astype(o_ref.dtype)


def _moba_call(q_t, k, v_t):
    bsz, seq, width = k.shape
    tq = MOBA_BLOCK
    nblk = seq // tq
    gw = MOBA_PAIRS * LANES
    return pl.pallas_call(
        _moba_body,
        out_shape=jax.ShapeDtypeStruct((bsz, seq, width), _BF16),
        grid=(bsz, width // gw, nblk),
        in_specs=[
            pl.BlockSpec((1, gw, tq), lambda b, h, i: (b, h, i)),
            pl.BlockSpec((1, seq, gw), lambda b, h, i: (b, 0, h)),
            pl.BlockSpec((1, nblk, gw, tq), lambda b, h, i: (b, 0, h, 0)),
        ],
        out_specs=pl.BlockSpec((1, tq, gw), lambda b, h, i: (b, i, h)),
        scratch_shapes=[
            pltpu.VMEM((nblk, gw), _F32),
            pltpu.VMEM((nblk, MOBA_PAIRS * 2 * tq), _F32),
            pltpu.VMEM((gw, tq), _F32),
        ],
        compiler_params=_cparams(("arbitrary", "arbitrary", "arbitrary")),
        name="moba",
    )(q_t, k, v_t)


def _swa_body(sink_ref, q_ref, kp_ref, kc_ref, vp_ref, vc_ref, o_ref):
    i = pl.program_id(1)
    w = SWA_WINDOW
    lane = _lane_iota((1, LANES))
    row = _row_iota((w, 2 * w))
    col = _lane_iota((w, 2 * w))
    in_window = (col > row) & (col <= row + w)
    pairs_per_kv = (B_HEADS // B_KV_HEADS) // 2
    nwin = q_ref.shape[1] // w

    def band(prev_ref, cur_ref, n, g):
        glanes = slice(g * LANES, (g + 1) * LANES)
        before = prev_ref[0, :, glanes] if n == 0 else cur_ref[0, (n - 1) * w:n * w, glanes]
        return jnp.concatenate([before, cur_ref[0, n * w:(n + 1) * w, glanes]], axis=0)

    ss, vs = [], []
    for n in range(nwin):
        ks = [band(kp_ref, kc_ref, n, g) for g in range(B_KV_HEADS)]
        vs.append([band(vp_ref, vc_ref, n, g) for g in range(B_KV_HEADS)])
        for h in range(B_HEADS):
            qp = q_ref[0, n * w:(n + 1) * w, (h // 2) * LANES:(h // 2 + 1) * LANES]
            qh = jnp.where((lane // HEAD_DIM) == (h % 2), qp, jnp.zeros_like(qp))
            ss.append(_dot_t(qh, ks[(h // 2) // pairs_per_kv]))
    for n in range(nwin):
        mask = in_window & ((col >= w) | (i > 0)) if n == 0 else in_window
        outs = []
        for h in range(B_HEADS):
            sink = sink_ref[h]
            s = jnp.where(mask, ss[n * B_HEADS + h], -jnp.inf)
            m = jnp.maximum(jnp.max(s, axis=1, keepdims=True), sink)
            e = jnp.exp(s - m)
            l = jnp.sum(e, axis=1, keepdims=True) + jnp.exp(sink - m)
            outs.append(_dot(e.astype(_BF16), vs[n][(h // 2) // pairs_per_kv]) / l)
        for p in range(B_HEADS // 2):
            o_ref[0, n * w:(n + 1) * w, p * LANES:(p + 1) * LANES] = jnp.where(
                lane < HEAD_DIM, outs[2 * p], outs[2 * p + 1]).astype(o_ref.dtype)


def _swa_call(q, kdup, vdup, sinks):
    bsz, seq, width = q.shape
    w = SWA_WINDOW
    kvw = kdup.shape[2]
    nwin = SWA_WINDOWS_PER_STEP
    tq = nwin * w
    prev = pl.BlockSpec((1, w, kvw), lambda b, i: (b, jnp.maximum(i * nwin - 1, 0), 0))
    cur = pl.BlockSpec((1, tq, kvw), lambda b, i: (b, i, 0))
    return pl.pallas_call(
        _swa_body,
        out_shape=jax.ShapeDtypeStruct((bsz, seq, width), _BF16),
        grid=(bsz, seq // tq),
        in_specs=[
            pl.BlockSpec(memory_space=pltpu.SMEM),
            pl.BlockSpec((1, tq, width), lambda b, i: (b, i, 0)),
            prev, cur, prev, cur,
        ],
        out_specs=pl.BlockSpec((1, tq, width), lambda b, i: (b, i, 0)),
        compiler_params=_cparams(("arbitrary", "arbitrary")),
        name="swa_sink",
    )(sinks, q, kdup, kdup, vdup, vdup)


def _outproj_body(o1_ref, o2_ref, g_ref, x_ref, mod_ref, w_ref, lng_ref, lnb_ref, y_ref):
    half = o1_ref.shape[2]
    a1 = o1_ref[0] * g_ref[0, :, :half]
    a2 = o2_ref[0] * g_ref[0, :, half:]
    y = _dot(a1, w_ref[:half, :]) + _dot(a2, w_ref[half:, :])
    gate = mod_ref[0, 2:3, :]
    z = DEEPNORM_ALPHA * x_ref[0] + (1.0 + gate) * y
    mu = jnp.mean(z, axis=1, keepdims=True)
    zc = z - mu
    var = jnp.mean(zc * zc, axis=1, keepdims=True)
    y_ref[0] = zc * lax.rsqrt(var + LN_EPS) * lng_ref[...] + lnb_ref[...]


def _outproj_call(o1, o2, g, x, mod, w_out, ln_g, ln_b):
    bsz, seq, d = x.shape
    tm = PROJ_TM
    half = o1.shape[2]
    row = lambda n: pl.BlockSpec((1, tm, n), lambda b, i: (b, i, 0))
    return pl.pallas_call(
        _outproj_body,
        out_shape=jax.ShapeDtypeStruct((bsz, seq, d), _F32),
        grid=(bsz, seq // tm),
        in_specs=[
            row(half), row(half), row(2 * half), row(d),
            pl.BlockSpec((1, 3, d), lambda b, i: (b, 0, 0)),
            pl.BlockSpec((2 * half, d), lambda b, i: (0, 0)),
            pl.BlockSpec((1, d), lambda b, i: (0, 0)),
            pl.BlockSpec((1, d), lambda b, i: (0, 0)),
        ],
        out_specs=row(d),
        compiler_params=_cparams(("arbitrary", "arbitrary")),
        name="outproj_deepnorm",
    )(o1, o2, g, x, mod, w_out.astype(_BF16), ln_g.reshape(1, d), ln_b.reshape(1, d))


ODD_COLS = 4096


def _prep_w_odd(w):
    offs = [0]
    for n in ODD_SIZES:
        offs.append(offs[-1] + n)
    w = w.astype(_BF16)
    cqn, cqr, ckv, ckr, iq, ik, iw, dq, dk, dv, gate = [w[:, offs[i]:offs[i + 1]] for i in range(len(ODD_SIZES))]
    iw_blk = jnp.concatenate([iw, jnp.zeros((w.shape[0], LANES - IDX_HEADS), w.dtype)], axis=1)
    return jnp.concatenate([cqn, cqr, ckv, jnp.tile(ckr, (1, 4)), iq, jnp.tile(ik, (1, 4)), iw_blk,
                            dq, dk, dv, gate], axis=1)


def _inproj_odd_body(x_ref, mod_ref, w_ref, cosr_ref, sinr_ref, cosi_ref, sini_ref, kvg_ref,
                     cqn_ref, cqr_ref, kvc_ref, ckvt_ref, iq_ref, ik_ref, iw_ref,
                     dqt_ref, dk_ref, dvt_ref, g_ref):
    h = _modulated(x_ref, mod_ref)
    cosr, sinr = cosr_ref[...], sinr_ref[...]
    cosi, sini = cosi_ref[...], sini_ref[...]

    acc = _dot(h, w_ref[:, 0:512])
    for p in range(4):
        cqn_ref[0, p * LANES:(p + 1) * LANES, :] = acc[:, p * LANES:(p + 1) * LANES].T.astype(_BF16)
    acc = _dot(h, w_ref[:, 512:1024])
    for p in range(2):
        piece = _rope_piece(acc[:, p * LANES:(p + 1) * LANES], cosr, sinr, C_ROPE_DIM)
        cqr_ref[0, p * LANES:(p + 1) * LANES, :] = piece.T.astype(_BF16)
    ckv = acc[:, 256:384]
    ckv = ckv * lax.rsqrt(jnp.mean(ckv * ckv, axis=1, keepdims=True) + LN_EPS) * kvg_ref[...]
    kvc_ref[0, :, 0:LANES] = ckv.astype(_BF16)
    ckvt_ref[0, 0] = ckv.T.astype(_BF16)
    kvc_ref[0, :, LANES:2 * LANES] = _rope_piece(acc[:, 384:512], cosr, sinr, C_ROPE_DIM).astype(_BF16)
    acc = _dot(h, w_ref[:, 1024:1536])
    for p in range(2):
        piece = _rope_piece(acc[:, p * LANES:(p + 1) * LANES], cosi, sini, IDX_DIM)
        iq_ref[0, p * LANES:(p + 1) * LANES, :] = piece.T.astype(_BF16)
    ik_ref[0] = _rope_piece(acc[:, 256:384], cosi, sini, IDX_DIM).astype(_BF16)
    iw_ref[0] = acc[:, 384:512].T[0:IDX_HEADS, :]
    acc = _dot(h, w_ref[:, 1536:2048]) * (HEAD_DIM ** -0.5)
    for p in range(4):
        dqt_ref[0, p * LANES:(p + 1) * LANES, :] = acc[:, p * LANES:(p + 1) * LANES].T.astype(_BF16)
    dk_ref[0] = _dot(h, w_ref[:, 2048:2560]).astype(_BF16)
    _store_blocks_t(_dot(h, w_ref[:, 2560:3072]), dvt_ref, SB_T)
    _silu_gate(h, w_ref, 3072, g_ref)


def _inproj_odd_call(x, mod, w, rope_r, rope_i, kv_g):
    bsz, seq, d = x.shape
    tm = DSA_CK
    row = lambda n: pl.BlockSpec((1, tm, n), lambda b, i: (b, i, 0))
    col = lambda n: pl.BlockSpec((1, n, tm), lambda b, i: (b, 0, i))
    tab = pl.BlockSpec((tm, LANES), lambda b, i: (i, 0))
    tok = lambda n, dt: jax.ShapeDtypeStruct((bsz, seq, n), dt)
    feat = lambda n, dt: jax.ShapeDtypeStruct((bsz, n, seq), dt)
    return pl.pallas_call(
        _inproj_odd_body,
        out_shape=[
            feat(512, _BF16), feat(256, _BF16), tok(256, _BF16),
            jax.ShapeDtypeStruct((bsz, seq // tm, C_KV_LATENT, tm), _BF16),
            feat(256, _BF16), tok(128, _BF16), feat(IDX_HEADS, _F32),
            feat(512, _BF16), tok(512, _BF16),
            jax.ShapeDtypeStruct((bsz, seq // SB_T, 512, SB_T), _BF16), tok(1024, _BF16),
        ],
        grid=(bsz, seq // tm),
        in_specs=[
            row(d),
            pl.BlockSpec((1, 3, d), lambda b, i: (b, 0, 0)),
            pl.BlockSpec((d, ODD_COLS), lambda b, i: (0, 0)),
            tab, tab, tab, tab,
            pl.BlockSpec((1, LANES), lambda b, i: (0, 0)),
        ],
        out_specs=[
            col(512), col(256), row(256),
            pl.BlockSpec((1, 1, C_KV_LATENT, tm), lambda b, i: (b, i, 0, 0)),
            col(256), row(128), col(IDX_HEADS),
            col(512), row(512),
            pl.BlockSpec((1, tm // SB_T, 512, SB_T), lambda b, i: (b, i, 0, 0)), row(1024),
        ],
        compiler_params=_cparams(("arbitrary", "arbitrary")),
        name="inproj_odd",
    )(x, mod, w, rope_r[0], rope_r[1], rope_i[0], rope_i[1], kv_g.reshape(1, LANES))


def _dsa_body(cqn_ref, cqr_ref, iq_ref, iw_ref, kvc_ref, ckvt_ref, ik_ref, wuk_ref, wuvt_ref, o_ref,
              qct_ref, iqt_ref, sc_ref, acc_ref):
    i = pl.program_id(1)
    tq, ck = DSA_TQ, DSA_CK
    seq = kvc_ref.shape[1]
    n_sel = float(min(DSA_TOPK, seq // 4))
    nch = i // (ck // tq) + 1
    att_scale = (C_NOPE_DIM + C_ROPE_DIM) ** -0.5
    idx_scale = (IDX_DIM * IDX_HEADS) ** -0.5
    hq = C_HEADS * tq

    row128 = _row_iota((LANES, tq))
    for h in range(C_HEADS):
        cols = slice(h * tq, (h + 1) * tq)
        qn_t = cqn_ref[0, h * C_NOPE_DIM:(h + 1) * C_NOPE_DIM, :]
        qlat_t = _dot(wuk_ref[h], qn_t) * att_scale
        qct_ref[0:C_KV_LATENT, cols] = qlat_t.astype(_BF16)
        qr_t = cqr_ref[0, h * C_ROPE_DIM:(h + 1) * C_ROPE_DIM, :].astype(_F32) * att_scale
        qct_ref[C_KV_LATENT:C_KV_LATENT + C_ROPE_DIM, cols] = qr_t.astype(_BF16)
        qct_ref[C_KV_LATENT + C_ROPE_DIM:, cols] = jnp.zeros((LANES - C_ROPE_DIM, tq), _BF16)
        blk4 = h // 4
        iq_t = iq_ref[0, blk4 * LANES:(blk4 + 1) * LANES, :]
        iqt_ref[:, cols] = jnp.where((row128 // IDX_DIM) == (h % 4), iq_t, jnp.zeros_like(iq_t))

    t_pos = i * tq + _lane_iota((ck, tq))
    key_row = _row_iota((ck, tq))

    def score_body(c, carry):
        vmax, vmin = carry
        off = pl.multiple_of(c * ck, ck)
        rel = _dot(ik_ref[0, pl.ds(off, ck), :], iqt_ref[...])
        score = jnp.zeros((ck, tq), _F32)
        for h in range(IDX_HEADS):
            score = score + jnp.maximum(rel[:, h * tq:(h + 1) * tq], 0.0) * iw_ref[0, h:h + 1, :]
        score = score * idx_scale
        low = jnp.where(off + key_row <= t_pos, score, -jnp.inf)
        sc_ref[c] = low
        vmax = jnp.maximum(vmax, jnp.max(low, axis=0, keepdims=True))
        vmin = jnp.minimum(vmin, jnp.min(score, axis=0, keepdims=True))
        return vmax, vmin

    vmax, vmin = lax.fori_loop(0, nch, score_body,
                               (jnp.full((1, tq), -jnp.inf, _F32), jnp.full((1, tq), jnp.inf, _F32)))

    def count(pred):
        rows_acc = 8 * (8 * LANES // tq)

        slab_row = _row_iota((rows_acc, tq))

        def body(c, acc):
            for r in range(ck // rows_acc):
                rows = slice(r * rows_acc, (r + 1) * rows_acc)
                acc = acc + jnp.where(pred(sc_ref[c, rows, :], c * ck + r * rows_acc + slab_row), 1.0, 0.0)
            return acc

        acc = lax.fori_loop(0, nch, body, jnp.zeros((rows_acc, tq), _F32))
        return jnp.sum(acc, axis=0, keepdims=True)

    def ordered_float(u):
        key = u ^ INT_MIN
        return pltpu.bitcast(key ^ ((key >> 31) & 0x7FFFFFFF), _F32)

    n_valid = (i * tq + _lane_iota((1, tq)) + 1).astype(_F32)
    few = n_valid < n_sel
    c_ge0 = count(lambda sc, pos: sc >= 0.0)
    c_gt0 = count(lambda sc, pos: sc > 0.0)
    zero_tie = (c_gt0 < n_sel) & (c_ge0 >= n_sel)
    above = c_gt0 >= n_sel

    def bisect_body(_, carry):
        lo, hi, cnt_lo = carry
        mid = lo + 0.5 * (hi - lo)
        cnt = count(lambda sc, pos: sc >= mid)
        ok = cnt >= n_sel
        return jnp.where(ok, mid, lo), jnp.where(ok, hi, mid), jnp.where(ok, cnt, cnt_lo)

    def all_resolved(cnt_lo):
        done = few | zero_tie | (cnt_lo == n_sel)
        return jnp.min(jnp.where(done, 1.0, 0.0)) > 0.0

    state = (jnp.where(above, 0.0, vmin), jnp.where(above, vmax, 0.0), jnp.where(above, c_ge0, n_valid))
    state = lax.fori_loop(0, BISECT_STEPS, bisect_body, state)
    state = lax.cond(all_resolved(state[2]), lambda s: s,
                     lambda s: lax.fori_loop(0, BISECT_EXTRA, bisect_body, s), state)
    lo, _, cnt_lo = state

    def exact_search():
        def bit_body(b, carry):
            t, cnt_t = carry
            cand = t | jnp.left_shift(jnp.int32(1), 31 - b)
            cand_f = ordered_float(cand)
            cnt = count(lambda sc, pos: sc >= cand_f)
            ok = cnt >= n_sel
            return jnp.where(ok, cand, t), jnp.where(ok, cnt, cnt_t)

        t0 = jnp.zeros((1, tq), jnp.int32)
        t, cnt_t = lax.fori_loop(0, 32, bit_body, (t0, jnp.full((1, tq), float(seq), _F32)))
        thr = jnp.where(t == 0, -jnp.inf, ordered_float(t))
        return thr, cnt_t, n_sel - count(lambda sc, pos: sc > thr)

    thr, cnt_thr, room = lax.cond(
        all_resolved(cnt_lo),
        lambda: (jnp.where(few, -jnp.inf, jnp.where(zero_tie, 0.0, lo)), jnp.where(zero_tie, c_ge0, cnt_lo),
                 jnp.where(zero_tie, n_sel - c_gt0, n_sel)),
        exact_search)

    excess = jnp.max(jnp.where((cnt_thr > n_sel) & (thr > -jnp.inf), 1.0, 0.0)) > 0.0

    def plain_bias():
        def body(c, _):
            chosen = (sc_ref[c] >= thr) & (c * ck + key_row <= t_pos)
            sc_ref[c] = jnp.where(chosen, 0.0, NEG_BIG)
            return 0

        lax.fori_loop(0, nch, body, 0)

    def tie_bias():
        upto =jnp.where(_lane_iota((ck, ck)) <= _row_iota((ck, ck)), 1.0, 0.0).astype(_BF16)

        def body(c, seen):
            sc = sc_ref[c]
            causal = c * ck + key_row <= t_pos
            tied = (sc == thr) & causal
            one = jnp.where(tied, 1.0, 0.0)
            rank = _dot(upto, one.astype(_BF16)) + seen
            chosen = ((sc > thr) & causal) | (tied & (rank <= room))
            sc_ref[c] = jnp.where(chosen, 0.0, NEG_BIG)
            return seen + jnp.sum(one, axis=0, keepdims=True)

        lax.fori_loop(0, nch, body, jnp.zeros((1, tq), _F32))

    lax.cond(excess, tie_bias, plain_bias)

    acc_ref[...] = jnp.zeros_like(acc_ref)
    grp = 2 * tq
    gcols = [slice(g * grp, (g + 1) * grp) for g in range(hq // grp)]

    def att_body(c, carry):
        off = pl.multiple_of(c * ck, ck)
        kv = kvc_ref[0, pl.ds(off, ck), :]
        bias = sc_ref[c]
        bias2 = jnp.concatenate([bias, bias], axis=1)
        ss = [_dot(kv, qct_ref[:, cols]) + bias2 for cols in gcols]
        ms, ls, es, alphas = [], [], [], []
        for g, cols in enumerate(gcols):
            m, l = carry[0][:, cols], carry[1][:, cols]
            m_new = jnp.maximum(m, jnp.max(ss[g], axis=0, keepdims=True))
            alpha = jnp.exp(m - m_new)
            e = jnp.exp(ss[g] - m_new)
            ls.append(alpha * l + jnp.sum(e, axis=0, keepdims=True))
            ms.append(m_new)
            alphas.append(alpha)
            es.append(e.astype(_BF16))
        kv_t = ckvt_ref[0, c]
        for g, cols in enumerate(gcols):
            acc_ref[:, cols] = alphas[g] * acc_ref[:, cols] + _dot(kv_t, es[g])
        return jnp.concatenate(ms, axis=1), jnp.concatenate(ls, axis=1)

    m0 = jnp.full((1, hq), NEG_BIG, _F32)
    l0 = jnp.zeros((1, hq), _F32)
    _, l = lax.fori_loop(0, nch, att_body, (m0, l0))
    o_lat_t = (acc_ref[...] / l).astype(_BF16)
    outs = [_dot(wuvt_ref[h], o_lat_t[:, h * tq:(h + 1) * tq]) for h in range(C_HEADS)]
    o_ref[0] = jnp.concatenate(outs, axis=0).T.astype(o_ref.dtype)


def _dsa_call(cqn_t, cqr_t, iq_t, iw_t, kvc, ckv_t, ik, w_uk, wuv_t):
    bsz, seq, _ = kvc.shape
    tq = DSA_TQ
    qcol = lambda n: pl.BlockSpec((1, n, tq), lambda b, i: (b, 0, i))
    full = lambda n: pl.BlockSpec((1, seq, n), lambda b, i: (b, 0, 0))
    whole = lambda a: pl.BlockSpec(a.shape, lambda b, i: (0,) * a.ndim)
    width = C_HEADS * C_V_DIM
    return pl.pallas_call(
        _dsa_body,
        out_shape=jax.ShapeDtypeStruct((bsz, seq, width), _BF16),
        grid=(bsz, seq // tq),
        in_specs=[
            qcol(512), qcol(256), qcol(256), qcol(IDX_HEADS), full(256),
            pl.BlockSpec((1,) + ckv_t.shape[1:], lambda b, i: (b, 0, 0, 0)),
            full(128), whole(w_uk), whole(wuv_t),
        ],
        out_specs=pl.BlockSpec((1, tq, width), lambda b, i: (b, i, 0)),
        scratch_shapes=[
            pltpu.VMEM((2 * LANES, C_HEADS * tq), _BF16),
            pltpu.VMEM((LANES, IDX_HEADS * tq), _BF16),
            pltpu.VMEM((seq // DSA_CK, DSA_CK, tq), _F32),
            pltpu.VMEM((C_KV_LATENT, C_HEADS * tq), _F32),
        ],
        compiler_params=_cparams(("arbitrary", "arbitrary")),
        name="dsa",
    )(cqn_t, cqr_t, iq_t, iw_t, kvc, ckv_t, ik, w_uk, wuv_t)


def _sb_body(qt_ref, k_ref, vt_ref, o_ref, acc_ref):
    i = pl.program_id(1)
    t = SB_T
    npair = D_HEADS // 2
    width = 2 * t
    qcats = [_head_pair_rhs(qt_ref[0, p * LANES:(p + 1) * LANES, :], HEAD_DIM) for p in range(npair)]
    key_row = _row_iota((t, width))
    q_lane = _lane_iota((t, width)) % t
    later = jnp.where(_lane_iota((t, t)) > _row_iota((t, t)), 1.0, 0.0).astype(_BF16)
    later2 = jnp.concatenate([later, later], axis=1)
    acc_ref[...] = jnp.zeros_like(acc_ref)

    def blocks(js, rest, masks):
        ks = [k_ref[0, pl.ds(pl.multiple_of(j * t, t), t), :] for j in js]
        zs = [[_dot(kj[:, p * LANES:(p + 1) * LANES], qcats[p]) for p in range(npair)] for kj in ks]
        rests = []
        for p in range(npair):
            run = rest[:, p * width:(p + 1) * width]
            for n, j in enumerate(js):
                z = zs[n][p]
                log_keep = -(jnp.maximum(z, 0.0) + jnp.log(1.0 + jnp.exp(-jnp.abs(z))))
                log_beta = z + log_keep
                if masks[n] is not None:
                    log_keep = jnp.where(masks[n], log_keep, 0.0)
                hi = log_keep.astype(_BF16)
                lo = (log_keep - hi.astype(_F32)).astype(_BF16)
                a = jnp.exp(log_beta + (_dot(later2, jnp.concatenate([hi, lo], axis=0)) + run))
                if masks[n] is not None:
                    a = jnp.where(masks[n], a, 0.0)
                prows = slice(p * LANES, (p + 1) * LANES)
                pv = _dot(vt_ref[0, j, prows, :], a.astype(_BF16))
                for hh in range(2):
                    rows = slice((2 * p + hh) * HEAD_DIM, (2 * p + hh + 1) * HEAD_DIM)
                    acc_ref[rows, :] = acc_ref[rows, :] + pv[hh * HEAD_DIM:(hh + 1) * HEAD_DIM, hh * t:(hh + 1) * t]
                run = run + jnp.sum(log_keep, axis=0, keepdims=True)
            rests.append(run)
        return jnp.concatenate(rests, axis=1)

    rest = blocks([i], jnp.zeros((1, npair * width), _F32), [key_row < q_lane])

    def cond(carry):
        j, _, top = carry
        return (j >= 0) & (top > SB_UNDERFLOW)

    def body(carry):
        j, rest, _ = carry
        second_ok = jnp.broadcast_to(j >= 1, (t, width))
        rest = blocks([j, jnp.maximum(j - 1, 0)], rest, [None, second_ok])
        return j - 2, rest, jnp.max(rest)

    lax.while_loop(cond, body, (i - 1, rest, jnp.max(rest)))
    o_ref[0] = acc_ref[...].T.astype(o_ref.dtype)


def _sb_call(q_t, k, v_t):
    bsz, seq, width = k.shape
    t = SB_T
    return pl.pallas_call(
        _sb_body,
        out_shape=jax.ShapeDtypeStruct((bsz, seq, width), _BF16),
        grid=(bsz, seq // t),
        in_specs=[
            pl.BlockSpec((1, width, t), lambda b, i: (b, 0, i)),
            pl.BlockSpec((1, seq, width), lambda b, i: (b, 0, 0)),
            pl.BlockSpec((1, seq // t, width, t), lambda b, i: (b, 0, 0, 0)),
        ],
        out_specs=pl.BlockSpec((1, t, width), lambda b, i: (b, i, 0)),
        scratch_shapes=[pltpu.VMEM((width, t), _F32)],
        compiler_params=_cparams(("arbitrary", "arbitrary")),
        name="stick_breaking",
    )(q_t, k, v_t)


def _even_layer(x, mod, w_in, sinks, w_out, ln_g, ln_b, rope_h):
    aq_t, ak, bq, bk, av_t, bv, g = _inproj_even_call(x, mod, _prep_w_even(w_in), *rope_h)
    oa = _moba_call(aq_t, ak, av_t)
    ob = _swa_call(bq, bk, bv, sinks)
    return _outproj_call(oa, ob, g, x, mod, w_out, ln_g, ln_b)


def _odd_layer(x, mod, w_in, kv_g, w_uk, w_uv, w_out, ln_g, ln_b, rope_r, rope_i):
    cqn_t, cqr_t, kvc, ckv_t, iq_t, ik, iw_t, dq_t, dk, dv_t, g = _inproj_odd_call(
        x, mod, _prep_w_odd(w_in), rope_r, rope_i, kv_g)
    oc = _dsa_call(cqn_t, cqr_t, iq_t, iw_t, kvc, ckv_t, ik,
                   w_uk.astype(_BF16), w_uv.transpose(0, 2, 1).astype(_BF16))
    od = _sb_call(dq_t, dk, dv_t)
    return _outproj_call(oc, od, g, x, mod, w_out, ln_g, ln_b)


def kernel(x, c, w_ada, b_ada, w_in_even, sink_logits, w_in_odd, kv_norm_g, w_uk, w_uv, w_out, ln_g, ln_b):
    bsz, seq, d = x.shape
    rope_h = _rope_tables(seq, HEAD_DIM, LANES)
    rope_r = _rope_tables(seq, C_ROPE_DIM, LANES)
    rope_i = _rope_tables(seq, IDX_DIM, LANES)
    mods = _ada_call(c, w_ada, b_ada).reshape(DEPTH, bsz, 3, d)
    for layer in range(DEPTH):
        mod = mods[layer]
        j = layer // 2
        if layer % 2 == 0:
            x = _even_layer(x, mod, w_in_even[j], sink_logits[j], w_out[layer], ln_g[layer], ln_b[layer], rope_h)
        else:
            x = _odd_layer(x, mod, w_in_odd[j], kv_norm_g[j], w_uk[j], w_uv[j], w_out[layer],
                           ln_g[layer], ln_b[layer], rope_r, rope_i)
    return x
```

```python
import jax
import jax.numpy as jnp
from jax import lax
from jax.experimental import pallas as pl
from jax.experimental.pallas import tpu as pltpu

D_MODEL = 1024
DEPTH = 2
HEAD_DIM = 64
ROPE_THETA = 10000.0
LN_EPS = 1e-5
A_HEADS = 8
MOBA_BLOCK = 256
MOBA_TOPK = 3
B_HEADS = 8
B_KV_HEADS = 2
SWA_WINDOW = 128
C_HEADS = 8
C_NOPE_DIM = 64
C_ROPE_DIM = 32
C_V_DIM = 64
C_KV_LATENT = 128
IDX_HEADS = 8
IDX_DIM = 32
DSA_TOPK = 256
D_HEADS = 8
MIX_WIDTH = 1024
EVEN_SIZES = (512, 512, 512, 512, 128, 128, 1024)
ODD_SIZES = (512, 256, 128, 32, 256, 32, 8, 512, 512, 512, 1024)
DEEPNORM_ALPHA = (2 * DEPTH) ** 0.25

LANES = 128
NEG_BIG = -1e30
INT_MIN = -2 ** 31
SB_UNDERFLOW = -104.0

PROJ_TM = 512
DSA_TQ = 512
DSA_CK = 512
SB_T = 128
MOBA_PAIRS = 4
SWA_WINDOWS_PER_STEP = 2
BISECT_STEPS = 20
BISECT_EXTRA = 4
VMEM_LIMIT = 48 * 1024 * 1024

_BF16 = jnp.bfloat16
_F32 = jnp.float32


def _cparams(sem):
    return pltpu.CompilerParams(dimension_semantics=sem, vmem_limit_bytes=VMEM_LIMIT)


def _dot_t(a, b):
    return lax.dot_general(a, b, (((1,), (1,)), ((), ())), preferred_element_type=_F32)


def _dot(a, b):
    return jnp.dot(a, b, preferred_element_type=_F32)


def _lane_iota(shape):
    return lax.broadcasted_iota(jnp.int32, shape, len(shape) - 1)


def _row_iota(shape):
    return lax.broadcasted_iota(jnp.int32, shape, len(shape) - 2)


def _head_pair_rhs(q_t, head_dim):
    frow = _row_iota(q_t.shape)
    return jnp.concatenate(
        [jnp.where((frow // head_dim) == hh, q_t, jnp.zeros_like(q_t)) for hh in range(2)], axis=1)


def _ada_body(c_ref, w_ref, b_ref, o_ref):
    c = c_ref[...]
    cond = c * jax.nn.sigmoid(c)
    o_ref[0] = _dot(cond, w_ref[0]) + b_ref[0]


def _ada_call(c, w_ada, b_ada):
    depth, d, n3 = w_ada.shape
    bsz = c.shape[0]
    nb = n3 // d
    return pl.pallas_call(
        _ada_body,
        out_shape=jax.ShapeDtypeStruct((depth, bsz, n3), _F32),
        grid=(depth, nb),
        in_specs=[
            pl.BlockSpec((bsz, d), lambda l, j: (0, 0)),
            pl.BlockSpec((1, d, d), lambda l, j: (l, 0, j)),
            pl.BlockSpec((1, 1, d), lambda l, j: (l, 0, j)),
        ],
        out_specs=pl.BlockSpec((1, bsz, d), lambda l, j: (l, 0, j)),
        compiler_params=_cparams(("arbitrary", "arbitrary")),
        name="ada_mod",
    )(c, w_ada, b_ada.reshape(depth, 1, n3))


def _rope_tables(seq, dim, period_lanes):
    inv = 1.0 / (ROPE_THETA ** (jnp.arange(0, dim, 2, dtype=_F32) / dim))
    ang = jnp.arange(seq, dtype=_F32)[:, None] * inv[None, :]
    cos, sin = jnp.cos(ang), jnp.sin(ang)
    cos_h = jnp.concatenate([cos, cos], axis=1)
    sin_h = jnp.concatenate([-sin, sin], axis=1)
    reps = period_lanes // dim
    return jnp.tile(cos_h, (1, reps)), jnp.tile(sin_h, (1, reps))


def _rope_piece(x, cos, sin_signed, dim):
    half = dim // 2
    first = (_lane_iota(x.shape) % dim) < half
    partner = jnp.where(first, pltpu.roll(x, LANES - half, 1), pltpu.roll(x, half, 1))
    return x * cos + partner * sin_signed


def _modulated(x_ref, mod_ref):
    x = x_ref[0]
    shift = mod_ref[0, 0:1, :]
    scale = mod_ref[0, 1:2, :]
    return (x * (1.0 + scale) + shift).astype(_BF16)


def _silu_gate(h, w_ref, col0, g_ref):
    for p in range(2):
        gate = _dot(h, w_ref[:, col0 + 512 * p:col0 + 512 * (p + 1)])
        g_ref[0, :, 512 * p:512 * (p + 1)] = (gate * jax.nn.sigmoid(gate)).astype(_BF16)


def _store_blocks_t(acc, out_ref, blk):
    tm, n = acc.shape
    for r in range(tm // blk):
        for p in range(n // LANES):
            out_ref[0, r, p * LANES:(p + 1) * LANES, :] = (
                acc[r * blk:(r + 1) * blk, p * LANES:(p + 1) * LANES].T.astype(_BF16))


EVEN_COLS = sum(EVEN_SIZES)


def _inproj_even_body(x_ref, mod_ref, w_ref, cos_ref, sin_ref,
                      aqt_ref, ak_ref, bq_ref, bk_ref, avt_ref, bv_ref, g_ref):
    h = _modulated(x_ref, mod_ref)
    cos = cos_ref[...]
    sin = sin_ref[...]
    q_scale = HEAD_DIM ** -0.5

    def roped(col0, ncols, out_ref, scale, transposed=False):
        acc = _dot(h, w_ref[:, col0:col0 + ncols])
        for p in range(ncols // LANES):
            piece = _rope_piece(acc[:, p * LANES:(p + 1) * LANES], cos, sin, HEAD_DIM)
            if scale != 1.0:
                piece = piece * scale
            if transposed:
                out_ref[0, p * LANES:(p + 1) * LANES, :] = piece.T.astype(_BF16)
            else:
                out_ref[0, :, p * LANES:(p + 1) * LANES] = piece.astype(_BF16)

    def dup_kv_heads(x, out_ref):
        first = _lane_iota(x.shape) < HEAD_DIM
        swapped = pltpu.roll(x, HEAD_DIM, 1)
        out_ref[0, :, 0:LANES] = jnp.where(first, x, swapped).astype(_BF16)
        out_ref[0, :, LANES:2 * LANES] = jnp.where(first, swapped, x).astype(_BF16)

    roped(0, 512, aqt_ref, q_scale, transposed=True)
    roped(512, 512, ak_ref, 1.0)
    _store_blocks_t(_dot(h, w_ref[:, 1024:1536]), avt_ref, MOBA_BLOCK)
    roped(1536, 512, bq_ref, q_scale)
    dup_kv_heads(_rope_piece(_dot(h, w_ref[:, 2048:2176]), cos, sin, HEAD_DIM), bk_ref)
    dup_kv_heads(_dot(h, w_ref[:, 2176:2304]), bv_ref)
    _silu_gate(h, w_ref, 2304, g_ref)


def _inproj_even_call(x, mod, w, cos, sin):
    bsz, seq, d = x.shape
    tm = PROJ_TM
    row = lambda n: pl.BlockSpec((1, tm, n), lambda b, i: (b, i, 0))
    tok = lambda n: jax.ShapeDtypeStruct((bsz, seq, n), _BF16)
    per_tile = tm // MOBA_BLOCK
    return pl.pallas_call(
        _inproj_even_body,
        out_shape=[
            jax.ShapeDtypeStruct((bsz, 512, seq), _BF16), tok(512), tok(512), tok(256),
            jax.ShapeDtypeStruct((bsz, seq // MOBA_BLOCK, 512, MOBA_BLOCK), _BF16), tok(256), tok(1024),
        ],
        grid=(bsz, seq // tm),
        in_specs=[
            row(d),
            pl.BlockSpec((1, 3, d), lambda b, i: (b, 0, 0)),
            pl.BlockSpec((d, EVEN_COLS), lambda b, i: (0, 0)),
            pl.BlockSpec((tm, LANES), lambda b, i: (i, 0)),
            pl.BlockSpec((tm, LANES), lambda b, i: (i, 0)),
        ],
        out_specs=[
            pl.BlockSpec((1, 512, tm), lambda b, i: (b, 0, i)), row(512), row(512), row(256),
            pl.BlockSpec((1, per_tile, 512, MOBA_BLOCK), lambda b, i: (b, i, 0, 0)), row(256), row(1024),
        ],
        compiler_params=_cparams(("arbitrary", "arbitrary")),
        name="inproj_even",
    )(x, mod, w, cos, sin)


def _moba_body(qt_ref, k_ref, vt_ref, o_ref, km_ref, selb_ref, acc_ref):
    qi = pl.program_id(2)
    tq = MOBA_BLOCK
    nblk = km_ref.shape[0]
    npair = qt_ref.shape[1] // LANES
    pw = 2 * tq
    width = npair * pw

    @pl.when(qi == 0)
    def _():
        for r in range(nblk):
            blk_k = k_ref[0, r * tq:(r + 1) * tq, :].astype(_F32)
            km_ref[r:r + 1, :] = jnp.mean(blk_k, axis=0, keepdims=True)

    qcats = [_head_pair_rhs(qt_ref[0, p * LANES:(p + 1) * LANES, :], HEAD_DIM) for p in range(npair)]

    blk = _row_iota((nblk, width))
    blkf = blk.astype(_F32)
    valid = blk < qi
    gate = jnp.concatenate([_dot(km_ref[:, p * LANES:(p + 1) * LANES].astype(_BF16), qcats[p])
                            for p in range(npair)], axis=1)
    gate = jnp.where(valid, gate, -jnp.inf)
    sel = jnp.zeros((nblk, width), _F32)
    for _ in range(MOBA_TOPK):
        top = jnp.max(gate, axis=0, keepdims=True)
        first = jnp.min(jnp.where(gate == top, blkf, float(nblk)), axis=0, keepdims=True)
        pick = blkf == first
        sel = jnp.where(pick, 1.0, sel)
        gate = jnp.where(pick, -jnp.inf, gate)
    selb_ref[...] = jnp.where(valid & (sel > 0.0), 0.0, NEG_BIG)

    acc_ref[...] = jnp.zeros_like(acc_ref)

    def step(carry, blocks):
        m, l = carry
        ks = [k_ref[0, pl.ds(pl.multiple_of(j * tq, tq), tq), :] for j, _ in blocks]
        ss = [[_dot(kj[:, p * LANES:(p + 1) * LANES], qcats[p]) + bias(p) for kj, (_, bias) in zip(ks, blocks)]
              for p in range(npair)]
        ms, ls = [], []
        for p in range(npair):
            pcols = slice(p * pw, (p + 1) * pw)
            m_new = m[:, pcols]
            for s in ss[p]:
                m_new = jnp.maximum(m_new, jnp.max(s, axis=0, keepdims=True))
            alpha = jnp.exp(m[:, pcols] - m_new)
            es = [jnp.exp(s - m_new) for s in ss[p]]
            l_new = alpha * l[:, pcols]
            for e in es:
                l_new = l_new + jnp.sum(e, axis=0, keepdims=True)
            ls.append(l_new)
            ms.append(m_new)
            e = jnp.concatenate([e.astype(_BF16) for e in es], axis=0)
            for hh in range(2):
                rows = slice((2 * p + hh) * HEAD_DIM, (2 * p + hh + 1) * HEAD_DIM)
                cols = slice(hh * tq, (hh + 1) * tq)
                v_t = jnp.concatenate([vt_ref[0, j, rows, :] for j, _ in blocks], axis=1)
                acc_ref[rows, :] = alpha[:, cols] * acc_ref[rows, :] + _dot(v_t, e[:, cols])
        return jnp.concatenate(ms, axis=1), jnp.concatenate(ls, axis=1)

    def row_bias(j):
        row = selb_ref[pl.ds(j, 1), :]
        return lambda p: row[:, p * pw:(p + 1) * pw]

    def pair_body(t, carry):
        return step(carry, [(2 * t, row_bias(2 * t)), (2 * t + 1, row_bias(2 * t + 1))])

    carry = (jnp.full((1, width), NEG_BIG, _F32), jnp.zeros((1, width), _F32))
    carry = lax.fori_loop(0, qi // 2, pair_body, carry)
    causal = jnp.where(_row_iota((tq, pw)) <= _lane_iota((tq, pw)) % tq, 0.0, NEG_BIG)
    own = (qi, lambda p: causal)
    _, l = lax.cond(qi % 2 == 1,
                    lambda c: step(c, [own, (qi - 1, row_bias(qi - 1))]),
                    lambda c: step(c, [own]), carry)
    inv = 1.0 / l
    out_t = jnp.concatenate([acc_ref[h * HEAD_DIM:(h + 1) * HEAD_DIM, :] * inv[:, h * tq:(h + 1) * tq]
                             for h in range(2 * npair)], axis=0)
    o_ref[0] = out_t.T.astype(o_ref.dtype)


def _moba_call(q_t, k, v_t):
    bsz, seq, width = k.shape
    tq = MOBA_BLOCK
    nblk = seq // tq
    gw = MOBA_PAIRS * LANES
    return pl.pallas_call(
        _moba_body,
        out_shape=jax.ShapeDtypeStruct((bsz, seq, width), _BF16),
        grid=(bsz, width // gw, nblk),
        in_specs=[
            pl.BlockSpec((1, gw, tq), lambda b, h, i: (b, h, i)),
            pl.BlockSpec((1, seq, gw), lambda b, h, i: (b, 0, h)),
            pl.BlockSpec((1, nblk, gw, tq), lambda b, h, i: (b, 0, h, 0)),
        ],
        out_specs=pl.BlockSpec((1, tq, gw), lambda b, h, i: (b, i, h)),
        scratch_shapes=[
            pltpu.VMEM((nblk, gw), _F32),
            pltpu.VMEM((nblk, MOBA_PAIRS * 2 * tq), _F32),
            pltpu.VMEM((gw, tq), _F32),
        ],
        compiler_params=_cparams(("arbitrary", "arbitrary", "arbitrary")),
        name="moba",
    )(q_t, k, v_t)


def _swa_body(sink_ref, q_ref, kp_ref, kc_ref, vp_ref, vc_ref, o_ref):
    i = pl.program_id(1)
    w = SWA_WINDOW
    lane = _lane_iota((1, LANES))
    row = _row_iota((w, 2 * w))
    col = _lane_iota((w, 2 * w))
    in_window = (col > row) & (col <= row + w)
    pairs_per_kv = (B_HEADS // B_KV_HEADS) // 2
    nwin = q_ref.shape[1] // w

    def band(prev_ref, cur_ref, n, g):
        glanes = slice(g * LANES, (g + 1) * LANES)
        before = prev_ref[0, :, glanes] if n == 0 else cur_ref[0, (n - 1) * w:n * w, glanes]
        return jnp.concatenate([before, cur_ref[0, n * w:(n + 1) * w, glanes]], axis=0)

    ss, vs = [], []
    for n in range(nwin):
        ks = [band(kp_ref, kc_ref, n, g) for g in range(B_KV_HEADS)]
        vs.append([band(vp_ref, vc_ref, n, g) for g in range(B_KV_HEADS)])
        for h in range(B_HEADS):
            qp = q_ref[0, n * w:(n + 1) * w, (h // 2) * LANES:(h // 2 + 1) * LANES]
            qh = jnp.where((lane // HEAD_DIM) == (h % 2), qp, jnp.zeros_like(qp))
            ss.append(_dot_t(qh, ks[(h // 2) // pairs_per_kv]))
    for n in range(nwin):
        mask = in_window & ((col >= w) | (i > 0)) if n == 0 else in_window
        outs = []
        for h in range(B_HEADS):
            sink = sink_ref[h]
            s = jnp.where(mask, ss[n * B_HEADS + h], -jnp.inf)
            m = jnp.maximum(jnp.max(s, axis=1, keepdims=True), sink)
            e = jnp.exp(s - m)
            l = jnp.sum(e, axis=1, keepdims=True) + jnp.exp(sink - m)
            outs.append(_dot(e.astype(_BF16), vs[n][(h // 2) // pairs_per_kv]) / l)
        for p in range(B_HEADS // 2):
            o_ref[0, n * w:(n + 1) * w, p * LANES:(p + 1) * LANES] = jnp.where(
                lane < HEAD_DIM, outs[2 * p], outs[2 * p + 1]).astype(o_ref.dtype)


def _swa_call(q, kdup, vdup, sinks):
    bsz, seq, width = q.shape
    w = SWA_WINDOW
    kvw = kdup.shape[2]
    nwin = SWA_WINDOWS_PER_STEP
    tq = nwin * w
    prev = pl.BlockSpec((1, w, kvw), lambda b, i: (b, jnp.maximum(i * nwin - 1, 0), 0))
    cur = pl.BlockSpec((1, tq, kvw), lambda b, i: (b, i, 0))
    return pl.pallas_call(
        _swa_body,
        out_shape=jax.ShapeDtypeStruct((bsz, seq, width), _BF16),
        grid=(bsz, seq // tq),
        in_specs=[
            pl.BlockSpec(memory_space=pltpu.SMEM),
            pl.BlockSpec((1, tq, width), lambda b, i: (b, i, 0)),
            prev, cur, prev, cur,
        ],
        out_specs=pl.BlockSpec((1, tq, width), lambda b, i: (b, i, 0)),
        compiler_params=_cparams(("arbitrary", "arbitrary")),
        name="swa_sink",
    )(sinks, q, kdup, kdup, vdup, vdup)


def _outproj_body(o1_ref, o2_ref, g_ref, x_ref, mod_ref, w_ref, lng_ref, lnb_ref, y_ref):
    half = o1_ref.shape[2]
    a1 = o1_ref[0] * g_ref[0, :, :half]
    a2 = o2_ref[0] * g_ref[0, :, half:]
    y = _dot(a1, w_ref[:half, :]) + _dot(a2, w_ref[half:, :])
    gate = mod_ref[0, 2:3, :]
    z = DEEPNORM_ALPHA * x_ref[0] + (1.0 + gate) * y
    mu = jnp.mean(z, axis=1, keepdims=True)
    zc = z - mu
    var = jnp.mean(zc * zc, axis=1, keepdims=True)
    y_ref[0] = zc * lax.rsqrt(var + LN_EPS) * lng_ref[...] + lnb_ref[...]


def _outproj_call(o1, o2, g, x, mod, w_out, ln_g, ln_b):
    bsz, seq, d = x.shape
    tm = PROJ_TM
    half = o1.shape[2]
    row = lambda n: pl.BlockSpec((1, tm, n), lambda b, i: (b, i, 0))
    return pl.pallas_call(
        _outproj_body,
        out_shape=jax.ShapeDtypeStruct((bsz, seq, d), _F32),
        grid=(bsz, seq // tm),
        in_specs=[
            row(half), row(half), row(2 * half), row(d),
            pl.BlockSpec((1, 3, d), lambda b, i: (b, 0, 0)),
            pl.BlockSpec((2 * half, d), lambda b, i: (0, 0)),
            pl.BlockSpec((1, d), lambda b, i: (0, 0)),
            pl.BlockSpec((1, d), lambda b, i: (0, 0)),
        ],
        out_specs=row(d),
        compiler_params=_cparams(("arbitrary", "arbitrary")),
        name="outproj_deepnorm",
    )(o1, o2, g, x, mod, w_out.astype(_BF16), ln_g.reshape(1, d), ln_b.reshape(1, d))


ODD_COLS = 4096


def _prep_w_odd(w):
    offs = [0]
    for n in ODD_SIZES:
        offs.append(offs[-1] + n)
    w = w.astype(_BF16)
    cqn, cqr, ckv, ckr, iq, ik, iw, dq, dk, dv, gate = [w[:, offs[i]:offs[i + 1]] for i in range(len(ODD_SIZES))]
    iw_blk = jnp.concatenate([iw, jnp.zeros((w.shape[0], LANES - IDX_HEADS), w.dtype)], axis=1)
    return jnp.concatenate([cqn, cqr, ckv, jnp.tile(ckr, (1, 4)), iq, jnp.tile(ik, (1, 4)), iw_blk,
                            dq, dk, dv, gate], axis=1)


def _inproj_odd_body(x_ref, mod_ref, w_ref, cosr_ref, sinr_ref, cosi_ref, sini_ref, kvg_ref,
                     cqn_ref, cqr_ref, kvc_ref, ckvt_ref, iq_ref, ik_ref, iw_ref,
                     dqt_ref, dk_ref, dvt_ref, g_ref):
    h = _modulated(x_ref, mod_ref)
    cosr, sinr = cosr_ref[...], sinr_ref[...]
    cosi, sini = cosi_ref[...], sini_ref[...]

    acc = _dot(h, w_ref[:, 0:512])
    for p in range(4):
        cqn_ref[0, p * LANES:(p + 1) * LANES, :] = acc[:, p * LANES:(p + 1) * LANES].T.astype(_BF16)
    acc = _dot(h, w_ref[:, 512:1024])
    for p in range(2):
        piece = _rope_piece(acc[:, p * LANES:(p + 1) * LANES], cosr, sinr, C_ROPE_DIM)
        cqr_ref[0, p * LANES:(p + 1) * LANES, :] = piece.T.astype(_BF16)
    ckv = acc[:, 256:384]
    ckv = ckv * lax.rsqrt(jnp.mean(ckv * ckv, axis=1, keepdims=True) + LN_EPS) * kvg_ref[...]
    kvc_ref[0, :, 0:LANES] = ckv.astype(_BF16)
    ckvt_ref[0, 0] = ckv.T.astype(_BF16)
    kvc_ref[0, :, LANES:2 * LANES] = _rope_piece(acc[:, 384:512], cosr, sinr, C_ROPE_DIM).astype(_BF16)
    acc = _dot(h, w_ref[:, 1024:1536])
    for p in range(2):
        piece = _rope_piece(acc[:, p * LANES:(p + 1) * LANES], cosi, sini, IDX_DIM)
        iq_ref[0, p * LANES:(p + 1) * LANES, :] = piece.T.astype(_BF16)
    ik_ref[0] = _rope_piece(acc[:, 256:384], cosi, sini, IDX_DIM).astype(_BF16)
    iw_ref[0] = acc[:, 384:512].T[0:IDX_HEADS, :]
    acc = _dot(h, w_ref[:, 1536:2048]) * (HEAD_DIM ** -0.5)
    for p in range(4):
        dqt_ref[0, p * LANES:(p + 1) * LANES, :] = acc[:, p * LANES:(p + 1) * LANES].T.astype(_BF16)
    dk_ref[0] = _dot(h, w_ref[:, 2048:2560]).astype(_BF16)
    _store_blocks_t(_dot(h, w_ref[:, 2560:3072]), dvt_ref, SB_T)
    _silu_gate(h, w_ref, 3072, g_ref)


def _inproj_odd_call(x, mod, w, rope_r, rope_i, kv_g):
    bsz, seq, d = x.shape
    tm = DSA_CK
    row = lambda n: pl.BlockSpec((1, tm, n), lambda b, i: (b, i, 0))
    col = lambda n: pl.BlockSpec((1, n, tm), lambda b, i: (b, 0, i))
    tab = pl.BlockSpec((tm, LANES), lambda b, i: (i, 0))
    tok = lambda n, dt: jax.ShapeDtypeStruct((bsz, seq, n), dt)
    feat = lambda n, dt: jax.ShapeDtypeStruct((bsz, n, seq), dt)
    return pl.pallas_call(
        _inproj_odd_body,
        out_shape=[
            feat(512, _BF16), feat(256, _BF16), tok(256, _BF16),
            jax.ShapeDtypeStruct((bsz, seq // tm, C_KV_LATENT, tm), _BF16),
            feat(256, _BF16), tok(128, _BF16), feat(IDX_HEADS, _F32),
            feat(512, _BF16), tok(512, _BF16),
            jax.ShapeDtypeStruct((bsz, seq // SB_T, 512, SB_T), _BF16), tok(1024, _BF16),
        ],
        grid=(bsz, seq // tm),
        in_specs=[
            row(d),
            pl.BlockSpec((1, 3, d), lambda b, i: (b, 0, 0)),
            pl.BlockSpec((d, ODD_COLS), lambda b, i: (0, 0)),
            tab, tab, tab, tab,
            pl.BlockSpec((1, LANES), lambda b, i: (0, 0)),
        ],
        out_specs=[
            col(512), col(256), row(256),
            pl.BlockSpec((1, 1, C_KV_LATENT, tm), lambda b, i: (b, i, 0, 0)),
            col(256), row(128), col(IDX_HEADS),
            col(512), row(512),
            pl.BlockSpec((1, tm // SB_T, 512, SB_T), lambda b, i: (b, i, 0, 0)), row(1024),
        ],
        compiler_params=_cparams(("arbitrary", "arbitrary")),
        name="inproj_odd",
    )(x, mod, w, rope_r[0], rope_r[1], rope_i[0], rope_i[1], kv_g.reshape(1, LANES))


def _dsa_body(cqn_ref, cqr_ref, iq_ref, iw_ref, kvc_ref, ckvt_ref, ik_ref, wuk_ref, wuvt_ref, o_ref,
              qct_ref, iqt_ref, sc_ref, acc_ref):
    i = pl.program_id(1)
    tq, ck = DSA_TQ, DSA_CK
    seq = kvc_ref.shape[1]
    n_sel = float(min(DSA_TOPK, seq // 4))
    nch = i // (ck // tq) + 1
    att_scale = (C_NOPE_DIM + C_ROPE_DIM) ** -0.5
    idx_scale = (IDX_DIM * IDX_HEADS) ** -0.5
    hq = C_HEADS * tq

    row128 = _row_iota((LANES, tq))
    for h in range(C_HEADS):
        cols = slice(h * tq, (h + 1) * tq)
        qn_t = cqn_ref[0, h * C_NOPE_DIM:(h + 1) * C_NOPE_DIM, :]
        qlat_t = _dot(wuk_ref[h], qn_t) * att_scale
        qct_ref[0:C_KV_LATENT, cols] = qlat_t.astype(_BF16)
        qr_t = cqr_ref[0, h * C_ROPE_DIM:(h + 1) * C_ROPE_DIM, :].astype(_F32) * att_scale
        qct_ref[C_KV_LATENT:C_KV_LATENT + C_ROPE_DIM, cols] = qr_t.astype(_BF16)
        qct_ref[C_KV_LATENT + C_ROPE_DIM:, cols] = jnp.zeros((LANES - C_ROPE_DIM, tq), _BF16)
        blk4 = h // 4
        iq_t = iq_ref[0, blk4 * LANES:(blk4 + 1) * LANES, :]
        iqt_ref[:, cols] = jnp.where((row128 // IDX_DIM) == (h % 4), iq_t, jnp.zeros_like(iq_t))

    t_pos = i * tq + _lane_iota((ck, tq))
    key_row = _row_iota((ck, tq))

    def score_body(c, carry):
        vmax, vmin = carry
        off = pl.multiple_of(c * ck, ck)
        rel = _dot(ik_ref[0, pl.ds(off, ck), :], iqt_ref[...])
        score = jnp.zeros((ck, tq), _F32)
        for h in range(IDX_HEADS):
            score = score + jnp.maximum(rel[:, h * tq:(h + 1) * tq], 0.0) * iw_ref[0, h:h + 1, :]
        score = score * idx_scale
        low = jnp.where(off + key_row <= t_pos, score, -jnp.inf)
        sc_ref[c] = low
        vmax = jnp.maximum(vmax, jnp.max(low, axis=0, keepdims=True))
        vmin = jnp.minimum(vmin, jnp.min(score, axis=0, keepdims=True))
        return vmax, vmin

    vmax, vmin = lax.fori_loop(0, nch, score_body,
                               (jnp.full((1, tq), -jnp.inf, _F32), jnp.full((1, tq), jnp.inf, _F32)))

    def count(pred):
        rows_acc = 8 * (8 * LANES // tq)

        slab_row = _row_iota((rows_acc, tq))

        def body(c, acc):
            for r in range(ck // rows_acc):
                rows = slice(r * rows_acc, (r + 1) * rows_acc)
                acc = acc + jnp.where(pred(sc_ref[c, rows, :], c * ck + r * rows_acc + slab_row), 1.0, 0.0)
            return acc

        acc = lax.fori_loop(0, nch, body, jnp.zeros((rows_acc, tq), _F32))
        return jnp.sum(acc, axis=0, keepdims=True)

    def ordered_float(u):
        key = u ^ INT_MIN
        return pltpu.bitcast(key ^ ((key >> 31) & 0x7FFFFFFF), _F32)

    n_valid = (i * tq + _lane_iota((1, tq)) + 1).astype(_F32)
    few = n_valid < n_sel
    c_ge0 = count(lambda sc, pos: sc >= 0.0)
    c_gt0 = count(lambda sc, pos: sc > 0.0)
    zero_tie = (c_gt0 < n_sel) & (c_ge0 >= n_sel)
    above = c_gt0 >= n_sel

    def bisect_body(_, carry):
        lo, hi, cnt_lo = carry
        mid = lo + 0.5 * (hi - lo)
        cnt = count(lambda sc, pos: sc >= mid)
        ok = cnt >= n_sel
        return jnp.where(ok, mid, lo), jnp.where(ok, hi, mid), jnp.where(ok, cnt, cnt_lo)

    def all_resolved(cnt_lo):
        done = few | zero_tie | (cnt_lo == n_sel)
        return jnp.min(jnp.where(done, 1.0, 0.0)) > 0.0

    state = (jnp.where(above, 0.0, vmin), jnp.where(above, vmax, 0.0), jnp.where(above, c_ge0, n_valid))
    state = lax.fori_loop(0, BISECT_STEPS, bisect_body, state)
    state = lax.cond(all_resolved(state[2]), lambda s: s,
                     lambda s: lax.fori_loop(0, BISECT_EXTRA, bisect_body, s), state)
    lo, _, cnt_lo = state

    def exact_search():
        def bit_body(b, carry):
            t, cnt_t = carry
            cand = t | jnp.left_shift(jnp.int32(1), 31 - b)
            cand_f = ordered_float(cand)
            cnt = count(lambda sc, pos: sc >= cand_f)
            ok = cnt >= n_sel
            return jnp.where(ok, cand, t), jnp.where(ok, cnt, cnt_t)

        t0 = jnp.zeros((1, tq), jnp.int32)
        t, cnt_t = lax.fori_loop(0, 32, bit_body, (t0, jnp.full((1, tq), float(seq), _F32)))
        thr = jnp.where(t == 0, -jnp.inf, ordered_float(t))
        return thr, cnt_t, n_sel - count(lambda sc, pos: sc > thr)

    thr, cnt_thr, room = lax.cond(
        all_resolved(cnt_lo),
        lambda: (jnp.where(few, -jnp.inf, jnp.where(zero_tie, 0.0, lo)), jnp.where(zero_tie, c_ge0, cnt_lo),
                 jnp.where(zero_tie, n_sel - c_gt0, n_sel)),
        exact_search)

    excess = jnp.max(jnp.where((cnt_thr > n_sel) & (thr > -jnp.inf), 1.0, 0.0)) > 0.0

    def plain_bias():
        def body(c, _):
            chosen = (sc_ref[c] >= thr) & (c * ck + key_row <= t_pos)
            sc_ref[c] = jnp.where(chosen, 0.0, NEG_BIG)
            return 0

        lax.fori_loop(0, nch, body, 0)

    def tie_bias():
        upto =jnp.where(_lane_iota((ck, ck)) <= _row_iota((ck, ck)), 1.0, 0.0).astype(_BF16)

        def body(c, seen):
            sc = sc_ref[c]
            causal = c * ck + key_row <= t_pos
            tied = (sc == thr) & causal
            one = jnp.where(tied, 1.0, 0.0)
            rank = _dot(upto, one.astype(_BF16)) + seen
            chosen = ((sc > thr) & causal) | (tied & (rank <= room))
            sc_ref[c] = jnp.where(chosen, 0.0, NEG_BIG)
            return seen + jnp.sum(one, axis=0, keepdims=True)

        lax.fori_loop(0, nch, body, jnp.zeros((1, tq), _F32))

    lax.cond(excess, tie_bias, plain_bias)

    acc_ref[...] = jnp.zeros_like(acc_ref)
    grp = 2 * tq
    gcols = [slice(g * grp, (g + 1) * grp) for g in range(hq // grp)]

    def att_body(c, carry):
        off = pl.multiple_of(c * ck, ck)
        kv = kvc_ref[0, pl.ds(off, ck), :]
        bias = sc_ref[c]
        bias2 = jnp.concatenate([bias, bias], axis=1)
        ss = [_dot(kv, qct_ref[:, cols]) + bias2 for cols in gcols]
        ms, ls, es, alphas = [], [], [], []
        for g, cols in enumerate(gcols):
            m, l = carry[0][:, cols], carry[1][:, cols]
            m_new = jnp.maximum(m, jnp.max(ss[g], axis=0, keepdims=True))
            alpha = jnp.exp(m - m_new)
            e = jnp.exp(ss[g] - m_new)
            ls.append(alpha * l + jnp.sum(e, axis=0, keepdims=True))
            ms.append(m_new)
            alphas.append(alpha)
            es.append(e.astype(_BF16))
        kv_t = ckvt_ref[0, c]
        for g, cols in enumerate(gcols):
            acc_ref[:, cols] = alphas[g] * acc_ref[:, cols] + _dot(kv_t, es[g])
        return jnp.concatenate(ms, axis=1), jnp.concatenate(ls, axis=1)

    m0 = jnp.full((1, hq), NEG_BIG, _F32)
    l0 = jnp.zeros((1, hq), _F32)
    _, l = lax.fori_loop(0, nch, att_body, (m0, l0))
    o_lat_t = (acc_ref[...] / l).astype(_BF16)
    outs = [_dot(wuvt_ref[h], o_lat_t[:, h * tq:(h + 1) * tq]) for h in range(C_HEADS)]
    o_ref[0] = jnp.concatenate(outs, axis=0).T.astype(o_ref.dtype)


def _dsa_call(cqn_t, cqr_t, iq_t, iw_t, kvc, ckv_t, ik, w_uk, wuv_t):
    bsz, seq, _ = kvc.shape
    tq = DSA_TQ
    qcol = lambda n: pl.BlockSpec((1, n, tq), lambda b, i: (b, 0, i))
    full = lambda n: pl.BlockSpec((1, seq, n), lambda b, i: (b, 0, 0))
    whole = lambda a: pl.BlockSpec(a.shape, lambda b, i: (0,) * a.ndim)
    width = C_HEADS * C_V_DIM
    return pl.pallas_call(
        _dsa_body,
        out_shape=jax.ShapeDtypeStruct((bsz, seq, width), _BF16),
        grid=(bsz, seq // tq),
        in_specs=[
            qcol(512), qcol(256), qcol(256), qcol(IDX_HEADS), full(256),
            pl.BlockSpec((1,) + ckv_t.shape[1:], lambda b, i: (b, 0, 0, 0)),
            full(128), whole(w_uk), whole(wuv_t),
        ],
        out_specs=pl.BlockSpec((1, tq, width), lambda b, i: (b, i, 0)),
        scratch_shapes=[
            pltpu.VMEM((2 * LANES, C_HEADS * tq), _BF16),
            pltpu.VMEM((LANES, IDX_HEADS * tq), _BF16),
            pltpu.VMEM((seq // DSA_CK, DSA_CK, tq), _F32),
            pltpu.VMEM((C_KV_LATENT, C_HEADS * tq), _F32),
        ],
        compiler_params=_cparams(("arbitrary", "arbitrary")),
        name="dsa",
    )(cqn_t, cqr_t, iq_t, iw_t, kvc, ckv_t, ik, w_uk, wuv_t)


def _sb_body(qt_ref, k_ref, vt_ref, o_ref, acc_ref):
    i = pl.program_id(1)
    t = SB_T
    npair = D_HEADS // 2
    width = 2 * t
    qcats = [_head_pair_rhs(qt_ref[0, p * LANES:(p + 1) * LANES, :], HEAD_DIM) for p in range(npair)]
    key_row = _row_iota((t, width))
    q_lane = _lane_iota((t, width)) % t
    later = jnp.where(_lane_iota((t, t)) > _row_iota((t, t)), 1.0, 0.0).astype(_BF16)
    later2 = jnp.concatenate([later, later], axis=1)
    acc_ref[...] = jnp.zeros_like(acc_ref)

    def blocks(js, rest, masks):
        ks = [k_ref[0, pl.ds(pl.multiple_of(j * t, t), t), :] for j in js]
        zs = [[_dot(kj[:, p * LANES:(p + 1) * LANES], qcats[p]) for p in range(npair)] for kj in ks]
        rests = []
        for p in range(npair):
            run = rest[:, p * width:(p + 1) * width]
            for n, j in enumerate(js):
                z = zs[n][p]
                log_keep = -(jnp.maximum(z, 0.0) + jnp.log(1.0 + jnp.exp(-jnp.abs(z))))
                log_beta = z + log_keep
                if masks[n] is not None:
                    log_keep = jnp.where(masks[n], log_keep, 0.0)
                hi = log_keep.astype(_BF16)
                lo = (log_keep - hi.astype(_F32)).astype(_BF16)
                a = jnp.exp(log_beta + (_dot(later2, jnp.concatenate([hi, lo], axis=0)) + run))
                if masks[n] is not None:
                    a = jnp.where(masks[n], a, 0.0)
                prows = slice(p * LANES, (p + 1) * LANES)
                pv = _dot(vt_ref[0, j, prows, :], a.astype(_BF16))
                for hh in range(2):
                    rows = slice((2 * p + hh) * HEAD_DIM, (2 * p + hh + 1) * HEAD_DIM)
                    acc_ref[rows, :] = acc_ref[rows, :] + pv[hh * HEAD_DIM:(hh + 1) * HEAD_DIM, hh * t:(hh + 1) * t]
                run = run + jnp.sum(log_keep, axis=0, keepdims=True)
            rests.append(run)
        return jnp.concatenate(rests, axis=1)

    rest = blocks([i], jnp.zeros((1, npair * width), _F32), [key_row < q_lane])

    def cond(carry):
        j, _, top = carry
        return (j >= 0) & (top > SB_UNDERFLOW)

    def body(carry):
        j, rest, _ = carry
        second_ok = jnp.broadcast_to(j >= 1, (t, width))
        rest = blocks([j, jnp.maximum(j - 1, 0)], rest, [None, second_ok])
        return j - 2, rest, jnp.max(rest)

    lax.while_loop(cond, body, (i - 1, rest, jnp.max(rest)))
    o_ref[0] = acc_ref[...].T.astype(o_ref.dtype)


def _sb_call(q_t, k, v_t):
    bsz, seq, width = k.shape
    t = SB_T
    return pl.pallas_call(
        _sb_body,
        out_shape=jax.ShapeDtypeStruct((bsz, seq, width), _BF16),
        grid=(bsz, seq // t),
        in_specs=[
            pl.BlockSpec((1, width, t), lambda b, i: (b, 0, i)),
            pl.BlockSpec((1, seq, width), lambda b, i: (b, 0, 0)),
            pl.BlockSpec((1, seq // t, width, t), lambda b, i: (b, 0, 0, 0)),
        ],
        out_specs=pl.BlockSpec((1, t, width), lambda b, i: (b, i, 0)),
        scratch_shapes=[pltpu.VMEM((width, t), _F32)],
        compiler_params=_cparams(("arbitrary", "arbitrary")),
        name="stick_breaking",
    )(q_t, k, v_t)


def _even_layer(x, mod, w_in, sinks, w_out, ln_g, ln_b, rope_h):
    aq_t, ak, bq, bk, av_t, bv, g = _inproj_even_call(x, mod, w_in.astype(_BF16), *rope_h)
    oa = _moba_call(aq_t, ak, av_t)
    ob = _swa_call(bq, bk, bv, sinks)
    return _outproj_call(oa, ob, g, x, mod, w_out, ln_g, ln_b)


def _odd_layer(x, mod, w_in, kv_g, w_uk, w_uv, w_out, ln_g, ln_b, rope_r, rope_i):
    cqn_t, cqr_t, kvc, ckv_t, iq_t, ik, iw_t, dq_t, dk, dv_t, g = _inproj_odd_call(
        x, mod, _prep_w_odd(w_in), rope_r, rope_i, kv_g)
    oc = _dsa_call(cqn_t, cqr_t, iq_t, iw_t, kvc, ckv_t, ik,
                   w_uk.astype(_BF16), w_uv.transpose(0, 2, 1).astype(_BF16))
    od = _sb_call(dq_t, dk, dv_t)
    return _outproj_call(oc, od, g, x, mod, w_out, ln_g, ln_b)


def kernel(x, c, w_ada, b_ada, w_in_even, sink_logits, w_in_odd, kv_norm_g, w_uk, w_uv, w_out, ln_g, ln_b):
    bsz, seq, d = x.shape
    rope_h = _rope_tables(seq, HEAD_DIM, LANES)
    rope_r = _rope_tables(seq, C_ROPE_DIM, LANES)
    rope_i = _rope_tables(seq, IDX_DIM, LANES)
    mods = _ada_call(c, w_ada, b_ada).reshape(DEPTH, bsz, 3, d)
    for layer in range(DEPTH):
        mod = mods[layer]
        j = layer // 2
        if layer % 2 == 0:
            x = _even_layer(x, mod, w_in_even[j], sink_logits[j], w_out[layer], ln_g[layer], ln_b[layer], rope_h)
        else:
            x = _odd_layer(x, mod, w_in_odd[j], kv_norm_g[j], w_uk[j], w_uv[j], w_out[layer],
                           ln_g[layer], ln_b[layer], rope_r, rope_i)
    return x
```

```python
import jax
import jax.numpy as jnp
from jax import lax
from jax.experimental import pallas as pl
from jax.experimental.pallas import tpu as pltpu

D_MODEL = 1024
DEPTH = 2
HEAD_DIM = 64
ROPE_THETA = 10000.0
LN_EPS = 1e-5
A_HEADS = 8
MOBA_BLOCK = 256
MOBA_TOPK = 3
B_HEADS = 8
B_KV_HEADS = 2
SWA_WINDOW = 128
C_HEADS = 8
C_NOPE_DIM = 64
C_ROPE_DIM = 32
C_V_DIM = 64
C_KV_LATENT = 128
IDX_HEADS = 8
IDX_DIM = 32
DSA_TOPK = 256
D_HEADS = 8
MIX_WIDTH = 1024
EVEN_SIZES = (512, 512, 512, 512, 128, 128, 1024)
ODD_SIZES = (512, 256, 128, 32, 256, 32, 8, 512, 512, 512, 1024)
DEEPNORM_ALPHA = (2 * DEPTH) ** 0.25

LANES = 128
NEG_BIG = -1e30
INT_MIN = -2 ** 31
SB_UNDERFLOW = -104.0

PROJ_TM = 512
DSA_TQ = 512
DSA_CK = 512
SB_T = 128
MOBA_PAIRS = 4
SWA_WINDOWS_PER_STEP = 2
BISECT_STEPS = 20
BISECT_EXTRA = 4
VMEM_LIMIT = 48 * 1024 * 1024

_BF16 = jnp.bfloat16
_F32 = jnp.float32


def _cparams(sem):
    return pltpu.CompilerParams(dimension_semantics=sem, vmem_limit_bytes=VMEM_LIMIT)


def _dot_t(a, b):
    return lax.dot_general(a, b, (((1,), (1,)), ((), ())), preferred_element_type=_F32)


def _dot(a, b):
    return jnp.dot(a, b, preferred_element_type=_F32)


def _lane_iota(shape):
    return lax.broadcasted_iota(jnp.int32, shape, len(shape) - 1)


def _row_iota(shape):
    return lax.broadcasted_iota(jnp.int32, shape, len(shape) - 2)


def _head_pair_rhs(q_t, head_dim):
    frow = _row_iota(q_t.shape)
    return jnp.concatenate(
        [jnp.where((frow // head_dim) == hh, q_t, jnp.zeros_like(q_t)) for hh in range(2)], axis=1)


def _ada_body(c_ref, w_ref, b_ref, o_ref):
    c = c_ref[...]
    cond = c * jax.nn.sigmoid(c)
    o_ref[0] = _dot(cond, w_ref[0]) + b_ref[0]


def _ada_call(c, w_ada, b_ada):
    depth, d, n3 = w_ada.shape
    bsz = c.shape[0]
    nb = n3 // d
    return pl.pallas_call(
        _ada_body,
        out_shape=jax.ShapeDtypeStruct((depth, bsz, n3), _F32),
        grid=(depth, nb),
        in_specs=[
            pl.BlockSpec((bsz, d), lambda l, j: (0, 0)),
            pl.BlockSpec((1, d, d), lambda l, j: (l, 0, j)),
            pl.BlockSpec((1, 1, d), lambda l, j: (l, 0, j)),
        ],
        out_specs=pl.BlockSpec((1, bsz, d), lambda l, j: (l, 0, j)),
        compiler_params=_cparams(("arbitrary", "arbitrary")),
        name="ada_mod",
    )(c, w_ada, b_ada.reshape(depth, 1, n3))


def _rope_tables(seq, dim, period_lanes):
    inv = 1.0 / (ROPE_THETA ** (jnp.arange(0, dim, 2, dtype=_F32) / dim))
    ang = jnp.arange(seq, dtype=_F32)[:, None] * inv[None, :]
    cos, sin = jnp.cos(ang), jnp.sin(ang)
    cos_h = jnp.concatenate([cos, cos], axis=1)
    sin_h = jnp.concatenate([-sin, sin], axis=1)
    reps = period_lanes // dim
    return jnp.tile(cos_h, (1, reps)), jnp.tile(sin_h, (1, reps))


def _rope_piece(x, cos, sin_signed, dim):
    half = dim // 2
    first = (_lane_iota(x.shape) % dim) < half
    partner = jnp.where(first, pltpu.roll(x, LANES - half, 1), pltpu.roll(x, half, 1))
    return x * cos + partner * sin_signed


def _modulated(x_ref, mod_ref):
    x = x_ref[0]
    shift = mod_ref[0, 0:1, :]
    scale = mod_ref[0, 1:2, :]
    return (x * (1.0 + scale) + shift).astype(_BF16)


def _silu_gate(h, w_ref, col0, g_ref):
    for p in range(2):
        gate = _dot(h, w_ref[:, col0 + 512 * p:col0 + 512 * (p + 1)])
        g_ref[0, :, 512 * p:512 * (p + 1)] = (gate * jax.nn.sigmoid(gate)).astype(_BF16)


def _store_blocks_t(acc, out_ref, blk):
    tm, n = acc.shape
    for r in range(tm // blk):
        for p in range(n // LANES):
            out_ref[0, r, p * LANES:(p + 1) * LANES, :] = (
                acc[r * blk:(r + 1) * blk, p * LANES:(p + 1) * LANES].T.astype(_BF16))


EVEN_COLS = sum(EVEN_SIZES)


def _inproj_even_body(x_ref, mod_ref, w_ref, cos_ref, sin_ref,
                      aqt_ref, ak_ref, bq_ref, bk_ref, avt_ref, bv_ref, g_ref):
    h = _modulated(x_ref, mod_ref)
    cos = cos_ref[...]
    sin = sin_ref[...]
    q_scale = HEAD_DIM ** -0.5

    def roped(col0, ncols, out_ref, scale, transposed=False):
        acc = _dot(h, w_ref[:, col0:col0 + ncols])
        for p in range(ncols // LANES):
            piece = _rope_piece(acc[:, p * LANES:(p + 1) * LANES], cos, sin, HEAD_DIM)
            if scale != 1.0:
                piece = piece * scale
            if transposed:
                out_ref[0, p * LANES:(p + 1) * LANES, :] = piece.T.astype(_BF16)
            else:
                out_ref[0, :, p * LANES:(p + 1) * LANES] = piece.astype(_BF16)

    def dup_kv_heads(x, out_ref):
        first = _lane_iota(x.shape) < HEAD_DIM
        swapped = pltpu.roll(x, HEAD_DIM, 1)
        out_ref[0, :, 0:LANES] = jnp.where(first, x, swapped).astype(_BF16)
        out_ref[0, :, LANES:2 * LANES] = jnp.where(first, swapped, x).astype(_BF16)

    roped(0, 512, aqt_ref, q_scale, transposed=True)
    roped(512, 512, ak_ref, 1.0)
    _store_blocks_t(_dot(h, w_ref[:, 1024:1536]), avt_ref, MOBA_BLOCK)
    roped(1536, 512, bq_ref, q_scale)
    dup_kv_heads(_rope_piece(_dot(h, w_ref[:, 2048:2176]), cos, sin, HEAD_DIM), bk_ref)
    dup_kv_heads(_dot(h, w_ref[:, 2176:2304]), bv_ref)
    _silu_gate(h, w_ref, 2304, g_ref)


def _inproj_even_call(x, mod, w, cos, sin):
    bsz, seq, d = x.shape
    tm = PROJ_TM
    row = lambda n: pl.BlockSpec((1, tm, n), lambda b, i: (b, i, 0))
    tok = lambda n: jax.ShapeDtypeStruct((bsz, seq, n), _BF16)
    per_tile = tm // MOBA_BLOCK
    return pl.pallas_call(
        _inproj_even_body,
        out_shape=[
            jax.ShapeDtypeStruct((bsz, 512, seq), _BF16), tok(512), tok(512), tok(256),
            jax.ShapeDtypeStruct((bsz, seq // MOBA_BLOCK, 512, MOBA_BLOCK), _BF16), tok(256), tok(1024),
        ],
        grid=(bsz, seq // tm),
        in_specs=[
            row(d),
            pl.BlockSpec((1, 3, d), lambda b, i: (b, 0, 0)),
            pl.BlockSpec((d, EVEN_COLS), lambda b, i: (0, 0)),
            pl.BlockSpec((tm, LANES), lambda b, i: (i, 0)),
            pl.BlockSpec((tm, LANES), lambda b, i: (i, 0)),
        ],
        out_specs=[
            pl.BlockSpec((1, 512, tm), lambda b, i: (b, 0, i)), row(512), row(512), row(256),
            pl.BlockSpec((1, per_tile, 512, MOBA_BLOCK), lambda b, i: (b, i, 0, 0)), row(256), row(1024),
        ],
        compiler_params=_cparams(("arbitrary", "arbitrary")),
        name="inproj_even",
    )(x, mod, w, cos, sin)


def _moba_body(qt_ref, k_ref, vt_ref, o_ref, km_ref, selb_ref, acc_ref):
    qi = pl.program_id(2)
    tq = MOBA_BLOCK
    nblk = km_ref.shape[0]
    npair = qt_ref.shape[1] // LANES
    pw = 2 * tq
    width = npair * pw

    @pl.when(qi == 0)
    def _():
        for r in range(nblk):
            blk_k = k_ref[0, r * tq:(r + 1) * tq, :].astype(_F32)
            km_ref[r:r + 1, :] = jnp.mean(blk_k, axis=0, keepdims=True)

    qcats = [_head_pair_rhs(qt_ref[0, p * LANES:(p + 1) * LANES, :], HEAD_DIM) for p in range(npair)]

    blk = _row_iota((nblk, width))
    blkf = blk.astype(_F32)
    valid = blk < qi
    gate = jnp.concatenate([_dot(km_ref[:, p * LANES:(p + 1) * LANES].astype(_BF16), qcats[p])
                            for p in range(npair)], axis=1)
    gate = jnp.where(valid, gate, -jnp.inf)
    sel = jnp.zeros((nblk, width), _F32)
    for _ in range(MOBA_TOPK):
        top = jnp.max(gate, axis=0, keepdims=True)
        first = jnp.min(jnp.where(gate == top, blkf, float(nblk)), axis=0, keepdims=True)
        pick = blkf == first
        sel = jnp.where(pick, 1.0, sel)
        gate = jnp.where(pick, -jnp.inf, gate)
    selb_ref[...] = jnp.where(valid & (sel > 0.0), 0.0, NEG_BIG)

    acc_ref[...] = jnp.zeros_like(acc_ref)

    def step(carry, blocks):
        m, l = carry
        ks = [k_ref[0, pl.ds(pl.multiple_of(j * tq, tq), tq), :] for j, _ in blocks]
        ss = [[_dot(kj[:, p * LANES:(p + 1) * LANES], qcats[p]) + bias(p) for kj, (_, bias) in zip(ks, blocks)]
              for p in range(npair)]
        ms, ls = [], []
        for p in range(npair):
            pcols = slice(p * pw, (p + 1) * pw)
            m_new = m[:, pcols]
            for s in ss[p]:
                m_new = jnp.maximum(m_new, jnp.max(s, axis=0, keepdims=True))
            alpha = jnp.exp(m[:, pcols] - m_new)
            es = [jnp.exp(s - m_new) for s in ss[p]]
            l_new = alpha * l[:, pcols]
            for e in es:
                l_new = l_new + jnp.sum(e, axis=0, keepdims=True)
            ls.append(l_new)
            ms.append(m_new)
            e = jnp.concatenate([e.astype(_BF16) for e in es], axis=0)
            for hh in range(2):
                rows = slice((2 * p + hh) * HEAD_DIM, (2 * p + hh + 1) * HEAD_DIM)
                cols = slice(hh * tq, (hh + 1) * tq)
                v_t = jnp.concatenate([vt_ref[0, j, rows, :] for j, _ in blocks], axis=1)
                acc_ref[rows, :] = alpha[:, cols] * acc_ref[rows, :] + _dot(v_t, e[:, cols])
        return jnp.concatenate(ms, axis=1), jnp.concatenate(ls, axis=1)

    def row_bias(j):
        row = selb_ref[pl.ds(j, 1), :]
        return lambda p: row[:, p * pw:(p + 1) * pw]

    def pair_body(t, carry):
        return step(carry, [(2 * t, row_bias(2 * t)), (2 * t + 1, row_bias(2 * t + 1))])

    carry = (jnp.full((1, width), NEG_BIG, _F32), jnp.zeros((1, width), _F32))
    carry = lax.fori_loop(0, qi // 2, pair_body, carry)
    causal = jnp.where(_row_iota((tq, pw)) <= _lane_iota((tq, pw)) % tq, 0.0, NEG_BIG)
    own = (qi, lambda p: causal)
    _, l = lax.cond(qi % 2 == 1,
                    lambda c: step(c, [own, (qi - 1, row_bias(qi - 1))]),
                    lambda c: step(c, [own]), carry)
    inv = 1.0 / l
    out_t = jnp.concatenate([acc_ref[h * HEAD_DIM:(h + 1) * HEAD_DIM, :] * inv[:, h * tq:(h + 1) * tq]
                             for h in range(2 * npair)], axis=0)
    o_ref[0] = out_t.T.astype(o_ref.dtype)


def _moba_call(q_t, k, v_t):
    bsz, seq, width = k.shape
    tq = MOBA_BLOCK
    nblk = seq // tq
    gw = MOBA_PAIRS * LANES
    return pl.pallas_call(
        _moba_body,
        out_shape=jax.ShapeDtypeStruct((bsz, seq, width), _BF16),
        grid=(bsz, width // gw, nblk),
        in_specs=[
            pl.BlockSpec((1, gw, tq), lambda b, h, i: (b, h, i)),
            pl.BlockSpec((1, seq, gw), lambda b, h, i: (b, 0, h)),
            pl.BlockSpec((1, nblk, gw, tq), lambda b, h, i: (b, 0, h, 0)),
        ],
        out_specs=pl.BlockSpec((1, tq, gw), lambda b, h, i: (b, i, h)),
        scratch_shapes=[
            pltpu.VMEM((nblk, gw), _F32),
            pltpu.VMEM((nblk, MOBA_PAIRS * 2 * tq), _F32),
            pltpu.VMEM((gw, tq), _F32),
        ],
        compiler_params=_cparams(("arbitrary", "arbitrary", "arbitrary")),
        name="moba",
    )(q_t, k, v_t)


def _swa_body(sink_ref, q_ref, kp_ref, kc_ref, vp_ref, vc_ref, o_ref):
    i = pl.program_id(1)
    w = SWA_WINDOW
    lane = _lane_iota((1, LANES))
    row = _row_iota((w, 2 * w))
    col = _lane_iota((w, 2 * w))
    in_window = (col > row) & (col <= row + w)
    pairs_per_kv = (B_HEADS // B_KV_HEADS) // 2
    nwin = q_ref.shape[1] // w

    def band(prev_ref, cur_ref, n, g):
        glanes = slice(g * LANES, (g + 1) * LANES)
        before = prev_ref[0, :, glanes] if n == 0 else cur_ref[0, (n - 1) * w:n * w, glanes]
        return jnp.concatenate([before, cur_ref[0, n * w:(n + 1) * w, glanes]], axis=0)

    ss, vs = [], []
    for n in range(nwin):
        ks = [band(kp_ref, kc_ref, n, g) for g in range(B_KV_HEADS)]
        vs.append([band(vp_ref, vc_ref, n, g) for g in range(B_KV_HEADS)])
        for h in range(B_HEADS):
            qp = q_ref[0, n * w:(n + 1) * w, (h // 2) * LANES:(h // 2 + 1) * LANES]
            qh = jnp.where((lane // HEAD_DIM) == (h % 2), qp, jnp.zeros_like(qp))
            ss.append(_dot_t(qh, ks[(h // 2) // pairs_per_kv]))
    for n in range(nwin):
        mask = in_window & ((col >= w) | (i > 0)) if n == 0 else in_window
        outs = []
        for h in range(B_HEADS):
            sink = sink_ref[h]
            s = jnp.where(mask, ss[n * B_HEADS + h], -jnp.inf)
            m = jnp.maximum(jnp.max(s, axis=1, keepdims=True), sink)
            e = jnp.exp(s - m)
            l = jnp.sum(e, axis=1, keepdims=True) + jnp.exp(sink - m)
            outs.append(_dot(e.astype(_BF16), vs[n][(h // 2) // pairs_per_kv]) / l)
        for p in range(B_HEADS // 2):
            o_ref[0, n * w:(n + 1) * w, p * LANES:(p + 1) * LANES] = jnp.where(
                lane < HEAD_DIM, outs[2 * p], outs[2 * p + 1]).astype(o_ref.dtype)


def _swa_call(q, kdup, vdup, sinks):
    bsz, seq, width = q.shape
    w = SWA_WINDOW
    kvw = kdup.shape[2]
    nwin = SWA_WINDOWS_PER_STEP
    tq = nwin * w
    prev = pl.BlockSpec((1, w, kvw), lambda b, i: (b, jnp.maximum(i * nwin - 1, 0), 0))
    cur = pl.BlockSpec((1, tq, kvw), lambda b, i: (b, i, 0))
    return pl.pallas_call(
        _swa_body,
        out_shape=jax.ShapeDtypeStruct((bsz, seq, width), _BF16),
        grid=(bsz, seq // tq),
        in_specs=[
            pl.BlockSpec(memory_space=pltpu.SMEM),
            pl.BlockSpec((1, tq, width), lambda b, i: (b, i, 0)),
            prev, cur, prev, cur,
        ],
        out_specs=pl.BlockSpec((1, tq, width), lambda b, i: (b, i, 0)),
        compiler_params=_cparams(("arbitrary", "arbitrary")),
        name="swa_sink",
    )(sinks, q, kdup, kdup, vdup, vdup)


def _outproj_body(o1_ref, o2_ref, g_ref, x_ref, mod_ref, w_ref, lng_ref, lnb_ref, y_ref):
    half = o1_ref.shape[2]
    a1 = o1_ref[0] * g_ref[0, :, :half]
    a2 = o2_ref[0] * g_ref[0, :, half:]
    y = _dot(a1, w_ref[:half, :]) + _dot(a2, w_ref[half:, :])
    gate = mod_ref[0, 2:3, :]
    z = DEEPNORM_ALPHA * x_ref[0] + (1.0 + gate) * y
    mu = jnp.mean(z, axis=1, keepdims=True)
    zc = z - mu
    var = jnp.mean(zc * zc, axis=1, keepdims=True)
    y_ref[0] = zc * lax.rsqrt(var + LN_EPS) * lng_ref[...] + lnb_ref[...]


def _outproj_call(o1, o2, g, x, mod, w_out, ln_g, ln_b):
    bsz, seq, d = x.shape
    tm = PROJ_TM
    half = o1.shape[2]
    row = lambda n: pl.BlockSpec((1, tm, n), lambda b, i: (b, i, 0))
    return pl.pallas_call(
        _outproj_body,
        out_shape=jax.ShapeDtypeStruct((bsz, seq, d), _F32),
        grid=(bsz, seq // tm),
        in_specs=[
            row(half), row(half), row(2 * half), row(d),
            pl.BlockSpec((1, 3, d), lambda b, i: (b, 0, 0)),
            pl.BlockSpec((2 * half, d), lambda b, i: (0, 0)),
            pl.BlockSpec((1, d), lambda b, i: (0, 0)),
            pl.BlockSpec((1, d), lambda b, i: (0, 0)),
        ],
        out_specs=row(d),
        compiler_params=_cparams(("arbitrary", "arbitrary")),
        name="outproj_deepnorm",
    )(o1, o2, g, x, mod, w_out.astype(_BF16), ln_g.reshape(1, d), ln_b.reshape(1, d))


ODD_COLS = 4096


def _prep_w_odd(w):
    offs = [0]
    for n in ODD_SIZES:
        offs.append(offs[-1] + n)
    w = w.astype(_BF16)
    cqn, cqr, ckv, ckr, iq, ik, iw, dq, dk, dv, gate = [w[:, offs[i]:offs[i + 1]] for i in range(len(ODD_SIZES))]
    iw_blk = jnp.concatenate([iw, jnp.zeros((w.shape[0], LANES - IDX_HEADS), w.dtype)], axis=1)
    return jnp.concatenate([cqn, cqr, ckv, jnp.tile(ckr, (1, 4)), iq, jnp.tile(ik, (1, 4)), iw_blk,
                            dq, dk, dv, gate], axis=1)


def _inproj_odd_body(x_ref, mod_ref, w_ref, cosr_ref, sinr_ref, cosi_ref, sini_ref, kvg_ref,
                     cqn_ref, cqr_ref, kvc_ref, ckvt_ref, iq_ref, ik_ref, iw_ref,
                     dqt_ref, dk_ref, dvt_ref, g_ref):
    h = _modulated(x_ref, mod_ref)
    cosr, sinr = cosr_ref[...], sinr_ref[...]
    cosi, sini = cosi_ref[...], sini_ref[...]

    acc = _dot(h, w_ref[:, 0:512])
    for p in range(4):
        cqn_ref[0, p * LANES:(p + 1) * LANES, :] = acc[:, p * LANES:(p + 1) * LANES].T.astype(_BF16)
    acc = _dot(h, w_ref[:, 512:1024])
    for p in range(2):
        piece = _rope_piece(acc[:, p * LANES:(p + 1) * LANES], cosr, sinr, C_ROPE_DIM)
        cqr_ref[0, p * LANES:(p + 1) * LANES, :] = piece.T.astype(_BF16)
    ckv = acc[:, 256:384]
    ckv = ckv * lax.rsqrt(jnp.mean(ckv * ckv, axis=1, keepdims=True) + LN_EPS) * kvg_ref[...]
    kvc_ref[0, :, 0:LANES] = ckv.astype(_BF16)
    ckvt_ref[0, 0] = ckv.T.astype(_BF16)
    kvc_ref[0, :, LANES:2 * LANES] = _rope_piece(acc[:, 384:512], cosr, sinr, C_ROPE_DIM).astype(_BF16)
    acc = _dot(h, w_ref[:, 1024:1536])
    for p in range(2):
        piece = _rope_piece(acc[:, p * LANES:(p + 1) * LANES], cosi, sini, IDX_DIM)
        iq_ref[0, p * LANES:(p + 1) * LANES, :] = piece.T.astype(_BF16)
    ik_ref[0] = _rope_piece(acc[:, 256:384], cosi, sini, IDX_DIM).astype(_BF16)
    iw_ref[0] = acc[:, 384:512].T[0:IDX_HEADS, :]
    acc = _dot(h, w_ref[:, 1536:2048]) * (HEAD_DIM ** -0.5)
    for p in range(4):
        dqt_ref[0, p * LANES:(p + 1) * LANES, :] = acc[:, p * LANES:(p + 1) * LANES].T.astype(_BF16)
    dk_ref[0] = _dot(h, w_ref[:, 2048:2560]).astype(_BF16)
    _store_blocks_t(_dot(h, w_ref[:, 2560:3072]), dvt_ref, SB_T)
    _silu_gate(h, w_ref, 3072, g_ref)


def _inproj_odd_call(x, mod, w, rope_r, rope_i, kv_g):
    bsz, seq, d = x.shape
    tm = DSA_CK
    row = lambda n: pl.BlockSpec((1, tm, n), lambda b, i: (b, i, 0))
    col = lambda n: pl.BlockSpec((1, n, tm), lambda b, i: (b, 0, i))
    tab = pl.BlockSpec((tm, LANES), lambda b, i: (i, 0))
    tok = lambda n, dt: jax.ShapeDtypeStruct((bsz, seq, n), dt)
    feat = lambda n, dt: jax.ShapeDtypeStruct((bsz, n, seq), dt)
    return pl.pallas_call(
        _inproj_odd_body,
        out_shape=[
            feat(512, _BF16), feat(256, _BF16), tok(256, _BF16),
            jax.ShapeDtypeStruct((bsz, seq // tm, C_KV_LATENT, tm), _BF16),
            feat(256, _BF16), tok(128, _BF16), feat(IDX_HEADS, _F32),
            feat(512, _BF16), tok(512, _BF16),
            jax.ShapeDtypeStruct((bsz, seq // SB_T, 512, SB_T), _BF16), tok(1024, _BF16),
        ],
        grid=(bsz, seq // tm),
        in_specs=[
            row(d),
            pl.BlockSpec((1, 3, d), lambda b, i: (b, 0, 0)),
            pl.BlockSpec((d, ODD_COLS), lambda b, i: (0, 0)),
            tab, tab, tab, tab,
            pl.BlockSpec((1, LANES), lambda b, i: (0, 0)),
        ],
        out_specs=[
            col(512), col(256), row(256),
            pl.BlockSpec((1, 1, C_KV_LATENT, tm), lambda b, i: (b, i, 0, 0)),
            col(256), row(128), col(IDX_HEADS),
            col(512), row(512),
            pl.BlockSpec((1, tm // SB_T, 512, SB_T), lambda b, i: (b, i, 0, 0)), row(1024),
        ],
        compiler_params=_cparams(("arbitrary", "arbitrary")),
        name="inproj_odd",
    )(x, mod, w, rope_r[0], rope_r[1], rope_i[0], rope_i[1], kv_g.reshape(1, LANES))


def _dsa_body(cqn_ref, cqr_ref, iq_ref, iw_ref, kvc_ref, ckvt_ref, ik_ref, wuk_ref, wuvt_ref, o_ref,
              qct_ref, iqt_ref, sc_ref, acc_ref):
    i = pl.program_id(1)
    tq, ck = DSA_TQ, DSA_CK
    seq = kvc_ref.shape[1]
    n_sel = float(min(DSA_TOPK, seq // 4))
    nch = i // (ck // tq) + 1
    att_scale = (C_NOPE_DIM + C_ROPE_DIM) ** -0.5
    idx_scale = (IDX_DIM * IDX_HEADS) ** -0.5
    hq = C_HEADS * tq

    row128 = _row_iota((LANES, tq))
    for h in range(C_HEADS):
        cols = slice(h * tq, (h + 1) * tq)
        qn_t = cqn_ref[0, h * C_NOPE_DIM:(h + 1) * C_NOPE_DIM, :]
        qlat_t = _dot(wuk_ref[h], qn_t) * att_scale
        qct_ref[0:C_KV_LATENT, cols] = qlat_t.astype(_BF16)
        qr_t = cqr_ref[0, h * C_ROPE_DIM:(h + 1) * C_ROPE_DIM, :].astype(_F32) * att_scale
        qct_ref[C_KV_LATENT:C_KV_LATENT + C_ROPE_DIM, cols] = qr_t.astype(_BF16)
        qct_ref[C_KV_LATENT + C_ROPE_DIM:, cols] = jnp.zeros((LANES - C_ROPE_DIM, tq), _BF16)
        blk4 = h // 4
        iq_t = iq_ref[0, blk4 * LANES:(blk4 + 1) * LANES, :]
        iqt_ref[:, cols] = jnp.where((row128 // IDX_DIM) == (h % 4), iq_t, jnp.zeros_like(iq_t))

    t_pos = i * tq + _lane_iota((ck, tq))
    key_row = _row_iota((ck, tq))

    def score_body(c, carry):
        vmax, vmin = carry
        off = pl.multiple_of(c * ck, ck)
        rel = _dot(ik_ref[0, pl.ds(off, ck), :], iqt_ref[...])
        score = jnp.zeros((ck, tq), _F32)
        for h in range(IDX_HEADS):
            score = score + jnp.maximum(rel[:, h * tq:(h + 1) * tq], 0.0) * iw_ref[0, h:h + 1, :]
        score = score * idx_scale
        low = jnp.where(off + key_row <= t_pos, score, -jnp.inf)
        sc_ref[c] = low
        vmax = jnp.maximum(vmax, jnp.max(low, axis=0, keepdims=True))
        vmin = jnp.minimum(vmin, jnp.min(score, axis=0, keepdims=True))
        return vmax, vmin

    vmax, vmin = lax.fori_loop(0, nch, score_body,
                               (jnp.full((1, tq), -jnp.inf, _F32), jnp.full((1, tq), jnp.inf, _F32)))

    def count(pred):
        rows_acc = 8 * (8 * LANES // tq)

        slab_row = _row_iota((rows_acc, tq))

        def body(c, acc):
            for r in range(ck // rows_acc):
                rows = slice(r * rows_acc, (r + 1) * rows_acc)
                acc = acc + jnp.where(pred(sc_ref[c, rows, :], c * ck + r * rows_acc + slab_row), 1.0, 0.0)
            return acc

        acc = lax.fori_loop(0, nch, body, jnp.zeros((rows_acc, tq), _F32))
        return jnp.sum(acc, axis=0, keepdims=True)

    def ordered_float(u):
        key = u ^ INT_MIN
        return pltpu.bitcast(key ^ ((key >> 31) & 0x7FFFFFFF), _F32)

    n_valid = (i * tq + _lane_iota((1, tq)) + 1).astype(_F32)
    few = n_valid < n_sel
    c_ge0 = count(lambda sc, pos: sc >= 0.0)
    c_gt0 = count(lambda sc, pos: sc > 0.0)
    zero_tie = (c_gt0 < n_sel) & (c_ge0 >= n_sel)
    above = c_gt0 >= n_sel

    def bisect_body(_, carry):
        lo, hi, cnt_lo = carry
        mid = lo + 0.5 * (hi - lo)
        cnt = count(lambda sc, pos: sc >= mid)
        ok = cnt >= n_sel
        return jnp.where(ok, mid, lo), jnp.where(ok, hi, mid), jnp.where(ok, cnt, cnt_lo)

    def all_resolved(cnt_lo):
        done = few | zero_tie | (cnt_lo == n_sel)
        return jnp.min(jnp.where(done, 1.0, 0.0)) > 0.0

    state = (jnp.where(above, 0.0, vmin), jnp.where(above, vmax, 0.0), jnp.where(above, c_ge0, n_valid))
    state = lax.fori_loop(0, BISECT_STEPS, bisect_body, state)
    state = lax.cond(all_resolved(state[2]), lambda s: s,
                     lambda s: lax.fori_loop(0, BISECT_EXTRA, bisect_body, s), state)
    lo, _, cnt_lo = state

    def exact_search():
        def bit_body(b, carry):
            t, cnt_t = carry
            cand = t | jnp.left_shift(jnp.int32(1), 31 - b)
            cand_f = ordered_float(cand)
            cnt = count(lambda sc, pos: sc >= cand_f)
            ok = cnt >= n_sel
            return jnp.where(ok, cand, t), jnp.where(ok, cnt, cnt_t)

        t0 = jnp.zeros((1, tq), jnp.int32)
        t, cnt_t = lax.fori_loop(0, 32, bit_body, (t0, jnp.full((1, tq), float(seq), _F32)))
        thr = jnp.where(t == 0, -jnp.inf, ordered_float(t))
        return thr, cnt_t, n_sel - count(lambda sc, pos: sc > thr)

    thr, cnt_thr, room = lax.cond(
        all_resolved(cnt_lo),
        lambda: (jnp.where(few, -jnp.inf, jnp.where(zero_tie, 0.0, lo)), jnp.where(zero_tie, c_ge0, cnt_lo),
                 jnp.where(zero_tie, n_sel - c_gt0, n_sel)),
        exact_search)

    excess = jnp.max(jnp.where((cnt_thr > n_sel) & (thr > -jnp.inf), 1.0, 0.0)) > 0.0

    def plain_bias():
        def body(c, _):
            chosen = (sc_ref[c] >= thr) & (c * ck + key_row <= t_pos)
            sc_ref[c] = jnp.where(chosen, 0.0, NEG_BIG)
            return 0

        lax.fori_loop(0, nch, body, 0)

    def tie_bias():
        upto =jnp.where(_lane_iota((ck, ck)) <= _row_iota((ck, ck)), 1.0, 0.0).astype(_BF16)

        def body(c, seen):
            sc = sc_ref[c]
            causal = c * ck + key_row <= t_pos
            tied = (sc == thr) & causal
            one = jnp.where(tied, 1.0, 0.0)
            rank = _dot(upto, one.astype(_BF16)) + seen
            chosen = ((sc > thr) & causal) | (tied & (rank <= room))
            sc_ref[c] = jnp.where(chosen, 0.0, NEG_BIG)
            return seen + jnp.sum(one, axis=0, keepdims=True)

        lax.fori_loop(0, nch, body, jnp.zeros((1, tq), _F32))

    lax.cond(excess, tie_bias, plain_bias)

    acc_ref[...] = jnp.zeros_like(acc_ref)
    grp = 2 * tq
    gcols = [slice(g * grp, (g + 1) * grp) for g in range(hq // grp)]

    def att_body(c, carry):
        off = pl.multiple_of(c * ck, ck)
        kv = kvc_ref[0, pl.ds(off, ck), :]
        bias = sc_ref[c]
        bias2 = jnp.concatenate([bias, bias], axis=1)
        ss = [_dot(kv, qct_ref[:, cols]) + bias2 for cols in gcols]
        ms, ls, es, alphas = [], [], [], []
        for g, cols in enumerate(gcols):
            m, l = carry[0][:, cols], carry[1][:, cols]
            m_new = jnp.maximum(m, jnp.max(ss[g], axis=0, keepdims=True))
            alpha = jnp.exp(m - m_new)
            e = jnp.exp(ss[g] - m_new)
            ls.append(alpha * l + jnp.sum(e, axis=0, keepdims=True))
            ms.append(m_new)
            alphas.append(alpha)
            es.append(e.astype(_BF16))
        kv_t = ckvt_ref[0, c]
        for g, cols in enumerate(gcols):
            acc_ref[:, cols] = alphas[g] * acc_ref[:, cols] + _dot(kv_t, es[g])
        return jnp.concatenate(ms, axis=1), jnp.concatenate(ls, axis=1)

    m0 = jnp.full((1, hq), NEG_BIG, _F32)
    l0 = jnp.zeros((1, hq), _F32)
    _, l = lax.fori_loop(0, nch, att_body, (m0, l0))
    o_lat_t = (acc_ref[...] / l).astype(_BF16)
    outs = [_dot(wuvt_ref[h], o_lat_t[:, h * tq:(h + 1) * tq]) for h in range(C_HEADS)]
    o_ref[0] = jnp.concatenate(outs, axis=0).T.astype(o_ref.dtype)


def _dsa_call(cqn_t, cqr_t, iq_t, iw_t, kvc, ckv_t, ik, w_uk, wuv_t):
    bsz, seq, _ = kvc.shape
    tq = DSA_TQ
    qcol = lambda n: pl.BlockSpec((1, n, tq), lambda b, i: (b, 0, i))
    full = lambda n: pl.BlockSpec((1, seq, n), lambda b, i: (b, 0, 0))
    whole = lambda a: pl.BlockSpec(a.shape, lambda b, i: (0,) * a.ndim)
    width = C_HEADS * C_V_DIM
    return pl.pallas_call(
        _dsa_body,
        out_shape=jax.ShapeDtypeStruct((bsz, seq, width), _BF16),
        grid=(bsz, seq // tq),
        in_specs=[
            qcol(512), qcol(256), qcol(256), qcol(IDX_HEADS), full(256),
            pl.BlockSpec((1,) + ckv_t.shape[1:], lambda b, i: (b, 0, 0, 0)),
            full(128), whole(w_uk), whole(wuv_t),
        ],
        out_specs=pl.BlockSpec((1, tq, width), lambda b, i: (b, i, 0)),
        scratch_shapes=[
            pltpu.VMEM((2 * LANES, C_HEADS * tq), _BF16),
            pltpu.VMEM((LANES, IDX_HEADS * tq), _BF16),
            pltpu.VMEM((seq // DSA_CK, DSA_CK, tq), _F32),
            pltpu.VMEM((C_KV_LATENT, C_HEADS * tq), _F32),
        ],
        compiler_params=_cparams(("arbitrary", "arbitrary")),
        name="dsa",
    )(cqn_t, cqr_t, iq_t, iw_t, kvc, ckv_t, ik, w_uk, wuv_t)


def _sb_body(qt_ref, k_ref, vt_ref, o_ref, acc_ref):
    i = pl.program_id(1)
    t = SB_T
    npair = D_HEADS // 2
    width = 2 * t
    qcats = [_head_pair_rhs(qt_ref[0, p * LANES:(p + 1) * LANES, :], HEAD_DIM) for p in range(npair)]
    key_row = _row_iota((t, width))
    q_lane = _lane_iota((t, width)) % t
    later = jnp.where(_lane_iota((t, t)) > _row_iota((t, t)), 1.0, 0.0).astype(_BF16)
    later2 = jnp.concatenate([later, later], axis=1)
    acc_ref[...] = jnp.zeros_like(acc_ref)

    def blocks(js, rest, masks):
        ks = [k_ref[0, pl.ds(pl.multiple_of(j * t, t), t), :] for j in js]
        zs = [[_dot(kj[:, p * LANES:(p + 1) * LANES], qcats[p]) for p in range(npair)] for kj in ks]
        rests = []
        for p in range(npair):
            run = rest[:, p * width:(p + 1) * width]
            for n, j in enumerate(js):
                z = zs[n][p]
                log_keep = -(jnp.maximum(z, 0.0) + jnp.log(1.0 + jnp.exp(-jnp.abs(z))))
                log_beta = z + log_keep
                if masks[n] is not None:
                    log_keep = jnp.where(masks[n], log_keep, 0.0)
                hi = log_keep.astype(_BF16)
                lo = (log_keep - hi.astype(_F32)).astype(_BF16)
                a = jnp.exp(log_beta + (_dot(later2, jnp.concatenate([hi, lo], axis=0)) + run))
                if masks[n] is not None:
                    a = jnp.where(masks[n], a, 0.0)
                prows = slice(p * LANES, (p + 1) * LANES)
                pv = _dot(vt_ref[0, j, prows, :], a.astype(_BF16))
                for hh in range(2):
                    rows = slice((2 * p + hh) * HEAD_DIM, (2 * p + hh + 1) * HEAD_DIM)
                    acc_ref[rows, :] = acc_ref[rows, :] + pv[hh * HEAD_DIM:(hh + 1) * HEAD_DIM, hh * t:(hh + 1) * t]
                run = run + jnp.sum(log_keep, axis=0, keepdims=True)
            rests.append(run)
        return jnp.concatenate(rests, axis=1)

    exists = lambda n: jnp.broadcast_to(i >= n, (t, width))
    rest = blocks([i, jnp.maximum(i - 1, 0), jnp.maximum(i - 2, 0)], jnp.zeros((1, npair * width), _F32),
                  [key_row < q_lane, exists(1), exists(2)])

    def cond(carry):
        j, _, top = carry
        return (j >= 0) & (top > SB_UNDERFLOW)

    def body(carry):
        j, rest, _ = carry
        second_ok = jnp.broadcast_to(j >= 1, (t, width))
        rest = blocks([j, jnp.maximum(j - 1, 0)], rest, [None, second_ok])
        return j - 2, rest, jnp.max(rest)

    lax.while_loop(cond, body, (i - 3, rest, jnp.max(rest)))
    o_ref[0] = acc_ref[...].T.astype(o_ref.dtype)


def _sb_call(q_t, k, v_t):
    bsz, seq, width = k.shape
    t = SB_T
    return pl.pallas_call(
        _sb_body,
        out_shape=jax.ShapeDtypeStruct((bsz, seq, width), _BF16),
        grid=(bsz, seq // t),
        in_specs=[
            pl.BlockSpec((1, width, t), lambda b, i: (b, 0, i)),
            pl.BlockSpec((1, seq, width), lambda b, i: (b, 0, 0)),
            pl.BlockSpec((1, seq // t, width, t), lambda b, i: (b, 0, 0, 0)),
        ],
        out_specs=pl.BlockSpec((1, t, width), lambda b, i: (b, i, 0)),
        scratch_shapes=[pltpu.VMEM((width, t), _F32)],
        compiler_params=_cparams(("arbitrary", "arbitrary")),
        name="stick_breaking",
    )(q_t, k, v_t)


def _even_layer(x, mod, w_in, sinks, w_out, ln_g, ln_b, rope_h):
    aq_t, ak, bq, bk, av_t, bv, g = _inproj_even_call(x, mod, w_in.astype(_BF16), *rope_h)
    oa = _moba_call(aq_t, ak, av_t)
    ob = _swa_call(bq, bk, bv, sinks)
    return _outproj_call(oa, ob, g, x, mod, w_out, ln_g, ln_b)


def _odd_layer(x, mod, w_in, kv_g, w_uk, w_uv, w_out, ln_g, ln_b, rope_r, rope_i):
    cqn_t, cqr_t, kvc, ckv_t, iq_t, ik, iw_t, dq_t, dk, dv_t, g = _inproj_odd_call(
        x, mod, _prep_w_odd(w_in), rope_r, rope_i, kv_g)
    oc = _dsa_call(cqn_t, cqr_t, iq_t, iw_t, kvc, ckv_t, ik,
                   w_uk.astype(_BF16), w_uv.transpose(0, 2, 1).astype(_BF16))
    od = _sb_call(dq_t, dk, dv_t)
    return _outproj_call(oc, od, g, x, mod, w_out, ln_g, ln_b)


def kernel(x, c, w_ada, b_ada, w_in_even, sink_logits, w_in_odd, kv_norm_g, w_uk, w_uv, w_out, ln_g, ln_b):
    bsz, seq, d = x.shape
    rope_h = _rope_tables(seq, HEAD_DIM, LANES)
    rope_r = _rope_tables(seq, C_ROPE_DIM, LANES)
    rope_i = _rope_tables(seq, IDX_DIM, LANES)
    mods = _ada_call(c, w_ada, b_ada).reshape(DEPTH, bsz, 3, d)
    for layer in range(DEPTH):
        mod = mods[layer]
        j = layer // 2
        if layer % 2 == 0:
            x = _even_layer(x, mod, w_in_even[j], sink_logits[j], w_out[layer], ln_g[layer], ln_b[layer], rope_h)
        else:
            x = _odd_layer(x, mod, w_in_odd[j], kv_norm_g[j], w_uk[j], w_uv[j], w_out[layer],
                           ln_g[layer], ln_b[layer], rope_r, rope_i)
    return x
```
